```python
import math
import jax
import jax.numpy as jnp
from jax import lax
import numpy as np

D_MODEL = 1024
BATCH = 8
SEQ = 4096
DEPTH = 2

HG_HEADS = 8
HG_DK = D_MODEL // HG_HEADS
HG_CHUNK = 64
AT_HEADS = 16
AT_DH = 64
Q_LORA = 384
KV_LORA = 256
IDX_HEADS = 8
IDX_DIM = 64
TOPK_MAX = 256
QBLK = 128
REL_BUCKETS = 32
REL_MAX_DIST = 128
D_FF = 2816
CONV_W = 3
PLE_DIM = 256
EPS = 1e-6
N_A = (DEPTH + 1) // 2
N_B = DEPTH // 2

kernel_name = 'hybrid_hgrn2_dsa_convffn'


def rms_norm(x, g):
    xf = x.astype(jnp.float32)
    y = xf * lax.rsqrt(jnp.mean(xf * xf, axis=-1, keepdims=True) + EPS)
    return (y * g.astype(jnp.float32)).astype(x.dtype)


def rel_bucket(n):
    max_exact = REL_BUCKETS // 2
    nf = jnp.maximum(n, 1).astype(jnp.float32)
    large = max_exact + (jnp.log(nf / max_exact) / math.log(REL_MAX_DIST / max_exact)
                         * (REL_BUCKETS - max_exact)).astype(jnp.int32)
    large = jnp.minimum(large, REL_BUCKETS - 1)
    return jnp.where(n < max_exact, n, large)


def hgrn2_mixer(xn, w_in, lb, onorm, w_out):
    B, L, _ = xn.shape
    proj = xn @ w_in
    q, f, v, g = jnp.split(proj, 4, axis=-1)
    q = jax.nn.silu(q.astype(jnp.float32))
    f = lb + (1.0 - lb) * jax.nn.sigmoid(f.astype(jnp.float32))
    log_f = jnp.log(f)
    k = 1.0 - f
    v = v.astype(jnp.float32)
    nC = L // HG_CHUNK

    def to_chunks(a):
        return a.reshape(B, nC, HG_CHUNK, HG_HEADS, HG_DK).transpose(1, 0, 3, 2, 4)

    tri = jnp.tril(jnp.ones((HG_CHUNK, HG_CHUNK), dtype=bool))

    def step(S, inp):
        qc, kc, vc, lfc = inp
        b = jnp.cumsum(lfc, axis=2)
        diff = b[:, :, :, None, :] - b[:, :, None, :, :]
        decay = jnp.exp(jnp.where(tri[:, :, None], diff, -jnp.inf))
        attn = jnp.einsum('bhtd,bhsd,bhtsd->bhts', qc, kc, decay)
        o = jnp.einsum('bhts,bhsv->bhtv', attn, vc) + jnp.einsum('bhtd,bhdv->bhtv', qc * jnp.exp(b), S)
        b_last = b[:, :, -1, :]
        S = jnp.exp(b_last)[..., None] * S + jnp.einsum(
            'bhsd,bhsv->bhdv', kc * jnp.exp(b_last[:, :, None, :] - b), vc)
        return S, o

    S0 = jnp.zeros((B, HG_HEADS, HG_DK, HG_DK), jnp.float32)
    _, o = lax.scan(step, S0, (to_chunks(q), to_chunks(k), to_chunks(v), to_chunks(log_f)))
    o = o.transpose(1, 0, 3, 2, 4).reshape(B, L, HG_HEADS, HG_DK)
    o = rms_norm(o, onorm.reshape(HG_HEADS, HG_DK)).reshape(B, L, D_MODEL)
    o = o * jax.nn.silu(g.astype(jnp.float32))
    return o.astype(xn.dtype) @ w_out


def dsa_mixer(xn, w_in, q_norm, kv_norm, w_uq, w_uk, w_uv, w_qidx, w_out, rel_bias):
    B, L, _ = xn.shape
    proj = xn @ w_in
    c_q, c_kv, k_idx, w_idx = jnp.split(
        proj, [Q_LORA, Q_LORA + KV_LORA, Q_LORA + KV_LORA + IDX_DIM], axis=-1)
    c_q = rms_norm(c_q, q_norm)
    c_kv = rms_norm(c_kv, kv_norm)
    q_nope = (c_q @ w_uq).reshape(B, L, AT_HEADS, AT_DH)
    q_idx = (c_q @ w_qidx).reshape(B, L, IDX_HEADS, IDX_DIM)
    w_idx = w_idx.astype(jnp.float32) * (IDX_HEADS ** -0.5 * IDX_DIM ** -0.5)
    k_idx = k_idx.astype(jnp.float32)
    topk = max(1, min(TOPK_MAX, L // 4))
    nblk = L // QBLK
    key_pos = jnp.arange(L, dtype=jnp.int32)

    def blocks(a):
        return jnp.moveaxis(a.reshape(B, nblk, QBLK, *a.shape[2:]), 1, 0)

    def attend(inp):
        qn, qi, wi, start = inp
        t = start + jnp.arange(QBLK, dtype=jnp.int32)
        causal = key_pos[None, :] <= t[:, None]
        rel = jax.nn.relu(jnp.einsum('bthd,bsd->bths', qi.astype(jnp.float32), k_idx))
        score = jnp.einsum('bths,bth->bts', rel, wi)
        score = jnp.where(causal[None], score, -jnp.inf)
        _, idx = lax.top_k(score, topk)
        valid = idx <= t[None, :, None]
        c_sel = jax.vmap(lambda c, i: c[i])(c_kv, idx)
        q_lat = jnp.einsum('bthd,hdc->bthc', qn, w_uk)
        logits = jnp.einsum('bthc,btkc->bhtk', q_lat.astype(jnp.float32),
                            c_sel.astype(jnp.float32)) * (AT_DH ** -0.5)
        bucket = rel_bucket(jnp.maximum(t[None, :, None] - idx, 0))
        logits = logits + jnp.moveaxis(rel_bias[bucket].astype(jnp.float32), -1, 1)
        logits = jnp.where(valid[:, None], logits, -jnp.inf)
        probs = jax.nn.softmax(logits, axis=-1).astype(c_sel.dtype)
        o_lat = jnp.einsum('bhtk,btkc->bthc', probs, c_sel)
        return jnp.einsum('bthc,hcd->bthd', o_lat, w_uv)

    starts = jnp.arange(nblk, dtype=jnp.int32) * QBLK
    o = lax.map(attend, (blocks(q_nope), blocks(q_idx), blocks(w_idx), starts))
    o = jnp.moveaxis(o, 0, 1).reshape(B, L, AT_HEADS * AT_DH)
    return o @ w_out


def conv_ffn(xn, w_up, conv_w, conv_b, w_down):
    u = xn @ w_up
    C = u.shape[-1]
    uc = lax.conv_general_dilated(
        u, conv_w.astype(u.dtype).reshape(CONV_W, 1, C), window_strides=(1,),
        padding=[(CONV_W - 1, 0)], dimension_numbers=('NWC', 'WIO', 'NWC'),
        feature_group_count=C) + conv_b.astype(u.dtype)
    gate, val = jnp.split(uc, 2, axis=-1)
    return (jax.nn.silu(gate) * val) @ w_down


def setup_inputs(seed: int = 0) -> dict:
    key = jax.random.key(seed)
    ks = jax.random.split(key, 26)

    def w(k, shape, fan_in):
        return jax.random.normal(k, shape, jnp.float32) * fan_in ** -0.5

    def g(k, shape):
        return 1.0 + 0.05 * jax.random.normal(k, shape, jnp.float32)

    at_in = Q_LORA + KV_LORA + IDX_DIM + IDX_HEADS
    return {
        'x': jax.random.normal(ks[0], (BATCH, SEQ, D_MODEL), jnp.float32),
        'p': jax.random.normal(ks[1], (DEPTH, BATCH, SEQ, PLE_DIM), jnp.float32),
        'hg_norm': g(ks[2], (N_A, D_MODEL)),
        'hg_w_in': w(ks[3], (N_A, D_MODEL, 4 * D_MODEL), D_MODEL),
        'hg_lb': 0.1 * jax.random.normal(ks[4], (DEPTH + 1, D_MODEL), jnp.float32),
        'hg_onorm': g(ks[5], (N_A, D_MODEL)),
        'hg_w_out': w(ks[6], (N_A, D_MODEL, D_MODEL), D_MODEL),
        'at_norm': g(ks[7], (N_B, D_MODEL)),
        'at_w_in': w(ks[8], (N_B, D_MODEL, at_in), D_MODEL),
        'at_q_norm': g(ks[9], (N_B, Q_LORA)),
        'at_kv_norm': g(ks[10], (N_B, KV_LORA)),
        'at_w_uq': w(ks[11], (N_B, Q_LORA, AT_HEADS * AT_DH), Q_LORA),
        'at_w_uk': w(ks[12], (N_B, AT_HEADS, AT_DH, KV_LORA), KV_LORA),
        'at_w_uv': w(ks[13], (N_B, AT_HEADS, KV_LORA, AT_DH), KV_LORA),
        'at_w_qidx': w(ks[14], (N_B, Q_LORA, IDX_HEADS * IDX_DIM), Q_LORA),
        'at_w_out': w(ks[15], (N_B, AT_HEADS * AT_DH, D_MODEL), AT_HEADS * AT_DH),
        'rel_bias': 0.5 * jax.random.normal(ks[16], (REL_BUCKETS, AT_HEADS), jnp.float32),
        'ff_norm': g(ks[17], (DEPTH, D_MODEL)),
        'ff_w_up': w(ks[18], (DEPTH, D_MODEL, 2 * D_FF), D_MODEL),
        'ff_conv_w': w(ks[19], (DEPTH, CONV_W, 2 * D_FF), CONV_W),
        'ff_conv_b': 0.02 * jax.random.normal(ks[20], (DEPTH, 2 * D_FF), jnp.float32),
        'ff_w_down': w(ks[21], (DEPTH, D_FF, D_MODEL), D_FF),
        'ple_norm': g(ks[22], (DEPTH, D_MODEL)),
        'ple_w_gate': w(ks[23], (DEPTH, D_MODEL, D_MODEL), D_MODEL),
        'ple_w_proj': w(ks[24], (DEPTH, PLE_DIM, D_MODEL), PLE_DIM),
        'final_norm': g(ks[25], (D_MODEL,)),
    }


def reference(x, p, hg_norm, hg_w_in, hg_lb, hg_onorm, hg_w_out,
              at_norm, at_w_in, at_q_norm, at_kv_norm, at_w_uq, at_w_uk, at_w_uv,
              at_w_qidx, at_w_out, rel_bias,
              ff_norm, ff_w_up, ff_conv_w, ff_conv_b, ff_w_down,
              ple_norm, ple_w_gate, ple_w_proj, final_norm):
    lb_all = jnp.cumsum(jax.nn.softmax(hg_lb.astype(jnp.float32), axis=0), axis=0)
    h = x
    for i in range(DEPTH):
        j = i // 2
        if i % 2 == 0:
            h = h + hgrn2_mixer(rms_norm(h, hg_norm[j]), hg_w_in[j], lb_all[i],
                                hg_onorm[j], hg_w_out[j]).astype(h.dtype)
        else:
            h = h + dsa_mixer(rms_norm(h, at_norm[j]), at_w_in[j], at_q_norm[j], at_kv_norm[j],
                              at_w_uq[j], at_w_uk[j], at_w_uv[j], at_w_qidx[j], at_w_out[j],
                              rel_bias).astype(h.dtype)
        h = h + conv_ffn(rms_norm(h, ff_norm[i]), ff_w_up[i], ff_conv_w[i], ff_conv_b[i], ff_w_down[i])
        gate = jax.nn.sigmoid(rms_norm(h, ple_norm[i]) @ ple_w_gate[i])
        h = h + gate * (p[i] @ ple_w_proj[i])
    return rms_norm(h, final_norm)
```

```python
import functools
import math

import jax
import jax.numpy as jnp
import numpy as np
from jax import lax
from jax.experimental import pallas as pl
from jax.experimental.pallas import tpu as pltpu

F32 = jnp.float32
BF16 = jnp.bfloat16
I32 = jnp.int32

D_MODEL = 1024
HG_HEADS = 8
HG_DK = D_MODEL // HG_HEADS
AT_HEADS = 16
AT_DH = 64
Q_LORA = 384
KV_LORA = 256
IDX_HEADS = 8
IDX_DIM = 64
TOPK_MAX = 256
QBLK = 128
REL_BUCKETS = 32
REL_MAX_DIST = 128
D_FF = 2816
PLE_DIM = 256
EPS = 1e-6

VMEM_LIMIT_V7X = 56 * 1024 * 1024
SUB = 16
KEY_TILE = 256
NEG_BIG = -1e30
INT_MIN = -(2 ** 31)


def _cparams(sem):
    return pltpu.CompilerParams(dimension_semantics=sem, vmem_limit_bytes=VMEM_LIMIT_V7X)


def _rms(x, g):
    return x * lax.rsqrt(jnp.mean(x * x, axis=-1, keepdims=True) + EPS) * g


def _dot(a, b):
    return jnp.dot(a, b, preferred_element_type=F32)


def _dot_nt(a, b):
    return lax.dot_general(a, b, (((1,), (1,)), ((), ())), preferred_element_type=F32)


def _dot_tn(a, b):
    return lax.dot_general(a, b, (((0,), (0,)), ((), ())), preferred_element_type=F32)


def _full(shape):
    n = len(shape)
    return pl.BlockSpec(shape, lambda *_: (0,) * n)


def _hg_in_kernel(layer, x_ref, g_ref, lbp_ref, w_ref, q_ref, k_ref, v_ref, gs_ref, lf_ref):
    D = D_MODEL
    xn = _rms(x_ref[...], g_ref[...]).astype(BF16)
    lbp = lbp_ref[...]
    e = jnp.exp(lbp - jnp.max(lbp, axis=0, keepdims=True))
    lb = jnp.sum(e[: layer + 1], axis=0, keepdims=True) / jnp.sum(e, axis=0, keepdims=True)
    q = _dot(xn, w_ref[:, 0:D])
    q_ref[...] = (q * jax.nn.sigmoid(q)).astype(BF16)
    f = lb + (1.0 - lb) * jax.nn.sigmoid(_dot(xn, w_ref[:, D:2 * D]))
    lf_ref[...] = jnp.log(f)
    k_ref[...] = (1.0 - f).astype(BF16)
    v_ref[...] = _dot(xn, w_ref[:, 2 * D:3 * D]).astype(BF16)
    g = _dot(xn, w_ref[:, 3 * D:4 * D])
    gs_ref[...] = (g * jax.nn.sigmoid(g)).astype(BF16)


def _hg_in(x, g, lbp, w, layer, tm=512):
    T, D = x.shape
    tok = pl.BlockSpec((tm, D), lambda i: (i, 0))
    bf = jax.ShapeDtypeStruct((T, D), BF16)
    return pl.pallas_call(
        functools.partial(_hg_in_kernel, layer),
        grid=(T // tm,),
        in_specs=[tok, _full((1, D)), _full(lbp.shape), _full(w.shape)],
        out_specs=[tok] * 5,
        out_shape=[bf, bf, bf, bf, jax.ShapeDtypeStruct((T, D), F32)],
        compiler_params=_cparams(("parallel",)),
        name="hg_in",
    )(x, g, lbp, w)


def _hg_rec_kernel(lt, q_ref, k_ref, v_ref, gs_ref, lf_ref, x_ref, on_ref, wo_ref, tri_ref,
                   out_ref, st_ref, b_ref, o_ref):
    H, DK = HG_HEADS, HG_DK

    @pl.when(pl.program_id(1) == 0)
    def _():
        st_ref[...] = jnp.zeros_like(st_ref)

    lf = lf_ref[...]
    hi = lf.astype(BF16)
    r1 = lf - hi.astype(F32)
    mid = r1.astype(BF16)
    lo = (r1 - mid.astype(F32)).astype(BF16)
    tri = tri_ref[...]
    b_ref[...] = _dot(tri, hi) + _dot(tri, mid) + _dot(tri, lo)

    row = lax.broadcasted_iota(I32, (SUB, DK), 0)

    def sub_chunk(c, carry):
        r0 = pl.multiple_of(c * SUB, SUB)
        for h in range(H):
            ls = slice(h * DK, (h + 1) * DK)
            b = b_ref[pl.ds(r0, SUB), ls]
            q = q_ref[pl.ds(r0, SUB), ls].astype(F32)
            k = k_ref[pl.ds(r0, SUB), ls].astype(F32)
            v = v_ref[pl.ds(r0, SUB), ls].astype(F32)
            be = b[SUB - 1:SUB, :]
            st = st_ref[h]
            o = _dot_nt((q * jnp.exp(b)).astype(BF16), st.astype(BF16))
            for s in range(SUB):
                e = jnp.exp(jnp.where(row >= s, b - b[s:s + 1, :], NEG_BIG))
                a = jnp.sum(q * e * k[s:s + 1, :], axis=-1, keepdims=True)
                o = o + a * v[s:s + 1, :]
            o_ref[pl.ds(r0, SUB), ls] = o
            kd = (k * jnp.exp(be - b)).astype(BF16)
            st_ref[h] = st * jnp.exp(be) + _dot_tn(v.astype(BF16), kd)
        return carry

    lax.fori_loop(0, lt // SUB, sub_chunk, 0)

    on = on_ref[...]
    for h in range(H):
        ls = slice(h * DK, (h + 1) * DK)
        oh = _rms(o_ref[:, ls], on[:, ls]) * gs_ref[:, ls].astype(F32)
        o_ref[:, ls] = oh
    out_ref[...] = x_ref[...] + _dot(o_ref[...].astype(BF16), wo_ref[...])


def _hg_rec(q, k, v, gs, lf, x, onorm, wo, B, L, lt=256):
    T, D = x.shape
    nl = L // lt
    tok = pl.BlockSpec((lt, D), lambda b, j: (b * nl + j, 0))
    tri = (np.arange(lt)[:, None] >= np.arange(lt)[None, :]) & (
        np.arange(lt)[:, None] // SUB == np.arange(lt)[None, :] // SUB)
    tri = jnp.asarray(tri, BF16)
    return pl.pallas_call(
        functools.partial(_hg_rec_kernel, lt),
        grid=(B, nl),
        in_specs=[tok] * 6 + [_full((1, D)), _full((D, D)), _full((lt, lt))],
        out_specs=tok,
        out_shape=jax.ShapeDtypeStruct((T, D), F32),
        scratch_shapes=[pltpu.VMEM((HG_HEADS, HG_DK, HG_DK), F32),
                        pltpu.VMEM((lt, D), F32),
                        pltpu.VMEM((lt, D), F32)],
        compiler_params=_cparams(("arbitrary", "arbitrary")),
        name="hg_rec",
    )(q, k, v, gs, lf, x, onorm, wo, tri)


def _ffn_kernel(nseq, h_ref, g_ref, wg_ref, wv_ref, cwg_ref, cwv_ref, cbg_ref, cbv_ref, wd_ref,
                out_ref, xn_ref, cg_ref, cv_ref):
    i, j = pl.program_id(0), pl.program_id(1)

    @pl.when(j == 0)
    def _():
        xn_ref[...] = _rms(h_ref[...], g_ref[...]).astype(BF16)
        out_ref[...] = h_ref[...]

    seq_start = (i % nseq) == 0
    xn = xn_ref[...]

    def conv(u, w_ref, b_ref, carry_ref):
        fc = u.shape[1]
        carry = jnp.where(seq_start, 0.0, carry_ref[j])
        carry_ref[j] = u[u.shape[0] - 8:, :]
        r1 = pltpu.roll(u, 1, 0)
        r2 = pltpu.roll(u, 2, 0)
        rows = lax.broadcasted_iota(I32, (8, fc), 0)
        top1 = jnp.where(rows < 1, pltpu.roll(carry, 1, 0), r1[0:8])
        top2 = jnp.where(rows < 2, pltpu.roll(carry, 2, 0), r2[0:8])
        p1 = jnp.concatenate([top1, r1[8:]], axis=0)
        p2 = jnp.concatenate([top2, r2[8:]], axis=0)
        w = w_ref[0]
        return u * w[2:3, :] + p1 * w[1:2, :] + p2 * w[0:1, :] + b_ref[0]

    ug = conv(_dot(xn, wg_ref[...]), cwg_ref, cbg_ref, cg_ref)
    uv = conv(_dot(xn, wv_ref[...]), cwv_ref, cbv_ref, cv_ref)
    act = (ug * jax.nn.sigmoid(ug) * uv).astype(BF16)
    out_ref[...] += _dot(act, wd_ref[...])


def _ffn(h, g, w_up, conv_w, conv_b, w_down, L, tm=512, nf=2):
    T, D = h.shape
    F = w_down.shape[0]
    fc = F // nf
    assert fc % 128 == 0 and L % tm == 0
    cw = jnp.pad(conv_w, ((0, 5), (0, 0)))
    cwg = cw[:, :F].reshape(8, nf, fc).transpose(1, 0, 2)
    cwv = cw[:, F:].reshape(8, nf, fc).transpose(1, 0, 2)
    cbg = conv_b[:F].reshape(nf, 1, fc)
    cbv = conv_b[F:].reshape(nf, 1, fc)
    tok = pl.BlockSpec((tm, D), lambda i, j: (i, 0))
    return pl.pallas_call(
        functools.partial(_ffn_kernel, L // tm),
        grid=(T // tm, nf),
        in_specs=[tok, _full((1, D)),
                  pl.BlockSpec((D, fc), lambda i, j: (0, j)),
                  pl.BlockSpec((D, fc), lambda i, j: (0, nf + j)),
                  pl.BlockSpec((1, 8, fc), lambda i, j: (j, 0, 0)),
                  pl.BlockSpec((1, 8, fc), lambda i, j: (j, 0, 0)),
                  pl.BlockSpec((1, 1, fc), lambda i, j: (j, 0, 0)),
                  pl.BlockSpec((1, 1, fc), lambda i, j: (j, 0, 0)),
                  pl.BlockSpec((fc, D), lambda i, j: (j, 0))],
        out_specs=tok,
        out_shape=jax.ShapeDtypeStruct((T, D), F32),
        scratch_shapes=[pltpu.VMEM((tm, D), BF16),
                        pltpu.VMEM((nf, 8, fc), F32),
                        pltpu.VMEM((nf, 8, fc), F32)],
        compiler_params=_cparams(("arbitrary", "arbitrary")),
        name="conv_ffn",
    )(h, g, w_up, w_up, cwg, cwv, cbg, cbv, w_down)


def _ple_kernel(final, h_ref, p_ref, g_ref, wg_ref, wp_ref, fn_ref, out_ref):
    h = h_ref[...]
    gate = jax.nn.sigmoid(_dot(_rms(h, g_ref[...]).astype(BF16), wg_ref[...]))
    h = h + gate * _dot(p_ref[...].astype(BF16), wp_ref[...])
    if final:
        h = _rms(h, fn_ref[...])
    out_ref[...] = h


def _ple(h, p, g, wg, wp, fnorm, final, tm=512):
    T, D = h.shape
    tok = pl.BlockSpec((tm, D), lambda i: (i, 0))
    return pl.pallas_call(
        functools.partial(_ple_kernel, final),
        grid=(T // tm,),
        in_specs=[tok, pl.BlockSpec((tm, PLE_DIM), lambda i: (i, 0)), _full((1, D)),
                  _full((D, D)), _full((PLE_DIM, D)), _full((1, D))],
        out_specs=tok,
        out_shape=jax.ShapeDtypeStruct((T, D), F32),
        compiler_params=_cparams(("parallel",)),
        name="ple",
    )(h, p, g, wg, wp, fnorm)


def _at_in_kernel(h_ref, g_ref, wcq_ref, wckv_ref, wki_ref, wwi_ref, qn_ref, kvn_ref,
                  wuq_ref, wqi_ref, wuk_ref,
                  qlat_ref, qit_ref, wit_ref, ki_ref, ckv_ref, ckvt_ref):
    tm = h_ref.shape[0]
    xn = _rms(h_ref[...], g_ref[...]).astype(BF16)
    cq = _rms(_dot(xn, wcq_ref[...]), qn_ref[...]).astype(BF16)
    ckv = _rms(_dot(xn, wckv_ref[...]), kvn_ref[...])
    ckv_ref[...] = ckv.astype(BF16)
    for c in range(tm // KEY_TILE):
        ckvt_ref[c] = ckv[c * KEY_TILE:(c + 1) * KEY_TILE, :].T.astype(BF16)
    ki_ref[...] = _dot(xn, wki_ref[...]).astype(BF16)
    wit_ref[...] = _dot_nt(wwi_ref[...], xn) * (IDX_HEADS ** -0.5 * IDX_DIM ** -0.5)
    qit_ref[...] = _dot_nt(wqi_ref[...], cq).astype(BF16)
    qn = _dot(cq, wuq_ref[...]).astype(BF16)
    for pr in range(AT_HEADS // 2):
        ql = _dot(qn[:, pr * 128:(pr + 1) * 128], wuk_ref[pr]) * (AT_DH ** -0.5)
        qlat_ref[2 * pr] = ql[:, :KV_LORA].astype(BF16)
        qlat_ref[2 * pr + 1] = ql[:, KV_LORA:].astype(BF16)


def _at_in(h, g, wcq, wckv, wki, wwi_t, qn, kvn, wuq, wqi_t, wuk_bd, tm=512):
    T, D = h.shape
    nk = tm // KEY_TILE
    ins = [h, g, wcq, wckv, wki, wwi_t, qn, kvn, wuq, wqi_t, wuk_bd]
    in_specs = [pl.BlockSpec((tm, D), lambda i: (i, 0))] + [_full(a.shape) for a in ins[1:]]
    return pl.pallas_call(
        _at_in_kernel,
        grid=(T // tm,),
        in_specs=in_specs,
        out_specs=[pl.BlockSpec((AT_HEADS, tm, KV_LORA), lambda i: (0, i, 0)),
                   pl.BlockSpec((IDX_HEADS * IDX_DIM, tm), lambda i: (0, i)),
                   pl.BlockSpec((IDX_HEADS, tm), lambda i: (0, i)),
                   pl.BlockSpec((tm, IDX_DIM), lambda i: (i, 0)),
                   pl.BlockSpec((tm, KV_LORA), lambda i: (i, 0)),
                   pl.BlockSpec((nk, KV_LORA, KEY_TILE), lambda i: (i, 0, 0))],
        out_shape=[jax.ShapeDtypeStruct((AT_HEADS, T, KV_LORA), BF16),
                   jax.ShapeDtypeStruct((IDX_HEADS * IDX_DIM, T), BF16),
                   jax.ShapeDtypeStruct((IDX_HEADS, T), F32),
                   jax.ShapeDtypeStruct((T, IDX_DIM), BF16),
                   jax.ShapeDtypeStruct((T, KV_LORA), BF16),
                   jax.ShapeDtypeStruct((T // KEY_TILE, KV_LORA, KEY_TILE), BF16)],
        compiler_params=_cparams(("parallel",)),
        name="at_in",
    )(*ins)


def _at_core_kernel(topk, qlat_ref, qit_ref, wit_ref, ki_ref, ckv_ref, ckvt_ref, bias_ref,
                    wuv_ref, wo_ref, h_ref, out_ref, ik_ref, mask_ref, acc_ref, m_ref, l_ref):
    H, TK = AT_HEADS, KEY_TILE
    qb = pl.program_id(1)
    t0 = qb * QBLK
    nkt = qb // (TK // QBLK) + 1
    lane_q = lax.broadcasted_iota(I32, (TK, QBLK), 1)
    row_k = lax.broadcasted_iota(I32, (TK, QBLK), 0)

    wit = wit_ref[...]

    def score_tile(kt, carry):
        k0 = pl.multiple_of(kt * TK, TK)
        kk = ki_ref[pl.ds(k0, TK), :]
        acc = jnp.zeros((TK, 2 * QBLK), F32)
        for pr in range(IDX_HEADS // 2):
            qa = qit_ref[(2 * pr) * IDX_DIM:(2 * pr + 1) * IDX_DIM, :]
            qb_ = qit_ref[(2 * pr + 1) * IDX_DIM:(2 * pr + 2) * IDX_DIM, :]
            rel = jnp.maximum(_dot(kk, jnp.concatenate([qa, qb_], axis=1)), 0.0)
            w2 = jnp.concatenate([wit[2 * pr:2 * pr + 1, :], wit[2 * pr + 1:2 * pr + 2, :]], axis=1)
            acc = acc + rel * w2
        score = acc[:, :QBLK] + acc[:, QBLK:] + 0.0
        bits = pltpu.bitcast(score, I32)
        ikey = bits ^ ((bits >> 31) & 0x7FFFFFFF)
        causal = (k0 + row_k) <= (t0 + lane_q)
        ik_ref[pl.ds(k0, TK), :] = jnp.where(causal, ikey, INT_MIN)
        return carry

    lax.fori_loop(0, nkt, score_tile, 0)

    def count(pred):
        def body(kt, c):
            k0 = pl.multiple_of(kt * TK, TK)
            hit = jnp.where(pred(ik_ref[pl.ds(k0, TK), :], k0), 1, 0)
            return c + jnp.sum(hit.reshape(TK // 8, 8, QBLK), axis=0)
        c = lax.fori_loop(0, nkt, body, jnp.zeros((8, QBLK), I32))
        return jnp.sum(c, axis=0, keepdims=True)

    def bit_step(i, tau):
        cand = jnp.where(i == 0, jnp.zeros_like(tau), tau | (1 << (31 - i)))
        return jnp.where(count(lambda x, k0: x >= cand) >= topk, cand, tau)

    tau = lax.fori_loop(0, 32, bit_step, jnp.full((1, QBLK), INT_MIN, I32))
    n_gt = count(lambda x, k0: x > tau)
    n_ge = count(lambda x, k0: x >= tau)
    need = topk - n_gt

    def tie_search(_):
        def step(i, jc):
            cand = jc + (1 << (12 - i))
            n = count(lambda x, k0: (x == tau) & ((k0 + row_k) < cand))
            return jnp.where(n <= need, cand, jc)
        return lax.fori_loop(0, 13, step, jnp.zeros((1, QBLK), I32))

    jcut = lax.cond(jnp.max(n_ge) > topk, tie_search,
                    lambda _: jnp.full((1, QBLK), 2 ** 13, I32), 0)

    def mask_tile(kt, carry):
        k0 = pl.multiple_of(kt * TK, TK)
        x = ik_ref[pl.ds(k0, TK), :]
        kidx = k0 + row_k
        sel = ((x > tau) | ((x == tau) & (kidx < jcut))) & (kidx <= (t0 + lane_q))
        am = jnp.where(sel, 0.0, NEG_BIG)
        mask_ref[kt] = jnp.concatenate(
            [am[c * QBLK:(c + 1) * QBLK, :].T for c in range(TK // QBLK)], axis=1)
        return carry

    lax.fori_loop(0, nkt, mask_tile, 0)

    m_ref[...] = jnp.full_like(m_ref, NEG_BIG)
    l_ref[...] = jnp.zeros_like(l_ref)
    acc_ref[...] = jnp.zeros_like(acc_ref)
    qall = qlat_ref[...].reshape(H * QBLK, KV_LORA)

    def attn_tile(kt, carry):
        k0 = pl.multiple_of(kt * TK, TK)
        bidx = jnp.minimum((t0 - k0) // QBLK, bias_ref.shape[0] - 1)
        lg = _dot(qall, ckvt_ref[kt]).reshape(H, QBLK, TK) + bias_ref[bidx] + mask_ref[kt][None]
        m_old = m_ref[...]
        m_new = jnp.maximum(m_old, jnp.max(lg, axis=-1, keepdims=True))
        alpha = jnp.exp(m_old - m_new)
        p = jnp.exp(lg - m_new)
        l_ref[...] = alpha * l_ref[...] + jnp.sum(p, axis=-1, keepdims=True)
        m_ref[...] = m_new
        pv = _dot(p.astype(BF16).reshape(H * QBLK, TK), ckv_ref[pl.ds(k0, TK), :])
        acc_ref[...] = alpha * acc_ref[...] + pv.reshape(H, QBLK, KV_LORA)
        return carry

    lax.fori_loop(0, nkt, attn_tile, 0)

    olat = (acc_ref[...] / l_ref[...]).astype(BF16)
    o = jnp.concatenate(
        [_dot(jnp.concatenate([olat[2 * pr], olat[2 * pr + 1]], axis=1), wuv_ref[pr])
         for pr in range(H // 2)], axis=1)
    out_ref[...] = h_ref[...] + _dot(o.astype(BF16), wo_ref[...])


def _at_core(qlat, qit, wit, ki, ckv, ckvt, bias, wuv_bd, wo, h, B, L, topk):
    T, D = h.shape
    nblk = L // QBLK
    nkt = L // KEY_TILE
    H = AT_HEADS
    return pl.pallas_call(
        functools.partial(_at_core_kernel, topk),
        grid=(B, nblk),
        in_specs=[pl.BlockSpec((H, QBLK, KV_LORA), lambda b, q: (0, b * nblk + q, 0)),
                  pl.BlockSpec((IDX_HEADS * IDX_DIM, QBLK), lambda b, q: (0, b * nblk + q)),
                  pl.BlockSpec((IDX_HEADS, QBLK), lambda b, q: (0, b * nblk + q)),
                  pl.BlockSpec((L, IDX_DIM), lambda b, q: (b, 0)),
                  pl.BlockSpec((L, KV_LORA), lambda b, q: (b, 0)),
                  pl.BlockSpec((nkt, KV_LORA, KEY_TILE), lambda b, q: (b, 0, 0)),
                  _full(bias.shape), _full(wuv_bd.shape), _full(wo.shape),
                  pl.BlockSpec((QBLK, D), lambda b, q: (b * nblk + q, 0))],
        out_specs=pl.BlockSpec((QBLK, D), lambda b, q: (b * nblk + q, 0)),
        out_shape=jax.ShapeDtypeStruct((T, D), F32),
        scratch_shapes=[pltpu.VMEM((L, QBLK), I32),
                        pltpu.VMEM((nkt, QBLK, KEY_TILE), F32),
                        pltpu.VMEM((H, QBLK, KV_LORA), F32),
                        pltpu.VMEM((H, QBLK, 1), F32),
                        pltpu.VMEM((H, QBLK, 1), F32)],
        compiler_params=_cparams(("arbitrary", "arbitrary")),
        name="at_core",
    )(qlat, qit, wit, ki, ckv, ckvt, bias, wuv_bd, wo, h)


def _rel_bucket(n):
    max_exact = REL_BUCKETS // 2
    nf = jnp.maximum(n, 1).astype(F32)
    large = max_exact + (jnp.log(nf / max_exact) / math.log(REL_MAX_DIST / max_exact)
                         * (REL_BUCKETS - max_exact)).astype(I32)
    large = jnp.minimum(large, REL_BUCKETS - 1)
    return jnp.where(n < max_exact, n, large)


def _bias_tiles(rel_bias):
    nd = (KEY_TILE + REL_MAX_DIST) // QBLK + 1
    d = jnp.arange(nd, dtype=I32)[:, None, None]
    j = jnp.arange(QBLK, dtype=I32)[None, :, None]
    i = jnp.arange(KEY_TILE, dtype=I32)[None, None, :]
    n = jnp.maximum(d * QBLK + j - i, 0)
    n = jnp.where(d == nd - 1, jnp.maximum(n, REL_MAX_DIST), n)
    tab = rel_bias[_rel_bucket(n)]
    return jnp.moveaxis(tab, -1, 1).astype(F32)


def kernel(x, p, hg_norm, hg_w_in, hg_lb, hg_onorm, hg_w_out, at_norm, at_w_in, at_q_norm,
           at_kv_norm, at_w_uq, at_w_uk, at_w_uv, at_w_qidx, at_w_out, rel_bias, ff_norm, ff_w_up,
           ff_conv_w, ff_conv_b, ff_w_down, ple_norm, ple_w_gate, ple_w_proj, final_norm):
    B, L, D = x.shape
    T = B * L
    depth = p.shape[0]
    row = lambda a: a.reshape(1, -1)
    h = x.reshape(T, D)
    for i in range(depth):
        j = i // 2
        if i % 2 == 0:
            q, k, v, gs, lf = _hg_in(h, row(hg_norm[j]), hg_lb, hg_w_in[j].astype(BF16), i)
            h = _hg_rec(q, k, v, gs, lf, h, row(hg_onorm[j]), hg_w_out[j].astype(BF16), B, L)
        else:
            w_in = at_w_in[j].astype(BF16)
            o1, o2, o3 = Q_LORA, Q_LORA + KV_LORA, Q_LORA + KV_LORA + IDX_DIM
            wuk = at_w_uk[j].astype(BF16)
            z = jnp.zeros_like(wuk[0::2])
            wuk_bd = jnp.concatenate([jnp.concatenate([wuk[0::2], z], axis=2),
                                      jnp.concatenate([z, wuk[1::2]], axis=2)], axis=1)
            wuv = at_w_uv[j].astype(BF16)
            zv = jnp.zeros_like(wuv[0::2])
            wuv_bd = jnp.concatenate([jnp.concatenate([wuv[0::2], zv], axis=2),
                                      jnp.concatenate([zv, wuv[1::2]], axis=2)], axis=1)
            qlat, qit, wit, ki, ckv, ckvt = _at_in(
                h, row(at_norm[j]), w_in[:, :o1], w_in[:, o1:o2], w_in[:, o2:o3], w_in[:, o3:].T,
                row(at_q_norm[j]), row(at_kv_norm[j]), at_w_uq[j].astype(BF16),
                at_w_qidx[j].astype(BF16).T, wuk_bd)
            topk = max(1, min(TOPK_MAX, L // 4))
            h = _at_core(qlat, qit, wit, ki, ckv, ckvt, _bias_tiles(rel_bias), wuv_bd,
                         at_w_out[j].astype(BF16), h, B, L, topk)
        h = _ffn(h, row(ff_norm[i]), ff_w_up[i].astype(BF16), ff_conv_w[i], ff_conv_b[i],
                 ff_w_down[i].astype(BF16), L)
        h = _ple(h, p[i].reshape(T, PLE_DIM), row(ple_norm[i]), ple_w_gate[i].astype(BF16),
                 ple_w_proj[i].astype(BF16), row(final_norm), final=(i == depth - 1))
    return h.reshape(B, L, D)
```

```python
import functools
import math

import jax
import jax.numpy as jnp
import numpy as np
from jax import lax
from jax.experimental import pallas as pl
from jax.experimental.pallas import tpu as pltpu

F32 = jnp.float32
BF16 = jnp.bfloat16
I32 = jnp.int32

D_MODEL = 1024
HG_HEADS = 8
HG_DK = D_MODEL // HG_HEADS
AT_HEADS = 16
AT_DH = 64
Q_LORA = 384
KV_LORA = 256
IDX_HEADS = 8
IDX_DIM = 64
TOPK_MAX = 256
QBLK = 128
REL_BUCKETS = 32
REL_MAX_DIST = 128
D_FF = 2816
PLE_DIM = 256
EPS = 1e-6

VMEM_LIMIT_V7X = 56 * 1024 * 1024
SUB = 16
KEY_TILE = 256
NEAR_TILES = (KEY_TILE + REL_MAX_DIST) // QBLK
NEG_BIG = -1e30
INT_MIN = -(2 ** 31)


def _cparams(sem):
    return pltpu.CompilerParams(dimension_semantics=sem, vmem_limit_bytes=VMEM_LIMIT_V7X)


def _rms(x, g):
    return x * lax.rsqrt(jnp.mean(x * x, axis=-1, keepdims=True) + EPS) * g


def _dot(a, b):
    return jnp.dot(a, b, preferred_element_type=F32)


def _dot_nt(a, b):
    return lax.dot_general(a, b, (((1,), (1,)), ((), ())), preferred_element_type=F32)


def _dot_tn(a, b):
    return lax.dot_general(a, b, (((0,), (0,)), ((), ())), preferred_element_type=F32)


def _full(shape, single=False):
    n = len(shape)
    if single:
        return pl.BlockSpec(shape, lambda *_: (0,) * n, pipeline_mode=pl.Buffered(1))
    return pl.BlockSpec(shape, lambda *_: (0,) * n)


def _hg_in_kernel(layer, x_ref, g_ref, lbp_ref, w_ref, q_ref, k_ref, v_ref, gs_ref, lf_ref):
    D = D_MODEL
    xn = _rms(x_ref[...], g_ref[...]).astype(BF16)
    lbp = lbp_ref[...]
    e = jnp.exp(lbp - jnp.max(lbp, axis=0, keepdims=True))
    lb = jnp.sum(e[: layer + 1], axis=0, keepdims=True) / jnp.sum(e, axis=0, keepdims=True)
    q = _dot(xn, w_ref[:, 0:D])
    q_ref[...] = (q * jax.nn.sigmoid(q)).astype(BF16)
    f = lb + (1.0 - lb) * jax.nn.sigmoid(_dot(xn, w_ref[:, D:2 * D]))
    lf_ref[...] = jnp.log(f)
    k_ref[...] = (1.0 - f).astype(BF16)
    v_ref[...] = _dot(xn, w_ref[:, 2 * D:3 * D]).astype(BF16)
    g = _dot(xn, w_ref[:, 3 * D:4 * D])
    gs_ref[...] = (g * jax.nn.sigmoid(g)).astype(BF16)


def _hg_in(x, g, lbp, w, layer, tm=512):
    T, D = x.shape
    tok = pl.BlockSpec((tm, D), lambda i: (i, 0))
    bf = jax.ShapeDtypeStruct((T, D), BF16)
    return pl.pallas_call(
        functools.partial(_hg_in_kernel, layer),
        grid=(T // tm,),
        in_specs=[tok, _full((1, D)), _full(lbp.shape), _full(w.shape)],
        out_specs=[tok] * 5,
        out_shape=[bf, bf, bf, bf, jax.ShapeDtypeStruct((T, D), F32)],
        compiler_params=_cparams(("parallel",)),
        name="hg_in",
    )(x, g, lbp, w)


def _hg_rec_kernel(lt, q_ref, k_ref, v_ref, gs_ref, lf_ref, x_ref, on_ref, wo_ref, tri_ref,
                   out_ref, st_ref, b_ref, o_ref):
    H, DK = HG_HEADS, HG_DK

    @pl.when(pl.program_id(1) == 0)
    def _():
        st_ref[...] = jnp.zeros_like(st_ref)

    lf = lf_ref[...]
    hi = lf.astype(BF16)
    r1 = lf - hi.astype(F32)
    mid = r1.astype(BF16)
    lo = (r1 - mid.astype(F32)).astype(BF16)
    tri = tri_ref[...]
    b_ref[...] = _dot(tri, hi) + _dot(tri, mid) + _dot(tri, lo)

    row = lax.broadcasted_iota(I32, (SUB, DK), 0)

    def sub_chunk(c, carry):
        r0 = pl.multiple_of(c * SUB, SUB)
        for h in range(H):
            ls = slice(h * DK, (h + 1) * DK)
            b = b_ref[pl.ds(r0, SUB), ls]
            q = q_ref[pl.ds(r0, SUB), ls].astype(F32)
            k = k_ref[pl.ds(r0, SUB), ls].astype(F32)
            v = v_ref[pl.ds(r0, SUB), ls].astype(F32)
            be = b[SUB - 1:SUB, :]
            st = st_ref[h]
            o = _dot_nt((q * jnp.exp(b)).astype(BF16), st.astype(BF16))
            for s in range(SUB):
                e = jnp.exp(jnp.where(row >= s, b - b[s:s + 1, :], NEG_BIG))
                a = jnp.sum(q * e * k[s:s + 1, :], axis=-1, keepdims=True)
                o = o + a * v[s:s + 1, :]
            o_ref[pl.ds(r0, SUB), ls] = o
            kd = (k * jnp.exp(be - b)).astype(BF16)
            st_ref[h] = st * jnp.exp(be) + _dot_tn(v.astype(BF16), kd)
        return carry

    lax.fori_loop(0, lt // SUB, sub_chunk, 0)

    on = on_ref[...]
    for h in range(H):
        ls = slice(h * DK, (h + 1) * DK)
        oh = _rms(o_ref[:, ls], on[:, ls]) * gs_ref[:, ls].astype(F32)
        o_ref[:, ls] = oh
    out_ref[...] = x_ref[...] + _dot(o_ref[...].astype(BF16), wo_ref[...])


def _hg_rec(q, k, v, gs, lf, x, onorm, wo, B, L, lt=256):
    T, D = x.shape
    nl = L // lt
    tok = pl.BlockSpec((lt, D), lambda b, j: (b * nl + j, 0))
    tri = (np.arange(lt)[:, None] >= np.arange(lt)[None, :]) & (
        np.arange(lt)[:, None] // SUB == np.arange(lt)[None, :] // SUB)
    tri = jnp.asarray(tri, BF16)
    return pl.pallas_call(
        functools.partial(_hg_rec_kernel, lt),
        grid=(B, nl),
        in_specs=[tok] * 6 + [_full((1, D)), _full((D, D)), _full((lt, lt))],
        out_specs=tok,
        out_shape=jax.ShapeDtypeStruct((T, D), F32),
        scratch_shapes=[pltpu.VMEM((HG_HEADS, HG_DK, HG_DK), F32),
                        pltpu.VMEM((lt, D), F32),
                        pltpu.VMEM((lt, D), F32)],
        compiler_params=_cparams(("arbitrary", "arbitrary")),
        name="hg_rec",
    )(q, k, v, gs, lf, x, onorm, wo, tri)


def _ffn_kernel(nseq, h_ref, g_ref, wg_ref, wv_ref, cwg_ref, cwv_ref, cbg_ref, cbv_ref, wd_ref,
                out_ref, xn_ref, cg_ref, cv_ref):
    i, j = pl.program_id(0), pl.program_id(1)

    @pl.when(j == 0)
    def _():
        xn_ref[...] = _rms(h_ref[...], g_ref[...]).astype(BF16)
        out_ref[...] = h_ref[...]

    seq_start = (i % nseq) == 0
    xn = xn_ref[...]

    def conv(u, w_ref, b_ref, carry_ref):
        fc = u.shape[1]
        carry = jnp.where(seq_start, 0.0, carry_ref[j])
        carry_ref[j] = u[u.shape[0] - 8:, :]
        r1 = pltpu.roll(u, 1, 0)
        r2 = pltpu.roll(u, 2, 0)
        rows = lax.broadcasted_iota(I32, (8, fc), 0)
        top1 = jnp.where(rows < 1, pltpu.roll(carry, 1, 0), r1[0:8])
        top2 = jnp.where(rows < 2, pltpu.roll(carry, 2, 0), r2[0:8])
        p1 = jnp.concatenate([top1, r1[8:]], axis=0)
        p2 = jnp.concatenate([top2, r2[8:]], axis=0)
        w = w_ref[0]
        return u * w[2:3, :] + p1 * w[1:2, :] + p2 * w[0:1, :] + b_ref[0]

    ug = conv(_dot(xn, wg_ref[...]), cwg_ref, cbg_ref, cg_ref)
    uv = conv(_dot(xn, wv_ref[...]), cwv_ref, cbv_ref, cv_ref)
    act = (ug * jax.nn.sigmoid(ug) * uv).astype(BF16)
    out_ref[...] += _dot(act, wd_ref[...])


def _ffn(h, g, w_up, conv_w, conv_b, w_down, L, tm=512, nf=2):
    T, D = h.shape
    F = w_down.shape[0]
    fc = F // nf
    assert fc % 128 == 0 and L % tm == 0
    cw = jnp.pad(conv_w, ((0, 5), (0, 0)))
    cwg = cw[:, :F].reshape(8, nf, fc).transpose(1, 0, 2)
    cwv = cw[:, F:].reshape(8, nf, fc).transpose(1, 0, 2)
    cbg = conv_b[:F].reshape(nf, 1, fc)
    cbv = conv_b[F:].reshape(nf, 1, fc)
    tok = pl.BlockSpec((tm, D), lambda i, j: (i, 0))
    return pl.pallas_call(
        functools.partial(_ffn_kernel, L // tm),
        grid=(T // tm, nf),
        in_specs=[tok, _full((1, D)),
                  pl.BlockSpec((D, fc), lambda i, j: (0, j)),
                  pl.BlockSpec((D, fc), lambda i, j: (0, nf + j)),
                  pl.BlockSpec((1, 8, fc), lambda i, j: (j, 0, 0)),
                  pl.BlockSpec((1, 8, fc), lambda i, j: (j, 0, 0)),
                  pl.BlockSpec((1, 1, fc), lambda i, j: (j, 0, 0)),
                  pl.BlockSpec((1, 1, fc), lambda i, j: (j, 0, 0)),
                  pl.BlockSpec((fc, D), lambda i, j: (j, 0))],
        out_specs=tok,
        out_shape=jax.ShapeDtypeStruct((T, D), F32),
        scratch_shapes=[pltpu.VMEM((tm, D), BF16),
                        pltpu.VMEM((nf, 8, fc), F32),
                        pltpu.VMEM((nf, 8, fc), F32)],
        compiler_params=_cparams(("arbitrary", "arbitrary")),
        name="conv_ffn",
    )(h, g, w_up, w_up, cwg, cwv, cbg, cbv, w_down)


def _ple_kernel(final, h_ref, p_ref, g_ref, wg_ref, wp_ref, fn_ref, out_ref):
    h = h_ref[...]
    gate = jax.nn.sigmoid(_dot(_rms(h, g_ref[...]).astype(BF16), wg_ref[...]))
    h = h + gate * _dot(p_ref[...].astype(BF16), wp_ref[...])
    if final:
        h = _rms(h, fn_ref[...])
    out_ref[...] = h


def _ple(h, p, g, wg, wp, fnorm, final, tm=512):
    T, D = h.shape
    tok = pl.BlockSpec((tm, D), lambda i: (i, 0))
    return pl.pallas_call(
        functools.partial(_ple_kernel, final),
        grid=(T // tm,),
        in_specs=[tok, pl.BlockSpec((tm, PLE_DIM), lambda i: (i, 0)), _full((1, D)),
                  _full((D, D)), _full((PLE_DIM, D)), _full((1, D))],
        out_specs=tok,
        out_shape=jax.ShapeDtypeStruct((T, D), F32),
        compiler_params=_cparams(("parallel",)),
        name="ple",
    )(h, p, g, wg, wp, fnorm)


def _at_in_kernel(h_ref, g_ref, wcq_ref, wckv_ref, wki_ref, wwi_ref, qn_ref, kvn_ref,
                  wuq_ref, wqi_ref, wuk_ref,
                  qlat_ref, qit_ref, wit_ref, ki_ref, ckv_ref, ckvt_ref):
    tm = h_ref.shape[0]
    xn = _rms(h_ref[...], g_ref[...]).astype(BF16)
    cq = _rms(_dot(xn, wcq_ref[...]), qn_ref[...]).astype(BF16)
    ckv = _rms(_dot(xn, wckv_ref[...]), kvn_ref[...])
    ckv_ref[...] = ckv.astype(BF16)
    for c in range(tm // KEY_TILE):
        ckvt_ref[c] = ckv[c * KEY_TILE:(c + 1) * KEY_TILE, :].T.astype(BF16)
    ki_ref[...] = _dot(xn, wki_ref[...]).astype(BF16)
    wit_ref[...] = _dot_nt(wwi_ref[...], xn) * (IDX_HEADS ** -0.5 * IDX_DIM ** -0.5)
    qit_ref[...] = _dot_nt(wqi_ref[...], cq).astype(BF16)
    qn = _dot(cq, wuq_ref[...]).astype(BF16)
    for pr in range(AT_HEADS // 2):
        ql = _dot_nt(wuk_ref[pr], qn[:, pr * 128:(pr + 1) * 128]) * (AT_DH ** -0.5)
        qlat_ref[2 * pr] = ql[:KV_LORA, :].astype(BF16)
        qlat_ref[2 * pr + 1] = ql[KV_LORA:, :].astype(BF16)


def _at_in(h, g, wcq, wckv, wki, wwi_t, qn, kvn, wuq, wqi_t, wuk_bd_t, tm=512):
    T, D = h.shape
    nk = tm // KEY_TILE
    ins = [h, g, wcq, wckv, wki, wwi_t, qn, kvn, wuq, wqi_t, wuk_bd_t]
    in_specs = [pl.BlockSpec((tm, D), lambda i: (i, 0))] + [_full(a.shape) for a in ins[1:]]
    return pl.pallas_call(
        _at_in_kernel,
        grid=(T // tm,),
        in_specs=in_specs,
        out_specs=[pl.BlockSpec((AT_HEADS, KV_LORA, tm), lambda i: (0, 0, i)),
                   pl.BlockSpec((IDX_HEADS * IDX_DIM, tm), lambda i: (0, i)),
                   pl.BlockSpec((IDX_HEADS, tm), lambda i: (0, i)),
                   pl.BlockSpec((tm, IDX_DIM), lambda i: (i, 0)),
                   pl.BlockSpec((tm, KV_LORA), lambda i: (i, 0)),
                   pl.BlockSpec((nk, KV_LORA, KEY_TILE), lambda i: (i, 0, 0))],
        out_shape=[jax.ShapeDtypeStruct((AT_HEADS, KV_LORA, T), BF16),
                   jax.ShapeDtypeStruct((IDX_HEADS * IDX_DIM, T), BF16),
                   jax.ShapeDtypeStruct((IDX_HEADS, T), F32),
                   jax.ShapeDtypeStruct((T, IDX_DIM), BF16),
                   jax.ShapeDtypeStruct((T, KV_LORA), BF16),
                   jax.ShapeDtypeStruct((T // KEY_TILE, KV_LORA, KEY_TILE), BF16)],
        compiler_params=_cparams(("parallel",)),
        name="at_in",
    )(*ins)


def _at_core_kernel(topk, qlat_ref, qit_ref, wit_ref, ki_ref, ckv_ref, ckvt_ref, bias_ref,
                    wuvt_ref, wo_ref, h_ref, out_ref, ik_ref, mask_ref, acc_ref, ot_ref):
    H, TK = AT_HEADS, KEY_TILE
    qb = pl.program_id(1)
    t0 = qb * QBLK
    nkt = qb // (TK // QBLK) + 1
    lane_q = lax.broadcasted_iota(I32, (TK, QBLK), 1)
    row_k = lax.broadcasted_iota(I32, (TK, QBLK), 0)

    wit = wit_ref[...]

    def score_tile(kt, carry):
        k0 = pl.multiple_of(kt * TK, TK)
        kk = ki_ref[pl.ds(k0, TK), :]
        acc = jnp.zeros((TK, 2 * QBLK), F32)
        for pr in range(IDX_HEADS // 2):
            qa = qit_ref[(2 * pr) * IDX_DIM:(2 * pr + 1) * IDX_DIM, :]
            qb_ = qit_ref[(2 * pr + 1) * IDX_DIM:(2 * pr + 2) * IDX_DIM, :]
            rel = jnp.maximum(_dot(kk, jnp.concatenate([qa, qb_], axis=1)), 0.0)
            w2 = jnp.concatenate([wit[2 * pr:2 * pr + 1, :], wit[2 * pr + 1:2 * pr + 2, :]], axis=1)
            acc = acc + rel * w2
        score = acc[:, :QBLK] + acc[:, QBLK:] + 0.0
        bits = pltpu.bitcast(score, I32)
        ikey = bits ^ ((bits >> 31) & 0x7FFFFFFF)
        causal = (k0 + row_k) <= (t0 + lane_q)
        ik_ref[pl.ds(k0, TK), :] = jnp.where(causal, ikey, INT_MIN)
        return carry

    lax.fori_loop(0, nkt, score_tile, 0)

    def count(pred):
        def body(kt, c):
            k0 = pl.multiple_of(kt * TK, TK)
            hit = jnp.where(pred(ik_ref[pl.ds(k0, TK), :], k0), 1, 0)
            return c + jnp.sum(hit.reshape(TK // 8, 8, QBLK), axis=0)
        c = lax.fori_loop(0, nkt, body, jnp.zeros((8, QBLK), I32))
        return jnp.sum(c, axis=0, keepdims=True)

    def bit_step(i, tau):
        cand = jnp.where(i == 0, jnp.zeros_like(tau), tau | (1 << (31 - i)))
        return jnp.where(count(lambda x, k0: x >= cand) >= topk, cand, tau)

    tau = lax.fori_loop(0, 32, bit_step, jnp.full((1, QBLK), INT_MIN, I32))
    n_gt = count(lambda x, k0: x > tau)
    n_ge = count(lambda x, k0: x >= tau)
    need = topk - n_gt

    def tie_search(_):
        def step(i, jc):
            cand = jc + (1 << (12 - i))
            n = count(lambda x, k0: (x == tau) & ((k0 + row_k) < cand))
            return jnp.where(n <= need, cand, jc)
        return lax.fori_loop(0, 13, step, jnp.zeros((1, QBLK), I32))

    jcut = lax.cond(jnp.max(n_ge) > topk, tie_search,
                    lambda _: jnp.full((1, QBLK), 2 ** 13, I32), 0)

    def mask_tile(kt, carry):
        k0 = pl.multiple_of(kt * TK, TK)
        x = ik_ref[pl.ds(k0, TK), :]
        kidx = k0 + row_k
        sel = ((x > tau) | ((x == tau) & (kidx < jcut))) & (kidx <= (t0 + lane_q))
        mask_ref[kt] = jnp.where(sel, 0.0, NEG_BIG)
        return carry

    lax.fori_loop(0, nkt, mask_tile, 0)

    nfar = jnp.maximum(nkt - 2, 0)

    NP = H // 2
    acc_ref[...] = jnp.zeros_like(acc_ref)

    def tile(near, kt, ml):
        m_all, l_all = ml
        k0 = pl.multiple_of(kt * TK, TK)
        am = mask_ref[kt]
        am2 = jnp.concatenate([am, am], axis=1)
        ck = ckv_ref[pl.ds(k0, TK), :]
        ckt = ckvt_ref[kt]
        ms, ls = [], []
        for pr in range(NP):
            qt = jnp.concatenate([qlat_ref[2 * pr], qlat_ref[2 * pr + 1]], axis=1)
            lg = _dot(ck, qt) + am2
            if near:
                lg = lg + bias_ref[jnp.minimum((t0 - k0) // QBLK, NEAR_TILES), pr]
            m_old = m_all[pr:pr + 1, :]
            m_new = jnp.maximum(m_old, jnp.max(lg, axis=0, keepdims=True))
            alpha = jnp.exp(m_old - m_new)
            p = jnp.exp(lg - m_new)
            acc_ref[pr] = acc_ref[pr] * alpha + _dot(ckt, p.astype(BF16))
            ms.append(m_new)
            ls.append(alpha * l_all[pr:pr + 1, :] + jnp.sum(p, axis=0, keepdims=True))
        return jnp.concatenate(ms, axis=0), jnp.concatenate(ls, axis=0)

    ml = (jnp.full((NP, 2 * QBLK), NEG_BIG, F32), jnp.zeros((NP, 2 * QBLK), F32))
    ml = lax.fori_loop(0, nfar, functools.partial(tile, False), ml)
    ml = lax.fori_loop(nfar, nkt, functools.partial(tile, True), ml)
    for pr in range(NP):
        olat = (acc_ref[pr] / ml[1][pr:pr + 1, :]).astype(BF16)
        ot_ref[pr * 2 * AT_DH:(pr + 1) * 2 * AT_DH, :] = jnp.concatenate(
            [_dot(wuvt_ref[2 * pr], olat[:, :QBLK]), _dot(wuvt_ref[2 * pr + 1], olat[:, QBLK:])], axis=0)
    out_ref[...] = h_ref[...] + _dot(ot_ref[...].T.astype(BF16), wo_ref[...])


def _at_core(qlat, qit, wit, ki, ckv, ckvt, bias, wuvt, wo, h, B, L, topk):
    T, D = h.shape
    nblk = L // QBLK
    nkt = L // KEY_TILE
    H = AT_HEADS
    return pl.pallas_call(
        functools.partial(_at_core_kernel, topk),
        grid=(B, nblk),
        in_specs=[pl.BlockSpec((H, KV_LORA, QBLK), lambda b, q: (0, 0, b * nblk + q)),
                  pl.BlockSpec((IDX_HEADS * IDX_DIM, QBLK), lambda b, q: (0, b * nblk + q)),
                  pl.BlockSpec((IDX_HEADS, QBLK), lambda b, q: (0, b * nblk + q)),
                  pl.BlockSpec((L, IDX_DIM), lambda b, q: (b, 0)),
                  pl.BlockSpec((L, KV_LORA), lambda b, q: (b, 0)),
                  pl.BlockSpec((nkt, KV_LORA, KEY_TILE), lambda b, q: (b, 0, 0)),
                  _full(bias.shape, single=True), _full(wuvt.shape, single=True),
                  _full(wo.shape, single=True),
                  pl.BlockSpec((QBLK, D), lambda b, q: (b * nblk + q, 0))],
        out_specs=pl.BlockSpec((QBLK, D), lambda b, q: (b * nblk + q, 0)),
        out_shape=jax.ShapeDtypeStruct((T, D), F32),
        scratch_shapes=[pltpu.VMEM((L, QBLK), I32),
                        pltpu.VMEM((nkt, KEY_TILE, QBLK), F32),
                        pltpu.VMEM((H // 2, KV_LORA, 2 * QBLK), F32),
                        pltpu.VMEM((H * AT_DH, QBLK), F32)],
        compiler_params=_cparams(("arbitrary", "arbitrary")),
        name="at_core",
    )(qlat, qit, wit, ki, ckv, ckvt, bias, wuvt, wo, h)


def _rel_bucket(n):
    max_exact = REL_BUCKETS // 2
    nf = jnp.maximum(n, 1).astype(F32)
    large = max_exact + (jnp.log(nf / max_exact) / math.log(REL_MAX_DIST / max_exact)
                         * (REL_BUCKETS - max_exact)).astype(I32)
    large = jnp.minimum(large, REL_BUCKETS - 1)
    return jnp.where(n < max_exact, n, large)


def _bias_tiles(rel_bias):
    H = rel_bias.shape[1]
    span = KEY_TILE + QBLK - 1
    m = jnp.arange(NEAR_TILES * QBLK + span, dtype=I32)
    tab = rel_bias[_rel_bucket(jnp.maximum(m - (KEY_TILE - 1), 0))] - rel_bias[REL_BUCKETS - 1]
    tiles = []
    for d in range(NEAR_TILES):
        u = tab[d * QBLK:d * QBLK + span][::-1]
        r = jnp.tile(u, (KEY_TILE + 1, 1))[:KEY_TILE * (span + 1)].reshape(KEY_TILE, span + 1, H)
        tiles.append(r[:, :QBLK][:, ::-1])
    tiles.append(jnp.zeros_like(tiles[0]))
    t = jnp.stack(tiles).astype(F32)
    t = jnp.transpose(t, (0, 3, 1, 2))
    return jnp.concatenate([t[:, 0::2], t[:, 1::2]], axis=-1)


def kernel(x, p, hg_norm, hg_w_in, hg_lb, hg_onorm, hg_w_out, at_norm, at_w_in, at_q_norm,
           at_kv_norm, at_w_uq, at_w_uk, at_w_uv, at_w_qidx, at_w_out, rel_bias, ff_norm, ff_w_up,
           ff_conv_w, ff_conv_b, ff_w_down, ple_norm, ple_w_gate, ple_w_proj, final_norm):
    B, L, D = x.shape
    T = B * L
    depth = p.shape[0]
    row = lambda a: a.reshape(1, -1)
    h = x.reshape(T, D)
    for i in range(depth):
        j = i // 2
        if i % 2 == 0:
            q, k, v, gs, lf = _hg_in(h, row(hg_norm[j]), hg_lb, hg_w_in[j].astype(BF16), i)
            h = _hg_rec(q, k, v, gs, lf, h, row(hg_onorm[j]), hg_w_out[j].astype(BF16), B, L)
        else:
            w_in = at_w_in[j].astype(BF16)
            o1, o2, o3 = Q_LORA, Q_LORA + KV_LORA, Q_LORA + KV_LORA + IDX_DIM
            wuk = at_w_uk[j].astype(BF16)
            z = jnp.zeros_like(wuk[0::2])
            wuk_bd = jnp.concatenate([jnp.concatenate([wuk[0::2], z], axis=2),
                                      jnp.concatenate([z, wuk[1::2]], axis=2)], axis=1)
            wuvt = jnp.transpose(at_w_uv[j].astype(BF16), (0, 2, 1))
            qlat, qit, wit, ki, ckv, ckvt = _at_in(
                h, row(at_norm[j]), w_in[:, :o1], w_in[:, o1:o2], w_in[:, o2:o3], w_in[:, o3:].T,
                row(at_q_norm[j]), row(at_kv_norm[j]), at_w_uq[j].astype(BF16),
                at_w_qidx[j].astype(BF16).T, jnp.transpose(wuk_bd, (0, 2, 1)))
            topk = max(1, min(TOPK_MAX, L // 4))
            h = _at_core(qlat, qit, wit, ki, ckv, ckvt, _bias_tiles(rel_bias), wuvt,
                         at_w_out[j].astype(BF16), h, B, L, topk)
        h = _ffn(h, row(ff_norm[i]), ff_w_up[i].astype(BF16), ff_conv_w[i], ff_conv_b[i],
                 ff_w_down[i].astype(BF16), L)
        h = _ple(h, p[i].reshape(T, PLE_DIM), row(ple_norm[i]), ple_w_gate[i].astype(BF16),
                 ple_w_proj[i].astype(BF16), row(final_norm), final=(i == depth - 1))
    return h.reshape(B, L, D)
```

```python
import functools
import math

import jax
import jax.numpy as jnp
import numpy as np
from jax import lax
from jax.experimental import pallas as pl
from jax.experimental.pallas import tpu as pltpu

F32 = jnp.float32
BF16 = jnp.bfloat16
I32 = jnp.int32

D_MODEL = 1024
HG_HEADS = 8
HG_DK = D_MODEL // HG_HEADS
AT_HEADS = 16
AT_DH = 64
Q_LORA = 384
KV_LORA = 256
IDX_HEADS = 8
IDX_DIM = 64
TOPK_MAX = 256
QBLK = 128
REL_BUCKETS = 32
REL_MAX_DIST = 128
D_FF = 2816
PLE_DIM = 256
EPS = 1e-6

VMEM_LIMIT_V7X = 56 * 1024 * 1024
SUB = 16
KEY_TILE = 256
NEAR_TILES = (KEY_TILE + REL_MAX_DIST) // QBLK
NEG_BIG = -1e30
LOG2E = 1.4426950408889634
ONES_ROWS = 16
INT_MIN = -(2 ** 31)


def _cparams(sem):
    return pltpu.CompilerParams(dimension_semantics=sem, vmem_limit_bytes=VMEM_LIMIT_V7X)


def _rms(x, g):
    return x * lax.rsqrt(jnp.mean(x * x, axis=-1, keepdims=True) + EPS) * g


def _dot(a, b):
    return jnp.dot(a, b, preferred_element_type=F32)


def _dot_nt(a, b):
    return lax.dot_general(a, b, (((1,), (1,)), ((), ())), preferred_element_type=F32)


def _dot_tn(a, b):
    return lax.dot_general(a, b, (((0,), (0,)), ((), ())), preferred_element_type=F32)


def _full(shape, single=False):
    n = len(shape)
    if single:
        return pl.BlockSpec(shape, lambda *_: (0,) * n, pipeline_mode=pl.Buffered(1))
    return pl.BlockSpec(shape, lambda *_: (0,) * n)


def _hg_in_kernel(layer, x_ref, g_ref, lbp_ref, w_ref, q_ref, k_ref, v_ref, gs_ref, lf_ref):
    D = D_MODEL
    xn = _rms(x_ref[...], g_ref[...]).astype(BF16)
    lbp = lbp_ref[...]
    e = jnp.exp(lbp - jnp.max(lbp, axis=0, keepdims=True))
    lb = jnp.sum(e[: layer + 1], axis=0, keepdims=True) / jnp.sum(e, axis=0, keepdims=True)
    q = _dot(xn, w_ref[:, 0:D])
    q_ref[...] = (q * jax.nn.sigmoid(q)).astype(BF16)
    f = lb + (1.0 - lb) * jax.nn.sigmoid(_dot(xn, w_ref[:, D:2 * D]))
    lf_ref[...] = jnp.log(f)
    k_ref[...] = (1.0 - f).astype(BF16)
    v_ref[...] = _dot(xn, w_ref[:, 2 * D:3 * D]).astype(BF16)
    g = _dot(xn, w_ref[:, 3 * D:4 * D])
    gs_ref[...] = (g * jax.nn.sigmoid(g)).astype(BF16)


def _hg_in(x, g, lbp, w, layer, tm=512):
    T, D = x.shape
    tok = pl.BlockSpec((tm, D), lambda i: (i, 0))
    bf = jax.ShapeDtypeStruct((T, D), BF16)
    return pl.pallas_call(
        functools.partial(_hg_in_kernel, layer),
        grid=(T // tm,),
        in_specs=[tok, _full((1, D)), _full(lbp.shape), _full(w.shape)],
        out_specs=[tok] * 5,
        out_shape=[bf, bf, bf, bf, jax.ShapeDtypeStruct((T, D), F32)],
        compiler_params=_cparams(("parallel",)),
        name="hg_in",
    )(x, g, lbp, w)


def _hg_rec_kernel(lt, q_ref, k_ref, v_ref, gs_ref, lf_ref, x_ref, on_ref, wo_ref, tri_ref,
                   out_ref, st_ref, b_ref, o_ref):
    H, DK = HG_HEADS, HG_DK

    @pl.when(pl.program_id(1) == 0)
    def _():
        st_ref[...] = jnp.zeros_like(st_ref)

    lf = lf_ref[...]
    hi = lf.astype(BF16)
    r1 = lf - hi.astype(F32)
    mid = r1.astype(BF16)
    lo = (r1 - mid.astype(F32)).astype(BF16)
    tri = tri_ref[...]
    b_ref[...] = _dot(tri, hi) + _dot(tri, mid) + _dot(tri, lo)

    HALF = SUB // 2
    row_h = lax.broadcasted_iota(I32, (HALF, DK), 0)

    def sub_chunk(c, carry):
        r0 = pl.multiple_of(c * SUB, SUB)
        for h in range(H):
            ls = slice(h * DK, (h + 1) * DK)
            b = b_ref[pl.ds(r0, SUB), ls]
            q = q_ref[pl.ds(r0, SUB), ls].astype(F32)
            k = k_ref[pl.ds(r0, SUB), ls].astype(F32)
            v = v_ref[pl.ds(r0, SUB), ls].astype(F32)
            be = b[SUB - 1:SUB, :]
            st = st_ref[h]
            o = _dot_nt((q * jnp.exp(b)).astype(BF16), st.astype(BF16))
            o_top = jnp.zeros((HALF, DK), F32)
            o_bot = jnp.zeros((HALF, DK), F32)
            for s in range(SUB):
                bs, ks, vs = b[s:s + 1, :], k[s:s + 1, :], v[s:s + 1, :]
                if s < HALF:
                    e = jnp.exp(jnp.where(row_h >= s, b[:HALF] - bs, NEG_BIG))
                    o_top = o_top + jnp.sum(q[:HALF] * e * ks, axis=-1, keepdims=True) * vs
                    e = jnp.exp(b[HALF:] - bs)
                else:
                    e = jnp.exp(jnp.where(row_h >= s - HALF, b[HALF:] - bs, NEG_BIG))
                o_bot = o_bot + jnp.sum(q[HALF:] * e * ks, axis=-1, keepdims=True) * vs
            o_ref[pl.ds(r0, SUB), ls] = o + jnp.concatenate([o_top, o_bot], axis=0)
            kd = (k * jnp.exp(be - b)).astype(BF16)
            st_ref[h] = st * jnp.exp(be) + _dot_tn(v.astype(BF16), kd)
        return carry

    lax.fori_loop(0, lt // SUB, sub_chunk, 0)

    on = on_ref[...]
    for h in range(H):
        ls = slice(h * DK, (h + 1) * DK)
        oh = _rms(o_ref[:, ls], on[:, ls]) * gs_ref[:, ls].astype(F32)
        o_ref[:, ls] = oh
    out_ref[...] = x_ref[...] + _dot(o_ref[...].astype(BF16), wo_ref[...])


def _hg_rec(q, k, v, gs, lf, x, onorm, wo, B, L, lt=256):
    T, D = x.shape
    nl = L // lt
    tok = pl.BlockSpec((lt, D), lambda b, j: (b * nl + j, 0))
    tri = (np.arange(lt)[:, None] >= np.arange(lt)[None, :]) & (
        np.arange(lt)[:, None] // SUB == np.arange(lt)[None, :] // SUB)
    tri = jnp.asarray(tri, BF16)
    return pl.pallas_call(
        functools.partial(_hg_rec_kernel, lt),
        grid=(B, nl),
        in_specs=[tok] * 6 + [_full((1, D)), _full((D, D)), _full((lt, lt))],
        out_specs=tok,
        out_shape=jax.ShapeDtypeStruct((T, D), F32),
        scratch_shapes=[pltpu.VMEM((HG_HEADS, HG_DK, HG_DK), F32),
                        pltpu.VMEM((lt, D), F32),
                        pltpu.VMEM((lt, D), F32)],
        compiler_params=_cparams(("arbitrary", "arbitrary")),
        name="hg_rec",
    )(q, k, v, gs, lf, x, onorm, wo, tri)


def _ffn_kernel(nseq, h_ref, g_ref, wg_ref, wv_ref, cwg_ref, cwv_ref, cbg_ref, cbv_ref, wd_ref,
                out_ref, xn_ref, cg_ref, cv_ref):
    i, j = pl.program_id(0), pl.program_id(1)

    @pl.when(j == 0)
    def _():
        xn_ref[...] = _rms(h_ref[...], g_ref[...]).astype(BF16)
        out_ref[...] = h_ref[...]

    seq_start = (i % nseq) == 0
    xn = xn_ref[...]

    def conv(u, w_ref, b_ref, carry_ref):
        fc = u.shape[1]
        carry = jnp.where(seq_start, 0.0, carry_ref[j])
        carry_ref[j] = u[u.shape[0] - 8:, :]
        r1 = pltpu.roll(u, 1, 0)
        r2 = pltpu.roll(u, 2, 0)
        rows = lax.broadcasted_iota(I32, (8, fc), 0)
        top1 = jnp.where(rows < 1, pltpu.roll(carry, 1, 0), r1[0:8])
        top2 = jnp.where(rows < 2, pltpu.roll(carry, 2, 0), r2[0:8])
        p1 = jnp.concatenate([top1, r1[8:]], axis=0)
        p2 = jnp.concatenate([top2, r2[8:]], axis=0)
        w = w_ref[0]
        return u * w[2:3, :] + p1 * w[1:2, :] + p2 * w[0:1, :] + b_ref[0]

    ug = conv(_dot(xn, wg_ref[...]), cwg_ref, cbg_ref, cg_ref)
    uv = conv(_dot(xn, wv_ref[...]), cwv_ref, cbv_ref, cv_ref)
    act = (ug * jax.nn.sigmoid(ug) * uv).astype(BF16)
    out_ref[...] += _dot(act, wd_ref[...])


def _ffn(h, g, w_up, conv_w, conv_b, w_down, L, tm=512, nf=2):
    T, D = h.shape
    F = w_down.shape[0]
    fc = F // nf
    assert fc % 128 == 0 and L % tm == 0
    cw = jnp.pad(conv_w, ((0, 5), (0, 0)))
    cwg = cw[:, :F].reshape(8, nf, fc).transpose(1, 0, 2)
    cwv = cw[:, F:].reshape(8, nf, fc).transpose(1, 0, 2)
    cbg = conv_b[:F].reshape(nf, 1, fc)
    cbv = conv_b[F:].reshape(nf, 1, fc)
    tok = pl.BlockSpec((tm, D), lambda i, j: (i, 0))
    return pl.pallas_call(
        functools.partial(_ffn_kernel, L // tm),
        grid=(T // tm, nf),
        in_specs=[tok, _full((1, D)),
                  pl.BlockSpec((D, fc), lambda i, j: (0, j)),
                  pl.BlockSpec((D, fc), lambda i, j: (0, nf + j)),
                  pl.BlockSpec((1, 8, fc), lambda i, j: (j, 0, 0)),
                  pl.BlockSpec((1, 8, fc), lambda i, j: (j, 0, 0)),
                  pl.BlockSpec((1, 1, fc), lambda i, j: (j, 0, 0)),
                  pl.BlockSpec((1, 1, fc), lambda i, j: (j, 0, 0)),
                  pl.BlockSpec((fc, D), lambda i, j: (j, 0))],
        out_specs=tok,
        out_shape=jax.ShapeDtypeStruct((T, D), F32),
        scratch_shapes=[pltpu.VMEM((tm, D), BF16),
                        pltpu.VMEM((nf, 8, fc), F32),
                        pltpu.VMEM((nf, 8, fc), F32)],
        compiler_params=_cparams(("arbitrary", "arbitrary")),
        name="conv_ffn",
    )(h, g, w_up, w_up, cwg, cwv, cbg, cbv, w_down)


def _ple_kernel(final, h_ref, p_ref, g_ref, wg_ref, wp_ref, fn_ref, out_ref):
    h = h_ref[...]
    gate = jax.nn.sigmoid(_dot(_rms(h, g_ref[...]).astype(BF16), wg_ref[...]))
    h = h + gate * _dot(p_ref[...].astype(BF16), wp_ref[...])
    if final:
        h = _rms(h, fn_ref[...])
    out_ref[...] = h


def _ple(h, p, g, wg, wp, fnorm, final, tm=512):
    T, D = h.shape
    tok = pl.BlockSpec((tm, D), lambda i: (i, 0))
    return pl.pallas_call(
        functools.partial(_ple_kernel, final),
        grid=(T // tm,),
        in_specs=[tok, pl.BlockSpec((tm, PLE_DIM), lambda i: (i, 0)), _full((1, D)),
                  _full((D, D)), _full((PLE_DIM, D)), _full((1, D))],
        out_specs=tok,
        out_shape=jax.ShapeDtypeStruct((T, D), F32),
        compiler_params=_cparams(("parallel",)),
        name="ple",
    )(h, p, g, wg, wp, fnorm)


def _at_in_kernel(h_ref, g_ref, wcq_ref, wckv_ref, wki_ref, wwi_ref, qn_ref, kvn_ref,
                  wuq_ref, wqi_ref, wuk_ref,
                  qlat_ref, qit_ref, wit_ref, ki_ref, ckv_ref, ckvt_ref):
    tm = h_ref.shape[0]
    xn = _rms(h_ref[...], g_ref[...]).astype(BF16)
    cq = _rms(_dot(xn, wcq_ref[...]), qn_ref[...]).astype(BF16)
    ckv = _rms(_dot(xn, wckv_ref[...]), kvn_ref[...])
    ckv_ref[...] = ckv.astype(BF16)
    for c in range(tm // KEY_TILE):
        ckvt_ref[c] = jnp.concatenate([ckv[c * KEY_TILE:(c + 1) * KEY_TILE, :].T.astype(BF16),
                                       jnp.ones((ONES_ROWS, KEY_TILE), BF16)], axis=0)
    ki_ref[...] = _dot(xn, wki_ref[...]).astype(BF16)
    wit_ref[...] = _dot_nt(wwi_ref[...], xn) * (IDX_HEADS ** -0.5 * IDX_DIM ** -0.5)
    qit_ref[...] = _dot_nt(wqi_ref[...], cq).astype(BF16)
    qn = _dot(cq, wuq_ref[...]).astype(BF16)
    for pr in range(AT_HEADS // 2):
        ql = _dot_nt(wuk_ref[pr], qn[:, pr * 128:(pr + 1) * 128]) * (AT_DH ** -0.5 * LOG2E)
        qlat_ref[2 * pr] = ql[:KV_LORA, :].astype(BF16)
        qlat_ref[2 * pr + 1] = ql[KV_LORA:, :].astype(BF16)


def _at_in(h, g, wcq, wckv, wki, wwi_t, qn, kvn, wuq, wqi_t, wuk_bd_t, tm=512):
    T, D = h.shape
    nk = tm // KEY_TILE
    ins = [h, g, wcq, wckv, wki, wwi_t, qn, kvn, wuq, wqi_t, wuk_bd_t]
    in_specs = [pl.BlockSpec((tm, D), lambda i: (i, 0))] + [_full(a.shape) for a in ins[1:]]
    return pl.pallas_call(
        _at_in_kernel,
        grid=(T // tm,),
        in_specs=in_specs,
        out_specs=[pl.BlockSpec((AT_HEADS, KV_LORA, tm), lambda i: (0, 0, i)),
                   pl.BlockSpec((IDX_HEADS * IDX_DIM, tm), lambda i: (0, i)),
                   pl.BlockSpec((IDX_HEADS, tm), lambda i: (0, i)),
                   pl.BlockSpec((tm, IDX_DIM), lambda i: (i, 0)),
                   pl.BlockSpec((tm, KV_LORA), lambda i: (i, 0)),
                   pl.BlockSpec((nk, KV_LORA + ONES_ROWS, KEY_TILE), lambda i: (i, 0, 0))],
        out_shape=[jax.ShapeDtypeStruct((AT_HEADS, KV_LORA, T), BF16),
                   jax.ShapeDtypeStruct((IDX_HEADS * IDX_DIM, T), BF16),
                   jax.ShapeDtypeStruct((IDX_HEADS, T), F32),
                   jax.ShapeDtypeStruct((T, IDX_DIM), BF16),
                   jax.ShapeDtypeStruct((T, KV_LORA), BF16),
                   jax.ShapeDtypeStruct((T // KEY_TILE, KV_LORA + ONES_ROWS, KEY_TILE), BF16)],
        compiler_params=_cparams(("parallel",)),
        name="at_in",
    )(*ins)


def _at_core_kernel(topk, qlat_ref, qit_ref, wit_ref, ki_ref, ckv_ref, ckvt_ref, bias_ref,
                    wuvt_ref, wo_ref, h_ref, out_ref, ik_ref, mask_ref, acc_ref, ot_ref):
    H, TK = AT_HEADS, KEY_TILE
    qb = pl.program_id(1)
    t0 = qb * QBLK
    nkt = qb // (TK // QBLK) + 1
    lane_q = lax.broadcasted_iota(I32, (TK, QBLK), 1)
    row_k = lax.broadcasted_iota(I32, (TK, QBLK), 0)

    wit = wit_ref[...]

    def score_tile(kt, carry):
        k0 = pl.multiple_of(kt * TK, TK)
        kk = ki_ref[pl.ds(k0, TK), :]
        acc = jnp.zeros((TK, 2 * QBLK), F32)
        for pr in range(IDX_HEADS // 2):
            qa = qit_ref[(2 * pr) * IDX_DIM:(2 * pr + 1) * IDX_DIM, :]
            qb_ = qit_ref[(2 * pr + 1) * IDX_DIM:(2 * pr + 2) * IDX_DIM, :]
            rel = jnp.maximum(_dot(kk, jnp.concatenate([qa, qb_], axis=1)), 0.0)
            w2 = jnp.concatenate([wit[2 * pr:2 * pr + 1, :], wit[2 * pr + 1:2 * pr + 2, :]], axis=1)
            acc = acc + rel * w2
        score = acc[:, :QBLK] + acc[:, QBLK:] + 0.0
        bits = pltpu.bitcast(score, I32)
        ikey = bits ^ ((bits >> 31) & 0x7FFFFFFF)
        causal = (k0 + row_k) <= (t0 + lane_q)
        ik_ref[pl.ds(k0, TK), :] = jnp.where(causal, ikey, INT_MIN)
        return carry

    lax.fori_loop(0, nkt, score_tile, 0)

    def count(pred):
        def body(kt, c):
            k0 = pl.multiple_of(kt * TK, TK)
            hit = jnp.where(pred(ik_ref[pl.ds(k0, TK), :], k0), 1, 0)
            return c + jnp.sum(hit.reshape(TK // 8, 8, QBLK), axis=0)
        c = lax.fori_loop(0, nkt, body, jnp.zeros((8, QBLK), I32))
        return jnp.sum(c, axis=0, keepdims=True)

    def bit_step(i, tau):
        cand = jnp.where(i == 0, jnp.zeros_like(tau), tau | (1 << (31 - i)))
        return jnp.where(count(lambda x, k0: x >= cand) >= topk, cand, tau)

    tau = lax.fori_loop(0, 32, bit_step, jnp.full((1, QBLK), INT_MIN, I32))
    n_gt = count(lambda x, k0: x > tau)
    n_ge = count(lambda x, k0: x >= tau)
    need = topk - n_gt

    def tie_search(_):
        def step(i, jc):
            cand = jc + (1 << (12 - i))
            n = count(lambda x, k0: (x == tau) & ((k0 + row_k) < cand))
            return jnp.where(n <= need, cand, jc)
        return lax.fori_loop(0, 13, step, jnp.zeros((1, QBLK), I32))

    jcut = lax.cond(jnp.max(n_ge) > topk, tie_search,
                    lambda _: jnp.full((1, QBLK), 2 ** 13, I32), 0)

    def mask_tile(kt, carry):
        k0 = pl.multiple_of(kt * TK, TK)
        x = ik_ref[pl.ds(k0, TK), :]
        kidx = k0 + row_k
        sel = ((x > tau) | ((x == tau) & (kidx < jcut))) & (kidx <= (t0 + lane_q))
        mask_ref[kt] = jnp.where(sel, 0.0, NEG_BIG)
        return carry

    lax.fori_loop(0, nkt, mask_tile, 0)

    nfar = jnp.maximum(nkt - 2, 0)

    NP = H // 2
    acc_ref[...] = jnp.zeros_like(acc_ref)

    def tile(near, kt, m_all):
        k0 = pl.multiple_of(kt * TK, TK)
        am = mask_ref[kt]
        am2 = jnp.concatenate([am, am], axis=1)
        ck = ckv_ref[pl.ds(k0, TK), :]
        ckt = ckvt_ref[kt]
        ms = []
        for pr in range(NP):
            qt = jnp.concatenate([qlat_ref[2 * pr], qlat_ref[2 * pr + 1]], axis=1)
            lg = _dot(ck, qt) + am2
            if near:
                lg = lg + bias_ref[jnp.minimum((t0 - k0) // QBLK, NEAR_TILES), pr]
            m_old = m_all[pr:pr + 1, :]
            m_new = jnp.maximum(m_old, jnp.max(lg, axis=0, keepdims=True))
            p = jnp.exp2((lg - m_new).astype(BF16))
            acc_ref[pr] = acc_ref[pr] * jnp.exp2(m_old - m_new) + _dot(ckt, p)
            ms.append(m_new)
        return jnp.concatenate(ms, axis=0)

    m_all = jnp.full((NP, 2 * QBLK), NEG_BIG, F32)
    m_all = lax.fori_loop(0, nfar, functools.partial(tile, False), m_all)
    m_all = lax.fori_loop(nfar, nkt, functools.partial(tile, True), m_all)
    for pr in range(NP):
        a = acc_ref[pr]
        olat = (a[:KV_LORA] / a[KV_LORA:KV_LORA + 1]).astype(BF16)
        ot_ref[pr * 2 * AT_DH:(pr + 1) * 2 * AT_DH, :] = jnp.concatenate(
            [_dot(wuvt_ref[2 * pr], olat[:, :QBLK]), _dot(wuvt_ref[2 * pr + 1], olat[:, QBLK:])], axis=0)
    out_ref[...] = h_ref[...] + _dot(ot_ref[...].T.astype(BF16), wo_ref[...])


def _at_core(qlat, qit, wit, ki, ckv, ckvt, bias, wuvt, wo, h, B, L, topk):
    T, D = h.shape
    nblk = L // QBLK
    nkt = L // KEY_TILE
    H = AT_HEADS
    return pl.pallas_call(
        functools.partial(_at_core_kernel, topk),
        grid=(B, nblk),
        in_specs=[pl.BlockSpec((H, KV_LORA, QBLK), lambda b, q: (0, 0, b * nblk + q)),
                  pl.BlockSpec((IDX_HEADS * IDX_DIM, QBLK), lambda b, q: (0, b * nblk + q)),
                  pl.BlockSpec((IDX_HEADS, QBLK), lambda b, q: (0, b * nblk + q)),
                  pl.BlockSpec((L, IDX_DIM), lambda b, q: (b, 0)),
                  pl.BlockSpec((L, KV_LORA), lambda b, q: (b, 0)),
                  pl.BlockSpec((nkt, KV_LORA + ONES_ROWS, KEY_TILE), lambda b, q: (b, 0, 0)),
                  _full(bias.shape, single=True), _full(wuvt.shape, single=True),
                  _full(wo.shape, single=True),
                  pl.BlockSpec((QBLK, D), lambda b, q: (b * nblk + q, 0))],
        out_specs=pl.BlockSpec((QBLK, D), lambda b, q: (b * nblk + q, 0)),
        out_shape=jax.ShapeDtypeStruct((T, D), F32),
        scratch_shapes=[pltpu.VMEM((L, QBLK), I32),
                        pltpu.VMEM((nkt, KEY_TILE, QBLK), F32),
                        pltpu.VMEM((H // 2, KV_LORA + ONES_ROWS, 2 * QBLK), F32),
                        pltpu.VMEM((H * AT_DH, QBLK), F32)],
        compiler_params=_cparams(("arbitrary", "arbitrary")),
        name="at_core",
    )(qlat, qit, wit, ki, ckv, ckvt, bias, wuvt, wo, h)


def _rel_bucket(n):
    max_exact = REL_BUCKETS // 2
    nf = jnp.maximum(n, 1).astype(F32)
    large = max_exact + (jnp.log(nf / max_exact) / math.log(REL_MAX_DIST / max_exact)
                         * (REL_BUCKETS - max_exact)).astype(I32)
    large = jnp.minimum(large, REL_BUCKETS - 1)
    return jnp.where(n < max_exact, n, large)


def _bias_tiles(rel_bias):
    H = rel_bias.shape[1]
    span = KEY_TILE + QBLK - 1
    m = jnp.arange(NEAR_TILES * QBLK + span, dtype=I32)
    tab = rel_bias[_rel_bucket(jnp.maximum(m - (KEY_TILE - 1), 0))] - rel_bias[REL_BUCKETS - 1]
    tiles = []
    for d in range(NEAR_TILES):
        u = tab[d * QBLK:d * QBLK + span][::-1]
        r = jnp.tile(u, (KEY_TILE + 1, 1))[:KEY_TILE * (span + 1)].reshape(KEY_TILE, span + 1, H)
        tiles.append(r[:, :QBLK][:, ::-1])
    tiles.append(jnp.zeros_like(tiles[0]))
    t = jnp.stack(tiles).astype(F32) * LOG2E
    t = jnp.transpose(t, (0, 3, 1, 2))
    return jnp.concatenate([t[:, 0::2], t[:, 1::2]], axis=-1)


def kernel(x, p, hg_norm, hg_w_in, hg_lb, hg_onorm, hg_w_out, at_norm, at_w_in, at_q_norm,
           at_kv_norm, at_w_uq, at_w_uk, at_w_uv, at_w_qidx, at_w_out, rel_bias, ff_norm, ff_w_up,
           ff_conv_w, ff_conv_b, ff_w_down, ple_norm, ple_w_gate, ple_w_proj, final_norm):
    B, L, D = x.shape
    T = B * L
    depth = p.shape[0]
    row = lambda a: a.reshape(1, -1)
    h = x.reshape(T, D)
    for i in range(depth):
        j = i // 2
        if i % 2 == 0:
            q, k, v, gs, lf = _hg_in(h, row(hg_norm[j]), hg_lb, hg_w_in[j].astype(BF16), i)
            h = _hg_rec(q, k, v, gs, lf, h, row(hg_onorm[j]), hg_w_out[j].astype(BF16), B, L)
        else:
            w_in = at_w_in[j].astype(BF16)
            o1, o2, o3 = Q_LORA, Q_LORA + KV_LORA, Q_LORA + KV_LORA + IDX_DIM
            wuk = at_w_uk[j].astype(BF16)
            z = jnp.zeros_like(wuk[0::2])
            wuk_bd = jnp.concatenate([jnp.concatenate([wuk[0::2], z], axis=2),
                                      jnp.concatenate([z, wuk[1::2]], axis=2)], axis=1)
            wuvt = jnp.transpose(at_w_uv[j].astype(BF16), (0, 2, 1))
            qlat, qit, wit, ki, ckv, ckvt = _at_in(
                h, row(at_norm[j]), w_in[:, :o1], w_in[:, o1:o2], w_in[:, o2:o3], w_in[:, o3:].T,
                row(at_q_norm[j]), row(at_kv_norm[j]), at_w_uq[j].astype(BF16),
                at_w_qidx[j].astype(BF16).T, jnp.transpose(wuk_bd, (0, 2, 1)))
            topk = max(1, min(TOPK_MAX, L // 4))
            h = _at_core(qlat, qit, wit, ki, ckv, ckvt, _bias_tiles(rel_bias), wuvt,
                         at_w_out[j].astype(BF16), h, B, L, topk)
        h = _ffn(h, row(ff_norm[i]), ff_w_up[i].astype(BF16), ff_conv_w[i], ff_conv_b[i],
                 ff_w_down[i].astype(BF16), L)
        h = _ple(h, p[i].reshape(T, PLE_DIM), row(ple_norm[i]), ple_w_gate[i].astype(BF16),
                 ple_w_proj[i].astype(BF16), row(final_norm), final=(i == depth - 1))
    return h.reshape(B, L, D)
```

```python
import functools
import math

import jax
import jax.numpy as jnp
import numpy as np
from jax import lax
from jax.experimental import pallas as pl
from jax.experimental.pallas import tpu as pltpu

F32 = jnp.float32
BF16 = jnp.bfloat16
I32 = jnp.int32

D_MODEL = 1024
HG_HEADS = 8
HG_DK = D_MODEL // HG_HEADS
AT_HEADS = 16
AT_DH = 64
Q_LORA = 384
KV_LORA = 256
IDX_HEADS = 8
IDX_DIM = 64
TOPK_MAX = 256
QBLK = 128
REL_BUCKETS = 32
REL_MAX_DIST = 128
D_FF = 2816
PLE_DIM = 256
EPS = 1e-6

VMEM_LIMIT_V7X = 56 * 1024 * 1024
SUB = 16
KEY_TILE = 256
NEAR_TILES = (KEY_TILE + REL_MAX_DIST) // QBLK
NEG_BIG = -1e30
LOG2E = 1.4426950408889634
ONES_ROWS = 16
INT_MIN = -(2 ** 31)


def _cparams(sem):
    return pltpu.CompilerParams(dimension_semantics=sem, vmem_limit_bytes=VMEM_LIMIT_V7X)


def _rms(x, g):
    return x * lax.rsqrt(jnp.mean(x * x, axis=-1, keepdims=True) + EPS) * g


def _dot(a, b):
    return jnp.dot(a, b, preferred_element_type=F32)


def _dot_nt(a, b):
    return lax.dot_general(a, b, (((1,), (1,)), ((), ())), preferred_element_type=F32)


def _dot_tn(a, b):
    return lax.dot_general(a, b, (((0,), (0,)), ((), ())), preferred_element_type=F32)


def _full(shape, single=False):
    n = len(shape)
    if single:
        return pl.BlockSpec(shape, lambda *_: (0,) * n, pipeline_mode=pl.Buffered(1))
    return pl.BlockSpec(shape, lambda *_: (0,) * n)


def _hg_in_kernel(layer, x_ref, g_ref, lbp_ref, w_ref, q_ref, k_ref, v_ref, gs_ref, lf_ref):
    D = D_MODEL
    xn = _rms(x_ref[...], g_ref[...]).astype(BF16)
    lbp = lbp_ref[...]
    e = jnp.exp(lbp - jnp.max(lbp, axis=0, keepdims=True))
    lb = jnp.sum(e[: layer + 1], axis=0, keepdims=True) / jnp.sum(e, axis=0, keepdims=True)
    q = _dot(xn, w_ref[:, 0:D])
    q_ref[...] = (q * jax.nn.sigmoid(q)).astype(BF16)
    f = lb + (1.0 - lb) * jax.nn.sigmoid(_dot(xn, w_ref[:, D:2 * D]))
    lf_ref[...] = jnp.log(f) * LOG2E
    k_ref[...] = (1.0 - f).astype(BF16)
    v_ref[...] = _dot(xn, w_ref[:, 2 * D:3 * D]).astype(BF16)
    g = _dot(xn, w_ref[:, 3 * D:4 * D])
    gs_ref[...] = (g * jax.nn.sigmoid(g)).astype(BF16)


def _hg_in(x, g, lbp, w, layer, tm=512):
    T, D = x.shape
    tok = pl.BlockSpec((tm, D), lambda i: (i, 0))
    bf = jax.ShapeDtypeStruct((T, D), BF16)
    return pl.pallas_call(
        functools.partial(_hg_in_kernel, layer),
        grid=(T // tm,),
        in_specs=[tok, _full((1, D)), _full(lbp.shape), _full(w.shape)],
        out_specs=[tok] * 5,
        out_shape=[bf, bf, bf, bf, jax.ShapeDtypeStruct((T, D), F32)],
        compiler_params=_cparams(("parallel",)),
        name="hg_in",
    )(x, g, lbp, w)


def _hg_rec_kernel(lt, q_ref, k_ref, v_ref, gs_ref, lf_ref, x_ref, on_ref, wo_ref, tri_ref,
                   out_ref, st_ref, b_ref, o_ref):
    H, DK = HG_HEADS, HG_DK

    @pl.when(pl.program_id(1) == 0)
    def _():
        st_ref[...] = jnp.zeros_like(st_ref)

    lf = lf_ref[...]
    hi = lf.astype(BF16)
    r1 = lf - hi.astype(F32)
    mid = r1.astype(BF16)
    lo = (r1 - mid.astype(F32)).astype(BF16)
    tri = tri_ref[...]
    b_ref[...] = _dot(tri, hi) + _dot(tri, mid) + _dot(tri, lo)

    HALF = SUB // 2
    row_h = lax.broadcasted_iota(I32, (HALF, DK), 0)

    def sub_chunk(c, carry):
        r0 = pl.multiple_of(c * SUB, SUB)
        for h in range(H):
            ls = slice(h * DK, (h + 1) * DK)
            b = b_ref[pl.ds(r0, SUB), ls]
            q = q_ref[pl.ds(r0, SUB), ls].astype(F32)
            k = k_ref[pl.ds(r0, SUB), ls].astype(F32)
            v = v_ref[pl.ds(r0, SUB), ls].astype(F32)
            be = b[SUB - 1:SUB, :]
            st = st_ref[h]
            o = _dot_nt((q * jnp.exp2(b)).astype(BF16), st.astype(BF16))
            o_top = jnp.zeros((HALF, DK), F32)
            o_bot = jnp.zeros((HALF, DK), F32)
            for s in range(SUB):
                bs, ks, vs = b[s:s + 1, :], k[s:s + 1, :], v[s:s + 1, :]
                if s < HALF:
                    e = jnp.exp2(jnp.where(row_h >= s, b[:HALF] - bs, NEG_BIG))
                    o_top = o_top + jnp.sum(q[:HALF] * e * ks, axis=-1, keepdims=True) * vs
                    e = jnp.exp2(b[HALF:] - bs)
                else:
                    e = jnp.exp2(jnp.where(row_h >= s - HALF, b[HALF:] - bs, NEG_BIG))
                o_bot = o_bot + jnp.sum(q[HALF:] * e * ks, axis=-1, keepdims=True) * vs
            o_ref[pl.ds(r0, SUB), ls] = o + jnp.concatenate([o_top, o_bot], axis=0)
            kd = (k * jnp.exp2(be - b)).astype(BF16)
            st_ref[h] = st * jnp.exp2(be) + _dot_tn(v.astype(BF16), kd)
        return carry

    lax.fori_loop(0, lt // SUB, sub_chunk, 0)

    on = on_ref[...]
    for h in range(H):
        ls = slice(h * DK, (h + 1) * DK)
        oh = _rms(o_ref[:, ls], on[:, ls]) * gs_ref[:, ls].astype(F32)
        o_ref[:, ls] = oh
    out_ref[...] = x_ref[...] + _dot(o_ref[...].astype(BF16), wo_ref[...])


def _hg_rec(q, k, v, gs, lf, x, onorm, wo, B, L, lt=256):
    T, D = x.shape
    nl = L // lt
    tok = pl.BlockSpec((lt, D), lambda b, j: (b * nl + j, 0))
    tri = (np.arange(lt)[:, None] >= np.arange(lt)[None, :]) & (
        np.arange(lt)[:, None] // SUB == np.arange(lt)[None, :] // SUB)
    tri = jnp.asarray(tri, BF16)
    return pl.pallas_call(
        functools.partial(_hg_rec_kernel, lt),
        grid=(B, nl),
        in_specs=[tok] * 6 + [_full((1, D)), _full((D, D)), _full((lt, lt))],
        out_specs=tok,
        out_shape=jax.ShapeDtypeStruct((T, D), F32),
        scratch_shapes=[pltpu.VMEM((HG_HEADS, HG_DK, HG_DK), F32),
                        pltpu.VMEM((lt, D), F32),
                        pltpu.VMEM((lt, D), F32)],
        compiler_params=_cparams(("arbitrary", "arbitrary")),
        name="hg_rec",
    )(q, k, v, gs, lf, x, onorm, wo, tri)


def _ffn_kernel(nseq, h_ref, g_ref, wg_ref, wv_ref, cwg_ref, cwv_ref, cbg_ref, cbv_ref, wd_ref,
                out_ref, xn_ref, cg_ref, cv_ref):
    i, j = pl.program_id(0), pl.program_id(1)

    @pl.when(j == 0)
    def _():
        xn_ref[...] = _rms(h_ref[...], g_ref[...]).astype(BF16)
        out_ref[...] = h_ref[...]

    seq_start = (i % nseq) == 0
    xn = xn_ref[...]

    def conv(u, w_ref, b_ref, carry_ref):
        fc = u.shape[1]
        carry = jnp.where(seq_start, 0.0, carry_ref[j])
        carry_ref[j] = u[u.shape[0] - 8:, :]
        r1 = pltpu.roll(u, 1, 0)
        r2 = pltpu.roll(u, 2, 0)
        rows = lax.broadcasted_iota(I32, (8, fc), 0)
        top1 = jnp.where(rows < 1, pltpu.roll(carry, 1, 0), r1[0:8])
        top2 = jnp.where(rows < 2, pltpu.roll(carry, 2, 0), r2[0:8])
        p1 = jnp.concatenate([top1, r1[8:]], axis=0)
        p2 = jnp.concatenate([top2, r2[8:]], axis=0)
        w = w_ref[0]
        return u * w[2:3, :] + p1 * w[1:2, :] + p2 * w[0:1, :] + b_ref[0]

    ug = conv(_dot(xn, wg_ref[...]), cwg_ref, cbg_ref, cg_ref)
    uv = conv(_dot(xn, wv_ref[...]), cwv_ref, cbv_ref, cv_ref)
    act = (ug * jax.nn.sigmoid(ug) * uv).astype(BF16)
    out_ref[...] += _dot(act, wd_ref[...])


def _ffn(h, g, w_up, conv_w, conv_b, w_down, L, tm=512, nf=2):
    T, D = h.shape
    F = w_down.shape[0]
    fc = F // nf
    assert fc % 128 == 0 and L % tm == 0
    cw = jnp.pad(conv_w, ((0, 5), (0, 0)))
    cwg = cw[:, :F].reshape(8, nf, fc).transpose(1, 0, 2)
    cwv = cw[:, F:].reshape(8, nf, fc).transpose(1, 0, 2)
    cbg = conv_b[:F].reshape(nf, 1, fc)
    cbv = conv_b[F:].reshape(nf, 1, fc)
    tok = pl.BlockSpec((tm, D), lambda i, j: (i, 0))
    return pl.pallas_call(
        functools.partial(_ffn_kernel, L // tm),
        grid=(T // tm, nf),
        in_specs=[tok, _full((1, D)),
                  pl.BlockSpec((D, fc), lambda i, j: (0, j)),
                  pl.BlockSpec((D, fc), lambda i, j: (0, nf + j)),
                  pl.BlockSpec((1, 8, fc), lambda i, j: (j, 0, 0)),
                  pl.BlockSpec((1, 8, fc), lambda i, j: (j, 0, 0)),
                  pl.BlockSpec((1, 1, fc), lambda i, j: (j, 0, 0)),
                  pl.BlockSpec((1, 1, fc), lambda i, j: (j, 0, 0)),
                  pl.BlockSpec((fc, D), lambda i, j: (j, 0))],
        out_specs=tok,
        out_shape=jax.ShapeDtypeStruct((T, D), F32),
        scratch_shapes=[pltpu.VMEM((tm, D), BF16),
                        pltpu.VMEM((nf, 8, fc), F32),
                        pltpu.VMEM((nf, 8, fc), F32)],
        compiler_params=_cparams(("arbitrary", "arbitrary")),
        name="conv_ffn",
    )(h, g, w_up, w_up, cwg, cwv, cbg, cbv, w_down)


def _ple_kernel(final, h_ref, p_ref, g_ref, wg_ref, wp_ref, fn_ref, out_ref):
    h = h_ref[...]
    gate = jax.nn.sigmoid(_dot(_rms(h, g_ref[...]).astype(BF16), wg_ref[...]))
    h = h + gate * _dot(p_ref[...].astype(BF16), wp_ref[...])
    if final:
        h = _rms(h, fn_ref[...])
    out_ref[...] = h


def _ple(h, p, g, wg, wp, fnorm, final, tm=512):
    T, D = h.shape
    tok = pl.BlockSpec((tm, D), lambda i: (i, 0))
    return pl.pallas_call(
        functools.partial(_ple_kernel, final),
        grid=(T // tm,),
        in_specs=[tok, pl.BlockSpec((tm, PLE_DIM), lambda i: (i, 0)), _full((1, D)),
                  _full((D, D)), _full((PLE_DIM, D)), _full((1, D))],
        out_specs=tok,
        out_shape=jax.ShapeDtypeStruct((T, D), F32),
        compiler_params=_cparams(("parallel",)),
        name="ple",
    )(h, p, g, wg, wp, fnorm)


def _at_in_kernel(h_ref, g_ref, wcq_ref, wckv_ref, wki_ref, wwi_ref, qn_ref, kvn_ref,
                  wuq_ref, wqi_ref, wuk_ref,
                  qlat_ref, qit_ref, wit_ref, ki_ref, ckv_ref, ckvt_ref):
    tm = h_ref.shape[0]
    xn = _rms(h_ref[...], g_ref[...]).astype(BF16)
    cq = _rms(_dot(xn, wcq_ref[...]), qn_ref[...]).astype(BF16)
    ckv = _rms(_dot(xn, wckv_ref[...]), kvn_ref[...])
    ckv_ref[...] = ckv.astype(BF16)
    for c in range(tm // KEY_TILE):
        ckvt_ref[c] = jnp.concatenate([ckv[c * KEY_TILE:(c + 1) * KEY_TILE, :].T.astype(BF16),
                                       jnp.ones((ONES_ROWS, KEY_TILE), BF16)], axis=0)
    ki_ref[...] = _dot(xn, wki_ref[...]).astype(BF16)
    wit_ref[...] = _dot_nt(wwi_ref[...], xn) * (IDX_HEADS ** -0.5 * IDX_DIM ** -0.5)
    qit_ref[...] = _dot_nt(wqi_ref[...], cq).astype(BF16)
    qn = _dot(cq, wuq_ref[...]).astype(BF16)
    for pr in range(AT_HEADS // 2):
        ql = _dot_nt(wuk_ref[pr], qn[:, pr * 128:(pr + 1) * 128]) * (AT_DH ** -0.5 * LOG2E)
        qlat_ref[2 * pr] = ql[:KV_LORA, :].astype(BF16)
        qlat_ref[2 * pr + 1] = ql[KV_LORA:, :].astype(BF16)


def _at_in(h, g, wcq, wckv, wki, wwi_t, qn, kvn, wuq, wqi_t, wuk_bd_t, tm=512):
    T, D = h.shape
    nk = tm // KEY_TILE
    ins = [h, g, wcq, wckv, wki, wwi_t, qn, kvn, wuq, wqi_t, wuk_bd_t]
    in_specs = [pl.BlockSpec((tm, D), lambda i: (i, 0))] + [_full(a.shape) for a in ins[1:]]
    return pl.pallas_call(
        _at_in_kernel,
        grid=(T // tm,),
        in_specs=in_specs,
        out_specs=[pl.BlockSpec((AT_HEADS, KV_LORA, tm), lambda i: (0, 0, i)),
                   pl.BlockSpec((IDX_HEADS * IDX_DIM, tm), lambda i: (0, i)),
                   pl.BlockSpec((IDX_HEADS, tm), lambda i: (0, i)),
                   pl.BlockSpec((tm, IDX_DIM), lambda i: (i, 0)),
                   pl.BlockSpec((tm, KV_LORA), lambda i: (i, 0)),
                   pl.BlockSpec((nk, KV_LORA + ONES_ROWS, KEY_TILE), lambda i: (i, 0, 0))],
        out_shape=[jax.ShapeDtypeStruct((AT_HEADS, KV_LORA, T), BF16),
                   jax.ShapeDtypeStruct((IDX_HEADS * IDX_DIM, T), BF16),
                   jax.ShapeDtypeStruct((IDX_HEADS, T), F32),
                   jax.ShapeDtypeStruct((T, IDX_DIM), BF16),
                   jax.ShapeDtypeStruct((T, KV_LORA), BF16),
                   jax.ShapeDtypeStruct((T // KEY_TILE, KV_LORA + ONES_ROWS, KEY_TILE), BF16)],
        compiler_params=_cparams(("parallel",)),
        name="at_in",
    )(*ins)


def _at_core_kernel(topk, qlat_ref, qit_ref, wit_ref, ki_ref, ckv_ref, ckvt_ref, bias_ref,
                    wuvt_ref, wo_ref, h_ref, out_ref, ik_ref, mask_ref, acc_ref, ot_ref):
    H, TK = AT_HEADS, KEY_TILE
    qb = pl.program_id(1)
    t0 = qb * QBLK
    nkt = qb // (TK // QBLK) + 1
    lane_q = lax.broadcasted_iota(I32, (TK, QBLK), 1)
    row_k = lax.broadcasted_iota(I32, (TK, QBLK), 0)

    wit = wit_ref[...]

    def score_tile(kt, carry):
        k0 = pl.multiple_of(kt * TK, TK)
        kk = ki_ref[pl.ds(k0, TK), :]
        acc = jnp.zeros((TK, 2 * QBLK), F32)
        for pr in range(IDX_HEADS // 2):
            qa = qit_ref[(2 * pr) * IDX_DIM:(2 * pr + 1) * IDX_DIM, :]
            qb_ = qit_ref[(2 * pr + 1) * IDX_DIM:(2 * pr + 2) * IDX_DIM, :]
            rel = jnp.maximum(_dot(kk, jnp.concatenate([qa, qb_], axis=1)), 0.0)
            w2 = jnp.concatenate([wit[2 * pr:2 * pr + 1, :], wit[2 * pr + 1:2 * pr + 2, :]], axis=1)
            acc = acc + rel * w2
        score = acc[:, :QBLK] + acc[:, QBLK:] + 0.0
        bits = pltpu.bitcast(score, I32)
        ikey = bits ^ ((bits >> 31) & 0x7FFFFFFF)
        causal = (k0 + row_k) <= (t0 + lane_q)
        ik_ref[pl.ds(k0, TK), :] = jnp.where(causal, ikey, INT_MIN)
        return carry

    lax.fori_loop(0, nkt, score_tile, 0)

    CT = 2 * TK
    nct = (nkt + 1) // 2

    @pl.when(nkt % 2 == 1)
    def _():
        ik_ref[pl.ds(pl.multiple_of(nkt * TK, TK), TK), :] = jnp.full((TK, QBLK), INT_MIN, I32)

    row_c = lax.broadcasted_iota(I32, (CT, QBLK), 0)

    def count(pred):
        def body(ct, c):
            k0 = pl.multiple_of(ct * CT, CT)
            hit = jnp.where(pred(ik_ref[pl.ds(k0, CT), :], k0 + row_c), 1, 0)
            return c + jnp.sum(hit.reshape(CT // 8, 8, QBLK), axis=0)
        c = lax.fori_loop(0, nct, body, jnp.zeros((8, QBLK), I32))
        return jnp.sum(c, axis=0, keepdims=True)

    def bit_step(i, tau):
        cand = jnp.where(i == 0, jnp.zeros_like(tau), tau | (1 << (31 - i)))
        return jnp.where(count(lambda x, k0: x >= cand) >= topk, cand, tau)

    tau = lax.fori_loop(0, 32, bit_step, jnp.full((1, QBLK), INT_MIN, I32))
    n_gt = count(lambda x, k0: x > tau)
    n_ge = count(lambda x, k0: x >= tau)
    need = topk - n_gt

    def tie_search(_):
        def step(i, jc):
            cand = jc + (1 << (12 - i))
            n = count(lambda x, kidx: (x == tau) & (kidx < cand))
            return jnp.where(n <= need, cand, jc)
        return lax.fori_loop(0, 13, step, jnp.zeros((1, QBLK), I32))

    jcut = lax.cond(jnp.max(n_ge) > topk, tie_search,
                    lambda _: jnp.full((1, QBLK), 2 ** 13, I32), 0)

    def mask_tile(kt, carry):
        k0 = pl.multiple_of(kt * TK, TK)
        x = ik_ref[pl.ds(k0, TK), :]
        kidx = k0 + row_k
        sel = ((x > tau) | ((x == tau) & (kidx < jcut))) & (kidx <= (t0 + lane_q))
        mask_ref[kt] = jnp.where(sel, 0.0, NEG_BIG)
        return carry

    lax.fori_loop(0, nkt, mask_tile, 0)

    nfar = jnp.maximum(nkt - 2, 0)

    NP = H // 2
    acc_ref[...] = jnp.zeros_like(acc_ref)

    def tile(near, kt, m_all):
        k0 = pl.multiple_of(kt * TK, TK)
        am = mask_ref[kt]
        am2 = jnp.concatenate([am, am], axis=1)
        ck = ckv_ref[pl.ds(k0, TK), :]
        ckt = ckvt_ref[kt]
        ms = []
        for pr in range(NP):
            qt = jnp.concatenate([qlat_ref[2 * pr], qlat_ref[2 * pr + 1]], axis=1)
            lg = _dot(ck, qt) + am2
            if near:
                lg = lg + bias_ref[jnp.minimum((t0 - k0) // QBLK, NEAR_TILES), pr]
            m_old = m_all[pr:pr + 1, :]
            m_new = jnp.maximum(m_old, jnp.max(lg, axis=0, keepdims=True))
            p = jnp.exp2((lg - m_new).astype(BF16))
            acc_ref[pr] = acc_ref[pr] * jnp.exp2(m_old - m_new) + _dot(ckt, p)
            ms.append(m_new)
        return jnp.concatenate(ms, axis=0)

    m_all = jnp.full((NP, 2 * QBLK), NEG_BIG, F32)
    m_all = lax.fori_loop(0, nfar, functools.partial(tile, False), m_all)
    m_all = lax.fori_loop(nfar, nkt, functools.partial(tile, True), m_all)
    for pr in range(NP):
        a = acc_ref[pr]
        olat = (a[:KV_LORA] / a[KV_LORA:KV_LORA + 1]).astype(BF16)
        ot_ref[pr * 2 * AT_DH:(pr + 1) * 2 * AT_DH, :] = jnp.concatenate(
            [_dot(wuvt_ref[2 * pr], olat[:, :QBLK]), _dot(wuvt_ref[2 * pr + 1], olat[:, QBLK:])], axis=0)
    out_ref[...] = h_ref[...] + _dot(ot_ref[...].T.astype(BF16), wo_ref[...])


def _at_core(qlat, qit, wit, ki, ckv, ckvt, bias, wuvt, wo, h, B, L, topk):
    T, D = h.shape
    nblk = L // QBLK
    nkt = L // KEY_TILE
    H = AT_HEADS
    assert L % (2 * KEY_TILE) == 0
    return pl.pallas_call(
        functools.partial(_at_core_kernel, topk),
        grid=(B, nblk),
        in_specs=[pl.BlockSpec((H, KV_LORA, QBLK), lambda b, q: (0, 0, b * nblk + q)),
                  pl.BlockSpec((IDX_HEADS * IDX_DIM, QBLK), lambda b, q: (0, b * nblk + q)),
                  pl.BlockSpec((IDX_HEADS, QBLK), lambda b, q: (0, b * nblk + q)),
                  pl.BlockSpec((L, IDX_DIM), lambda b, q: (b, 0)),
                  pl.BlockSpec((L, KV_LORA), lambda b, q: (b, 0)),
                  pl.BlockSpec((nkt, KV_LORA + ONES_ROWS, KEY_TILE), lambda b, q: (b, 0, 0)),
                  _full(bias.shape, single=True), _full(wuvt.shape, single=True),
                  _full(wo.shape, single=True),
                  pl.BlockSpec((QBLK, D), lambda b, q: (b * nblk + q, 0))],
        out_specs=pl.BlockSpec((QBLK, D), lambda b, q: (b * nblk + q, 0)),
        out_shape=jax.ShapeDtypeStruct((T, D), F32),
        scratch_shapes=[pltpu.VMEM((L, QBLK), I32),
                        pltpu.VMEM((nkt, KEY_TILE, QBLK), F32),
                        pltpu.VMEM((H // 2, KV_LORA + ONES_ROWS, 2 * QBLK), F32),
                        pltpu.VMEM((H * AT_DH, QBLK), F32)],
        compiler_params=_cparams(("arbitrary", "arbitrary")),
        name="at_core",
    )(qlat, qit, wit, ki, ckv, ckvt, bias, wuvt, wo, h)


def _rel_bucket(n):
    max_exact = REL_BUCKETS // 2
    nf = jnp.maximum(n, 1).astype(F32)
    large = max_exact + (jnp.log(nf / max_exact) / math.log(REL_MAX_DIST / max_exact)
                         * (REL_BUCKETS - max_exact)).astype(I32)
    large = jnp.minimum(large, REL_BUCKETS - 1)
    return jnp.where(n < max_exact, n, large)


def _bias_tiles(rel_bias):
    H = rel_bias.shape[1]
    span = KEY_TILE + QBLK - 1
    m = jnp.arange(NEAR_TILES * QBLK + span, dtype=I32)
    tab = rel_bias[_rel_bucket(jnp.maximum(m - (KEY_TILE - 1), 0))] - rel_bias[REL_BUCKETS - 1]
    tiles = []
    for d in range(NEAR_TILES):
        u = tab[d * QBLK:d * QBLK + span][::-1]
        r = jnp.tile(u, (KEY_TILE + 1, 1))[:KEY_TILE * (span + 1)].reshape(KEY_TILE, span + 1, H)
        tiles.append(r[:, :QBLK][:, ::-1])
    tiles.append(jnp.zeros_like(tiles[0]))
    t = jnp.stack(tiles).astype(F32) * LOG2E
    t = jnp.transpose(t, (0, 3, 1, 2))
    return jnp.concatenate([t[:, 0::2], t[:, 1::2]], axis=-1)


def kernel(x, p, hg_norm, hg_w_in, hg_lb, hg_onorm, hg_w_out, at_norm, at_w_in, at_q_norm,
           at_kv_norm, at_w_uq, at_w_uk, at_w_uv, at_w_qidx, at_w_out, rel_bias, ff_norm, ff_w_up,
           ff_conv_w, ff_conv_b, ff_w_down, ple_norm, ple_w_gate, ple_w_proj, final_norm):
    B, L, D = x.shape
    T = B * L
    depth = p.shape[0]
    row = lambda a: a.reshape(1, -1)
    h = x.reshape(T, D)
    for i in range(depth):
        j = i // 2
        if i % 2 == 0:
            q, k, v, gs, lf = _hg_in(h, row(hg_norm[j]), hg_lb, hg_w_in[j].astype(BF16), i)
            h = _hg_rec(q, k, v, gs, lf, h, row(hg_onorm[j]), hg_w_out[j].astype(BF16), B, L)
        else:
            w_in = at_w_in[j].astype(BF16)
            o1, o2, o3 = Q_LORA, Q_LORA + KV_LORA, Q_LORA + KV_LORA + IDX_DIM
            wuk = at_w_uk[j].astype(BF16)
            z = jnp.zeros_like(wuk[0::2])
            wuk_bd = jnp.concatenate([jnp.concatenate([wuk[0::2], z], axis=2),
                                      jnp.concatenate([z, wuk[1::2]], axis=2)], axis=1)
            wuvt = jnp.transpose(at_w_uv[j].astype(BF16), (0, 2, 1))
            qlat, qit, wit, ki, ckv, ckvt = _at_in(
                h, row(at_norm[j]), w_in[:, :o1], w_in[:, o1:o2], w_in[:, o2:o3], w_in[:, o3:].T,
                row(at_q_norm[j]), row(at_kv_norm[j]), at_w_uq[j].astype(BF16),
                at_w_qidx[j].astype(BF16).T, jnp.transpose(wuk_bd, (0, 2, 1)))
            topk = max(1, min(TOPK_MAX, L // 4))
            h = _at_core(qlat, qit, wit, ki, ckv, ckvt, _bias_tiles(rel_bias), wuvt,
                         at_w_out[j].astype(BF16), h, B, L, topk)
        h = _ffn(h, row(ff_norm[i]), ff_w_up[i].astype(BF16), ff_conv_w[i], ff_conv_b[i],
                 ff_w_down[i].astype(BF16), L)
        h = _ple(h, p[i].reshape(T, PLE_DIM), row(ple_norm[i]), ple_w_gate[i].astype(BF16),
                 ple_w_proj[i].astype(BF16), row(final_norm), final=(i == depth - 1))
    return h.reshape(B, L, D)
```

```python
import functools
import math

import jax
import jax.numpy as jnp
import numpy as np
from jax import lax
from jax.experimental import pallas as pl
from jax.experimental.pallas import tpu as pltpu

F32 = jnp.float32
BF16 = jnp.bfloat16
I32 = jnp.int32

D_MODEL = 1024
HG_HEADS = 8
HG_DK = D_MODEL // HG_HEADS
AT_HEADS = 16
AT_DH = 64
Q_LORA = 384
KV_LORA = 256
IDX_HEADS = 8
IDX_DIM = 64
TOPK_MAX = 256
QBLK = 128
REL_BUCKETS = 32
REL_MAX_DIST = 128
D_FF = 2816
PLE_DIM = 256
EPS = 1e-6

VMEM_LIMIT_V7X = 56 * 1024 * 1024
SUB = 16
KEY_TILE = 256
NEAR_TILES = (KEY_TILE + REL_MAX_DIST) // QBLK
NEG_BIG = -1e30
LOG2E = 1.4426950408889634
ONES_ROWS = 16
INT_MIN = -(2 ** 31)


def _cparams(sem):
    return pltpu.CompilerParams(dimension_semantics=sem, vmem_limit_bytes=VMEM_LIMIT_V7X)


def _rms(x, g):
    return x * lax.rsqrt(jnp.mean(x * x, axis=-1, keepdims=True) + EPS) * g


def _dot(a, b):
    return jnp.dot(a, b, preferred_element_type=F32)


def _dot_nt(a, b):
    return lax.dot_general(a, b, (((1,), (1,)), ((), ())), preferred_element_type=F32)


def _dot_tn(a, b):
    return lax.dot_general(a, b, (((0,), (0,)), ((), ())), preferred_element_type=F32)


def _full(shape, single=False):
    n = len(shape)
    if single:
        return pl.BlockSpec(shape, lambda *_: (0,) * n, pipeline_mode=pl.Buffered(1))
    return pl.BlockSpec(shape, lambda *_: (0,) * n)


def _hg_in_kernel(layer, x_ref, g_ref, lbp_ref, w_ref, q_ref, k_ref, v_ref, gs_ref, lf_ref):
    D = D_MODEL
    xn = _rms(x_ref[...], g_ref[...]).astype(BF16)
    lbp = lbp_ref[...]
    e = jnp.exp(lbp - jnp.max(lbp, axis=0, keepdims=True))
    lb = jnp.sum(e[: layer + 1], axis=0, keepdims=True) / jnp.sum(e, axis=0, keepdims=True)
    q = _dot(xn, w_ref[:, 0:D])
    q_ref[...] = (q * jax.nn.sigmoid(q)).astype(BF16)
    f = lb + (1.0 - lb) * jax.nn.sigmoid(_dot(xn, w_ref[:, D:2 * D]))
    lf_ref[...] = jnp.log(f) * LOG2E
    k_ref[...] = (1.0 - f).astype(BF16)
    v_ref[...] = _dot(xn, w_ref[:, 2 * D:3 * D]).astype(BF16)
    g = _dot(xn, w_ref[:, 3 * D:4 * D])
    gs_ref[...] = (g * jax.nn.sigmoid(g)).astype(BF16)


def _hg_in(x, g, lbp, w, layer, tm=512):
    T, D = x.shape
    tok = pl.BlockSpec((tm, D), lambda i: (i, 0))
    bf = jax.ShapeDtypeStruct((T, D), BF16)
    return pl.pallas_call(
        functools.partial(_hg_in_kernel, layer),
        grid=(T // tm,),
        in_specs=[tok, _full((1, D)), _full(lbp.shape), _full(w.shape)],
        out_specs=[tok] * 5,
        out_shape=[bf, bf, bf, bf, jax.ShapeDtypeStruct((T, D), F32)],
        compiler_params=_cparams(("parallel",)),
        name="hg_in",
    )(x, g, lbp, w)


def _hg_rec_kernel(lt, q_ref, k_ref, v_ref, gs_ref, lf_ref, x_ref, on_ref, wo_ref, tri_ref,
                   out_ref, st_ref, b_ref, o_ref):
    H, DK = HG_HEADS, HG_DK

    @pl.when(pl.program_id(1) == 0)
    def _():
        st_ref[...] = jnp.zeros_like(st_ref)

    lf = lf_ref[...]
    hi = lf.astype(BF16)
    r1 = lf - hi.astype(F32)
    mid = r1.astype(BF16)
    lo = (r1 - mid.astype(F32)).astype(BF16)
    tri = tri_ref[...]
    b_ref[...] = _dot(tri, hi) + _dot(tri, mid) + _dot(tri, lo)

    HALF = SUB // 2
    row_h = lax.broadcasted_iota(I32, (HALF, DK), 0)

    def sub_chunk(c, carry):
        r0 = pl.multiple_of(c * SUB, SUB)
        for h in range(H):
            ls = slice(h * DK, (h + 1) * DK)
            b = b_ref[pl.ds(r0, SUB), ls]
            q = q_ref[pl.ds(r0, SUB), ls].astype(F32)
            k = k_ref[pl.ds(r0, SUB), ls].astype(F32)
            v = v_ref[pl.ds(r0, SUB), ls].astype(F32)
            be = b[SUB - 1:SUB, :]
            st = st_ref[h]
            o = _dot_nt((q * jnp.exp2(b)).astype(BF16), st.astype(BF16))
            o_top = jnp.zeros((HALF, DK), F32)
            o_bot = jnp.zeros((HALF, DK), F32)
            for s in range(SUB):
                bs, ks, vs = b[s:s + 1, :], k[s:s + 1, :], v[s:s + 1, :]
                if s < HALF:
                    e = jnp.exp2(jnp.where(row_h >= s, b[:HALF] - bs, NEG_BIG))
                    o_top = o_top + jnp.sum(q[:HALF] * e * ks, axis=-1, keepdims=True) * vs
                    e = jnp.exp2(b[HALF:] - bs)
                else:
                    e = jnp.exp2(jnp.where(row_h >= s - HALF, b[HALF:] - bs, NEG_BIG))
                o_bot = o_bot + jnp.sum(q[HALF:] * e * ks, axis=-1, keepdims=True) * vs
            o_ref[pl.ds(r0, SUB), ls] = o + jnp.concatenate([o_top, o_bot], axis=0)
            kd = (k * jnp.exp2(be - b)).astype(BF16)
            st_ref[h] = st * jnp.exp2(be) + _dot_tn(v.astype(BF16), kd)
        return carry

    lax.fori_loop(0, lt // SUB, sub_chunk, 0)

    on = on_ref[...]
    for h in range(H):
        ls = slice(h * DK, (h + 1) * DK)
        oh = _rms(o_ref[:, ls], on[:, ls]) * gs_ref[:, ls].astype(F32)
        o_ref[:, ls] = oh
    out_ref[...] = x_ref[...] + _dot(o_ref[...].astype(BF16), wo_ref[...])


def _hg_rec(q, k, v, gs, lf, x, onorm, wo, B, L, lt=256):
    T, D = x.shape
    nl = L // lt
    tok = pl.BlockSpec((lt, D), lambda b, j: (b * nl + j, 0))
    tri = (np.arange(lt)[:, None] >= np.arange(lt)[None, :]) & (
        np.arange(lt)[:, None] // SUB == np.arange(lt)[None, :] // SUB)
    tri = jnp.asarray(tri, BF16)
    return pl.pallas_call(
        functools.partial(_hg_rec_kernel, lt),
        grid=(B, nl),
        in_specs=[tok] * 6 + [_full((1, D)), _full((D, D)), _full((lt, lt))],
        out_specs=tok,
        out_shape=jax.ShapeDtypeStruct((T, D), F32),
        scratch_shapes=[pltpu.VMEM((HG_HEADS, HG_DK, HG_DK), F32),
                        pltpu.VMEM((lt, D), F32),
                        pltpu.VMEM((lt, D), F32)],
        compiler_params=_cparams(("arbitrary", "arbitrary")),
        name="hg_rec",
    )(q, k, v, gs, lf, x, onorm, wo, tri)


def _ffn_kernel(nseq, nf, final, h_ref, g_ref, wg_ref, wv_ref, cwg_ref, cwv_ref, cbg_ref, cbv_ref,
                wd_ref, p_ref, pg_ref, pwg_ref, pwp_ref, fn_ref, out_ref, xn_ref, cg_ref, cv_ref):
    i, j = pl.program_id(0), pl.program_id(1)

    @pl.when(j == 0)
    def _():
        xn_ref[...] = _rms(h_ref[...], g_ref[...]).astype(BF16)
        out_ref[...] = h_ref[...]

    seq_start = (i % nseq) == 0
    xn = xn_ref[...]

    def conv(u, w_ref, b_ref, carry_ref):
        fc = u.shape[1]
        carry = jnp.where(seq_start, 0.0, carry_ref[j])
        carry_ref[j] = u[u.shape[0] - 8:, :]
        r1 = pltpu.roll(u, 1, 0)
        r2 = pltpu.roll(u, 2, 0)
        rows = lax.broadcasted_iota(I32, (8, fc), 0)
        top1 = jnp.where(rows < 1, pltpu.roll(carry, 1, 0), r1[0:8])
        top2 = jnp.where(rows < 2, pltpu.roll(carry, 2, 0), r2[0:8])
        p1 = jnp.concatenate([top1, r1[8:]], axis=0)
        p2 = jnp.concatenate([top2, r2[8:]], axis=0)
        w = w_ref[0]
        return u * w[2:3, :] + p1 * w[1:2, :] + p2 * w[0:1, :] + b_ref[0]

    ug = conv(_dot(xn, wg_ref[...]), cwg_ref, cbg_ref, cg_ref)
    uv = conv(_dot(xn, wv_ref[...]), cwv_ref, cbv_ref, cv_ref)
    act = (ug * jax.nn.sigmoid(ug) * uv).astype(BF16)
    out_ref[...] += _dot(act, wd_ref[...])

    @pl.when(j == nf - 1)
    def _():
        h = out_ref[...]
        gate = jax.nn.sigmoid(_dot(_rms(h, pg_ref[...]).astype(BF16), pwg_ref[...]))
        h = h + gate * _dot(p_ref[...].astype(BF16), pwp_ref[...])
        if final:
            h = _rms(h, fn_ref[...])
        out_ref[...] = h


def _ffn_ple(h, g, w_up, conv_w, conv_b, w_down, p, pg, pwg, pwp, fnorm, final, L, tm=512, nf=2):
    T, D = h.shape
    F = w_down.shape[0]
    fc = F // nf
    assert fc % 128 == 0 and L % tm == 0
    cw = jnp.pad(conv_w, ((0, 5), (0, 0)))
    cwg = cw[:, :F].reshape(8, nf, fc).transpose(1, 0, 2)
    cwv = cw[:, F:].reshape(8, nf, fc).transpose(1, 0, 2)
    cbg = conv_b[:F].reshape(nf, 1, fc)
    cbv = conv_b[F:].reshape(nf, 1, fc)
    tok = pl.BlockSpec((tm, D), lambda i, j: (i, 0))
    return pl.pallas_call(
        functools.partial(_ffn_kernel, L // tm, nf, final),
        grid=(T // tm, nf),
        in_specs=[tok, _full((1, D)),
                  pl.BlockSpec((D, fc), lambda i, j: (0, j)),
                  pl.BlockSpec((D, fc), lambda i, j: (0, nf + j)),
                  pl.BlockSpec((1, 8, fc), lambda i, j: (j, 0, 0)),
                  pl.BlockSpec((1, 8, fc), lambda i, j: (j, 0, 0)),
                  pl.BlockSpec((1, 1, fc), lambda i, j: (j, 0, 0)),
                  pl.BlockSpec((1, 1, fc), lambda i, j: (j, 0, 0)),
                  pl.BlockSpec((fc, D), lambda i, j: (j, 0)),
                  pl.BlockSpec((tm, PLE_DIM), lambda i, j: (i, 0)), _full((1, D)),
                  _full((D, D), single=True), _full((PLE_DIM, D), single=True), _full((1, D))],
        out_specs=tok,
        out_shape=jax.ShapeDtypeStruct((T, D), F32),
        scratch_shapes=[pltpu.VMEM((tm, D), BF16),
                        pltpu.VMEM((nf, 8, fc), F32),
                        pltpu.VMEM((nf, 8, fc), F32)],
        compiler_params=_cparams(("arbitrary", "arbitrary")),
        name="conv_ffn_ple",
    )(h, g, w_up, w_up, cwg, cwv, cbg, cbv, w_down, p, pg, pwg, pwp, fnorm)


def _at_in_kernel(h_ref, g_ref, wcq_ref, wckv_ref, wki_ref, wwi_ref, qn_ref, kvn_ref,
                  wuq_ref, wqi_ref, wuk_ref,
                  qlat_ref, qit_ref, wit_ref, ki_ref, ckv_ref, ckvt_ref):
    tm = h_ref.shape[0]
    xn = _rms(h_ref[...], g_ref[...]).astype(BF16)
    cq = _rms(_dot(xn, wcq_ref[...]), qn_ref[...]).astype(BF16)
    ckv = _rms(_dot(xn, wckv_ref[...]), kvn_ref[...])
    ckv_ref[...] = ckv.astype(BF16)
    for c in range(tm // KEY_TILE):
        ckvt_ref[c] = jnp.concatenate([ckv[c * KEY_TILE:(c + 1) * KEY_TILE, :].T.astype(BF16),
                                       jnp.ones((ONES_ROWS, KEY_TILE), BF16)], axis=0)
    ki_ref[...] = _dot(xn, wki_ref[...]).astype(BF16)
    nq = tm // QBLK
    wit = _dot_nt(wwi_ref[...], xn) * (IDX_HEADS ** -0.5 * IDX_DIM ** -0.5)
    qit = _dot_nt(wqi_ref[...], cq).astype(BF16)
    for c in range(nq):
        wit_ref[c] = wit[:, c * QBLK:(c + 1) * QBLK]
        qit_ref[c] = qit[:, c * QBLK:(c + 1) * QBLK]
    qn = _dot(cq, wuq_ref[...]).astype(BF16)
    for pr in range(AT_HEADS // 2):
        ql = (_dot_nt(wuk_ref[pr], qn[:, pr * 128:(pr + 1) * 128])
              * (AT_DH ** -0.5 * LOG2E)).astype(BF16)
        for c in range(nq):
            qlat_ref[c, 2 * pr] = ql[:KV_LORA, c * QBLK:(c + 1) * QBLK]
            qlat_ref[c, 2 * pr + 1] = ql[KV_LORA:, c * QBLK:(c + 1) * QBLK]


def _at_in(h, g, wcq, wckv, wki, wwi_t, qn, kvn, wuq, wqi_t, wuk_bd_t, tm=512):
    T, D = h.shape
    nk = tm // KEY_TILE
    ins = [h, g, wcq, wckv, wki, wwi_t, qn, kvn, wuq, wqi_t, wuk_bd_t]
    in_specs = [pl.BlockSpec((tm, D), lambda i: (i, 0))] + [_full(a.shape) for a in ins[1:]]
    return pl.pallas_call(
        _at_in_kernel,
        grid=(T // tm,),
        in_specs=in_specs,
        out_specs=[pl.BlockSpec((tm // QBLK, AT_HEADS, KV_LORA, QBLK), lambda i: (i, 0, 0, 0)),
                   pl.BlockSpec((tm // QBLK, IDX_HEADS * IDX_DIM, QBLK), lambda i: (i, 0, 0)),
                   pl.BlockSpec((tm // QBLK, IDX_HEADS, QBLK), lambda i: (i, 0, 0)),
                   pl.BlockSpec((tm, IDX_DIM), lambda i: (i, 0)),
                   pl.BlockSpec((tm, KV_LORA), lambda i: (i, 0)),
                   pl.BlockSpec((nk, KV_LORA + ONES_ROWS, KEY_TILE), lambda i: (i, 0, 0))],
        out_shape=[jax.ShapeDtypeStruct((T // QBLK, AT_HEADS, KV_LORA, QBLK), BF16),
                   jax.ShapeDtypeStruct((T // QBLK, IDX_HEADS * IDX_DIM, QBLK), BF16),
                   jax.ShapeDtypeStruct((T // QBLK, IDX_HEADS, QBLK), F32),
                   jax.ShapeDtypeStruct((T, IDX_DIM), BF16),
                   jax.ShapeDtypeStruct((T, KV_LORA), BF16),
                   jax.ShapeDtypeStruct((T // KEY_TILE, KV_LORA + ONES_ROWS, KEY_TILE), BF16)],
        compiler_params=_cparams(("parallel",)),
        name="at_in",
    )(*ins)


def _at_core_kernel(topk, qlat_ref, qit_ref, wit_ref, ki_ref, ckv_ref, ckvt_ref, bias_ref,
                    wuvt_ref, wo_ref, h_ref, out_ref, ik_ref, mask_ref, acc_ref, ot_ref):
    H, TK = AT_HEADS, KEY_TILE
    qb = pl.program_id(1)
    t0 = qb * QBLK
    nkt = qb // (TK // QBLK) + 1
    lane_q = lax.broadcasted_iota(I32, (TK, QBLK), 1)
    row_k = lax.broadcasted_iota(I32, (TK, QBLK), 0)

    wit = wit_ref[...]

    def score_tile(kt, carry):
        k0 = pl.multiple_of(kt * TK, TK)
        kk = ki_ref[pl.ds(k0, TK), :]
        acc = jnp.zeros((TK, 2 * QBLK), F32)
        for pr in range(IDX_HEADS // 2):
            qa = qit_ref[(2 * pr) * IDX_DIM:(2 * pr + 1) * IDX_DIM, :]
            qb_ = qit_ref[(2 * pr + 1) * IDX_DIM:(2 * pr + 2) * IDX_DIM, :]
            rel = jnp.maximum(_dot(kk, jnp.concatenate([qa, qb_], axis=1)), 0.0)
            w2 = jnp.concatenate([wit[2 * pr:2 * pr + 1, :], wit[2 * pr + 1:2 * pr + 2, :]], axis=1)
            acc = acc + rel * w2
        score = acc[:, :QBLK] + acc[:, QBLK:] + 0.0
        bits = pltpu.bitcast(score, I32)
        ikey = bits ^ ((bits >> 31) & 0x7FFFFFFF)
        causal = (k0 + row_k) <= (t0 + lane_q)
        ik_ref[pl.ds(k0, TK), :] = jnp.where(causal, ikey, INT_MIN)
        return carry

    lax.fori_loop(0, nkt, score_tile, 0)

    CT = 2 * TK
    nct = (nkt + 1) // 2

    @pl.when(nkt % 2 == 1)
    def _():
        ik_ref[pl.ds(pl.multiple_of(nkt * TK, TK), TK), :] = jnp.full((TK, QBLK), INT_MIN, I32)

    row_c = lax.broadcasted_iota(I32, (CT, QBLK), 0)

    def count(pred):
        def body(ct, c):
            k0 = pl.multiple_of(ct * CT, CT)
            hit = jnp.where(pred(ik_ref[pl.ds(k0, CT), :], k0 + row_c), 1, 0)
            return c + jnp.sum(hit.reshape(CT // 8, 8, QBLK), axis=0)
        c = lax.fori_loop(0, nct, body, jnp.zeros((8, QBLK), I32))
        return jnp.sum(c, axis=0, keepdims=True)

    def bit_step(i, tau):
        cand = jnp.where(i == 0, jnp.zeros_like(tau), tau | (1 << (31 - i)))
        return jnp.where(count(lambda x, k0: x >= cand) >= topk, cand, tau)

    tau = lax.fori_loop(0, 32, bit_step, jnp.full((1, QBLK), INT_MIN, I32))
    n_gt = count(lambda x, k0: x > tau)
    n_ge = count(lambda x, k0: x >= tau)
    need = topk - n_gt

    def tie_search(_):
        def step(i, jc):
            cand = jc + (1 << (12 - i))
            n = count(lambda x, kidx: (x == tau) & (kidx < cand))
            return jnp.where(n <= need, cand, jc)
        return lax.fori_loop(0, 13, step, jnp.zeros((1, QBLK), I32))

    jcut = lax.cond(jnp.max(n_ge) > topk, tie_search,
                    lambda _: jnp.full((1, QBLK), 2 ** 13, I32), 0)

    def mask_tile(kt, carry):
        k0 = pl.multiple_of(kt * TK, TK)
        x = ik_ref[pl.ds(k0, TK), :]
        kidx = k0 + row_k
        sel = ((x > tau) | ((x == tau) & (kidx < jcut))) & (kidx <= (t0 + lane_q))
        mask_ref[kt] = jnp.where(sel, 0.0, NEG_BIG)
        return carry

    lax.fori_loop(0, nkt, mask_tile, 0)

    nfar = jnp.maximum(nkt - 2, 0)

    NP = H // 2
    acc_ref[...] = jnp.zeros_like(acc_ref)

    def tile(near, kt, m_all):
        k0 = pl.multiple_of(kt * TK, TK)
        am = mask_ref[kt]
        am2 = jnp.concatenate([am, am], axis=1)
        ck = ckv_ref[pl.ds(k0, TK), :]
        ckt = ckvt_ref[kt]
        ms = []
        for pr in range(NP):
            qt = jnp.concatenate([qlat_ref[2 * pr], qlat_ref[2 * pr + 1]], axis=1)
            lg = _dot(ck, qt) + am2
            if near:
                lg = lg + bias_ref[jnp.minimum((t0 - k0) // QBLK, NEAR_TILES), pr]
            m_old = m_all[pr:pr + 1, :]
            m_new = jnp.maximum(m_old, jnp.max(lg, axis=0, keepdims=True))
            p = jnp.exp2((lg - m_new).astype(BF16))
            acc_ref[pr] = acc_ref[pr] * jnp.exp2(m_old - m_new) + _dot(ckt, p)
            ms.append(m_new)
        return jnp.concatenate(ms, axis=0)

    m_all = jnp.full((NP, 2 * QBLK), NEG_BIG, F32)
    m_all = lax.fori_loop(0, nfar, functools.partial(tile, False), m_all)
    m_all = lax.fori_loop(nfar, nkt, functools.partial(tile, True), m_all)
    for pr in range(NP):
        a = acc_ref[pr]
        olat = (a[:KV_LORA] / a[KV_LORA:KV_LORA + 1]).astype(BF16)
        ot_ref[pr * 2 * AT_DH:(pr + 1) * 2 * AT_DH, :] = jnp.concatenate(
            [_dot(wuvt_ref[2 * pr], olat[:, :QBLK]), _dot(wuvt_ref[2 * pr + 1], olat[:, QBLK:])], axis=0)
    out_ref[...] = h_ref[...] + _dot(ot_ref[...].T.astype(BF16), wo_ref[...])


def _at_core(qlat, qit, wit, ki, ckv, ckvt, bias, wuvt, wo, h, B, L, topk):
    T, D = h.shape
    nblk = L // QBLK
    nkt = L // KEY_TILE
    H = AT_HEADS
    assert L % (2 * KEY_TILE) == 0
    return pl.pallas_call(
        functools.partial(_at_core_kernel, topk),
        grid=(B, nblk),
        in_specs=[pl.BlockSpec((None, H, KV_LORA, QBLK), lambda b, q: (b * nblk + q, 0, 0, 0)),
                  pl.BlockSpec((None, IDX_HEADS * IDX_DIM, QBLK), lambda b, q: (b * nblk + q, 0, 0)),
                  pl.BlockSpec((None, IDX_HEADS, QBLK), lambda b, q: (b * nblk + q, 0, 0)),
                  pl.BlockSpec((L, IDX_DIM), lambda b, q: (b, 0)),
                  pl.BlockSpec((L, KV_LORA), lambda b, q: (b, 0)),
                  pl.BlockSpec((nkt, KV_LORA + ONES_ROWS, KEY_TILE), lambda b, q: (b, 0, 0)),
                  _full(bias.shape, single=True), _full(wuvt.shape, single=True),
                  _full(wo.shape, single=True),
                  pl.BlockSpec((QBLK, D), lambda b, q: (b * nblk + q, 0))],
        out_specs=pl.BlockSpec((QBLK, D), lambda b, q: (b * nblk + q, 0)),
        out_shape=jax.ShapeDtypeStruct((T, D), F32),
        scratch_shapes=[pltpu.VMEM((L, QBLK), I32),
                        pltpu.VMEM((nkt, KEY_TILE, QBLK), F32),
                        pltpu.VMEM((H // 2, KV_LORA + ONES_ROWS, 2 * QBLK), F32),
                        pltpu.VMEM((H * AT_DH, QBLK), F32)],
        compiler_params=_cparams(("arbitrary", "arbitrary")),
        name="at_core",
    )(qlat, qit, wit, ki, ckv, ckvt, bias, wuvt, wo, h)


def _rel_bucket(n):
    max_exact = REL_BUCKETS // 2
    nf = jnp.maximum(n, 1).astype(F32)
    large = max_exact + (jnp.log(nf / max_exact) / math.log(REL_MAX_DIST / max_exact)
                         * (REL_BUCKETS - max_exact)).astype(I32)
    large = jnp.minimum(large, REL_BUCKETS - 1)
    return jnp.where(n < max_exact, n, large)


def _bias_tiles(rel_bias):
    H = rel_bias.shape[1]
    span = KEY_TILE + QBLK - 1
    m = jnp.arange(NEAR_TILES * QBLK + span, dtype=I32)
    tab = rel_bias[_rel_bucket(jnp.maximum(m - (KEY_TILE - 1), 0))] - rel_bias[REL_BUCKETS - 1]
    tiles = []
    for d in range(NEAR_TILES):
        u = tab[d * QBLK:d * QBLK + span][::-1]
        r = jnp.tile(u, (KEY_TILE + 1, 1))[:KEY_TILE * (span + 1)].reshape(KEY_TILE, span + 1, H)
        tiles.append(r[:, :QBLK][:, ::-1])
    tiles.append(jnp.zeros_like(tiles[0]))
    t = jnp.stack(tiles).astype(F32) * LOG2E
    t = jnp.transpose(t, (0, 3, 1, 2))
    return jnp.concatenate([t[:, 0::2], t[:, 1::2]], axis=-1)


def kernel(x, p, hg_norm, hg_w_in, hg_lb, hg_onorm, hg_w_out, at_norm, at_w_in, at_q_norm,
           at_kv_norm, at_w_uq, at_w_uk, at_w_uv, at_w_qidx, at_w_out, rel_bias, ff_norm, ff_w_up,
           ff_conv_w, ff_conv_b, ff_w_down, ple_norm, ple_w_gate, ple_w_proj, final_norm):
    B, L, D = x.shape
    T = B * L
    depth = p.shape[0]
    row = lambda a: a.reshape(1, -1)
    h = x.reshape(T, D)
    for i in range(depth):
        j = i // 2
        if i % 2 == 0:
            q, k, v, gs, lf = _hg_in(h, row(hg_norm[j]), hg_lb, hg_w_in[j].astype(BF16), i)
            h = _hg_rec(q, k, v, gs, lf, h, row(hg_onorm[j]), hg_w_out[j].astype(BF16), B, L)
        else:
            w_in = at_w_in[j].astype(BF16)
            o1, o2, o3 = Q_LORA, Q_LORA + KV_LORA, Q_LORA + KV_LORA + IDX_DIM
            wuk = at_w_uk[j].astype(BF16)
            z = jnp.zeros_like(wuk[0::2])
            wuk_bd = jnp.concatenate([jnp.concatenate([wuk[0::2], z], axis=2),
                                      jnp.concatenate([z, wuk[1::2]], axis=2)], axis=1)
            wuvt = jnp.transpose(at_w_uv[j].astype(BF16), (0, 2, 1))
            qlat, qit, wit, ki, ckv, ckvt = _at_in(
                h, row(at_norm[j]), w_in[:, :o1], w_in[:, o1:o2], w_in[:, o2:o3], w_in[:, o3:].T,
                row(at_q_norm[j]), row(at_kv_norm[j]), at_w_uq[j].astype(BF16),
                at_w_qidx[j].astype(BF16).T, jnp.transpose(wuk_bd, (0, 2, 1)))
            topk = max(1, min(TOPK_MAX, L // 4))
            h = _at_core(qlat, qit, wit, ki, ckv, ckvt, _bias_tiles(rel_bias), wuvt,
                         at_w_out[j].astype(BF16), h, B, L, topk)
        h = _ffn_ple(h, row(ff_norm[i]), ff_w_up[i].astype(BF16), ff_conv_w[i], ff_conv_b[i],
                     ff_w_down[i].astype(BF16), p[i].reshape(T, PLE_DIM), row(ple_norm[i]),
                     ple_w_gate[i].astype(BF16), ple_w_proj[i].astype(BF16), row(final_norm),
                     i == depth - 1, L)
    return h.reshape(B, L, D)
```

```python
import functools
import math

import jax
import jax.numpy as jnp
import numpy as np
from jax import lax
from jax.experimental import pallas as pl
from jax.experimental.pallas import tpu as pltpu

F32 = jnp.float32
BF16 = jnp.bfloat16
I32 = jnp.int32

D_MODEL = 1024
HG_HEADS = 8
HG_DK = D_MODEL // HG_HEADS
AT_HEADS = 16
AT_DH = 64
Q_LORA = 384
KV_LORA = 256
IDX_HEADS = 8
IDX_DIM = 64
TOPK_MAX = 256
QBLK = 128
REL_BUCKETS = 32
REL_MAX_DIST = 128
D_FF = 2816
PLE_DIM = 256
EPS = 1e-6

VMEM_LIMIT_V7X = 56 * 1024 * 1024
SUB = 16
KEY_TILE = 256
NEAR_TILES = (KEY_TILE + REL_MAX_DIST) // QBLK
NEG_BIG = -1e30
LOG2E = 1.4426950408889634
ONES_ROWS = 16
INT_MIN = -(2 ** 31)


def _cparams(sem):
    return pltpu.CompilerParams(dimension_semantics=sem, vmem_limit_bytes=VMEM_LIMIT_V7X)


def _rms(x, g):
    return x * lax.rsqrt(jnp.mean(x * x, axis=-1, keepdims=True) + EPS) * g


def _dot(a, b):
    return jnp.dot(a, b, preferred_element_type=F32)


def _dot_nt(a, b):
    return lax.dot_general(a, b, (((1,), (1,)), ((), ())), preferred_element_type=F32)


def _dot_tn(a, b):
    return lax.dot_general(a, b, (((0,), (0,)), ((), ())), preferred_element_type=F32)


def _full(shape, single=False):
    n = len(shape)
    if single:
        return pl.BlockSpec(shape, lambda *_: (0,) * n, pipeline_mode=pl.Buffered(1))
    return pl.BlockSpec(shape, lambda *_: (0,) * n)


def _hg_in_kernel(layer, x_ref, g_ref, lbp_ref, w_ref, q_ref, k_ref, v_ref, gs_ref, lf_ref):
    D = D_MODEL
    xn = _rms(x_ref[...], g_ref[...]).astype(BF16)
    lbp = lbp_ref[...]
    e = jnp.exp(lbp - jnp.max(lbp, axis=0, keepdims=True))
    lb = jnp.sum(e[: layer + 1], axis=0, keepdims=True) / jnp.sum(e, axis=0, keepdims=True)
    q = _dot(xn, w_ref[:, 0:D])
    q_ref[...] = (q * jax.nn.sigmoid(q)).astype(BF16)
    f = lb + (1.0 - lb) * jax.nn.sigmoid(_dot(xn, w_ref[:, D:2 * D]))
    lf_ref[...] = jnp.log(f) * LOG2E
    k_ref[...] = (1.0 - f).astype(BF16)
    v_ref[...] = _dot(xn, w_ref[:, 2 * D:3 * D]).astype(BF16)
    g = _dot(xn, w_ref[:, 3 * D:4 * D])
    gs_ref[...] = (g * jax.nn.sigmoid(g)).astype(BF16)


def _hg_in(x, g, lbp, w, layer, tm=512):
    T, D = x.shape
    tok = pl.BlockSpec((tm, D), lambda i: (i, 0))
    bf = jax.ShapeDtypeStruct((T, D), BF16)
    return pl.pallas_call(
        functools.partial(_hg_in_kernel, layer),
        grid=(T // tm,),
        in_specs=[tok, _full((1, D)), _full(lbp.shape), _full(w.shape)],
        out_specs=[tok] * 5,
        out_shape=[bf, bf, bf, bf, jax.ShapeDtypeStruct((T, D), F32)],
        compiler_params=_cparams(("parallel",)),
        name="hg_in",
    )(x, g, lbp, w)


def _hg_rec_kernel(lt, q_ref, k_ref, v_ref, gs_ref, lf_ref, x_ref, on_ref, wo_ref, tri_ref,
                   out_ref, st_ref, b_ref, o_ref):
    H, DK = HG_HEADS, HG_DK

    @pl.when(pl.program_id(1) == 0)
    def _():
        st_ref[...] = jnp.zeros_like(st_ref)

    lf = lf_ref[...]
    hi = lf.astype(BF16)
    r1 = lf - hi.astype(F32)
    mid = r1.astype(BF16)
    lo = (r1 - mid.astype(F32)).astype(BF16)
    tri = tri_ref[...]
    b_ref[...] = _dot(tri, hi) + _dot(tri, mid) + _dot(tri, lo)

    HALF = SUB // 2
    row_h = lax.broadcasted_iota(I32, (HALF, DK), 0)

    def sub_chunk(c, carry):
        r0 = pl.multiple_of(c * SUB, SUB)
        for h in range(H):
            ls = slice(h * DK, (h + 1) * DK)
            b = b_ref[pl.ds(r0, SUB), ls]
            q = q_ref[pl.ds(r0, SUB), ls].astype(F32)
            k = k_ref[pl.ds(r0, SUB), ls].astype(F32)
            v = v_ref[pl.ds(r0, SUB), ls].astype(F32)
            be = b[SUB - 1:SUB, :]
            st = st_ref[h]
            o = _dot_nt((q * jnp.exp2(b)).astype(BF16), st.astype(BF16))
            o_top = jnp.zeros((HALF, DK), F32)
            o_bot = jnp.zeros((HALF, DK), F32)
            for s in range(SUB):
                bs, ks, vs = b[s:s + 1, :], k[s:s + 1, :], v[s:s + 1, :]
                if s < HALF:
                    e = jnp.exp2(jnp.where(row_h >= s, b[:HALF] - bs, NEG_BIG))
                    o_top = o_top + jnp.sum(q[:HALF] * e * ks, axis=-1, keepdims=True) * vs
                    e = jnp.exp2(b[HALF:] - bs)
                else:
                    e = jnp.exp2(jnp.where(row_h >= s - HALF, b[HALF:] - bs, NEG_BIG))
                o_bot = o_bot + jnp.sum(q[HALF:] * e * ks, axis=-1, keepdims=True) * vs
            o_ref[pl.ds(r0, SUB), ls] = o + jnp.concatenate([o_top, o_bot], axis=0)
            kd = (k * jnp.exp2(be - b)).astype(BF16)
            st_ref[h] = st * jnp.exp2(be) + _dot_tn(v.astype(BF16), kd)
        return carry

    lax.fori_loop(0, lt // SUB, sub_chunk, 0)

    on = on_ref[...]
    for h in range(H):
        ls = slice(h * DK, (h + 1) * DK)
        oh = _rms(o_ref[:, ls], on[:, ls]) * gs_ref[:, ls].astype(F32)
        o_ref[:, ls] = oh
    out_ref[...] = x_ref[...] + _dot(o_ref[...].astype(BF16), wo_ref[...])


def _hg_rec(q, k, v, gs, lf, x, onorm, wo, B, L, lt=256):
    T, D = x.shape
    nl = L // lt
    tok = pl.BlockSpec((lt, D), lambda b, j: (b * nl + j, 0))
    tri = (np.arange(lt)[:, None] >= np.arange(lt)[None, :]) & (
        np.arange(lt)[:, None] // SUB == np.arange(lt)[None, :] // SUB)
    tri = jnp.asarray(tri, BF16)
    return pl.pallas_call(
        functools.partial(_hg_rec_kernel, lt),
        grid=(B, nl),
        in_specs=[tok] * 6 + [_full((1, D)), _full((D, D)), _full((lt, lt))],
        out_specs=tok,
        out_shape=jax.ShapeDtypeStruct((T, D), F32),
        scratch_shapes=[pltpu.VMEM((HG_HEADS, HG_DK, HG_DK), F32),
                        pltpu.VMEM((lt, D), F32),
                        pltpu.VMEM((lt, D), F32)],
        compiler_params=_cparams(("arbitrary", "arbitrary")),
        name="hg_rec",
    )(q, k, v, gs, lf, x, onorm, wo, tri)


def _ffn_kernel(nseq, nf, final, h_ref, g_ref, wg_ref, wv_ref, cwg_ref, cwv_ref, cbg_ref, cbv_ref,
                wd_ref, p_ref, pg_ref, pwg_ref, pwp_ref, fn_ref, out_ref, xn_ref, cg_ref, cv_ref):
    i, j = pl.program_id(0), pl.program_id(1)

    @pl.when(j == 0)
    def _():
        xn_ref[...] = _rms(h_ref[...], g_ref[...]).astype(BF16)
        out_ref[...] = h_ref[...]

    seq_start = (i % nseq) == 0
    xn = xn_ref[...]

    def conv(u, w_ref, b_ref, carry_ref):
        fc = u.shape[1]
        carry = jnp.where(seq_start, 0.0, carry_ref[j])
        carry_ref[j] = u[u.shape[0] - 8:, :]
        r1 = pltpu.roll(u, 1, 0)
        r2 = pltpu.roll(u, 2, 0)
        rows = lax.broadcasted_iota(I32, (8, fc), 0)
        top1 = jnp.where(rows < 1, pltpu.roll(carry, 1, 0), r1[0:8])
        top2 = jnp.where(rows < 2, pltpu.roll(carry, 2, 0), r2[0:8])
        p1 = jnp.concatenate([top1, r1[8:]], axis=0)
        p2 = jnp.concatenate([top2, r2[8:]], axis=0)
        w = w_ref[0]
        return u * w[2:3, :] + p1 * w[1:2, :] + p2 * w[0:1, :] + b_ref[0]

    ug = conv(_dot(xn, wg_ref[...]), cwg_ref, cbg_ref, cg_ref)
    uv = conv(_dot(xn, wv_ref[...]), cwv_ref, cbv_ref, cv_ref)
    act = (ug * jax.nn.sigmoid(ug) * uv).astype(BF16)
    out_ref[...] += _dot(act, wd_ref[...])

    @pl.when(j == nf - 1)
    def _():
        h = out_ref[...]
        gate = jax.nn.sigmoid(_dot(_rms(h, pg_ref[...]).astype(BF16), pwg_ref[...]))
        h = h + gate * _dot(p_ref[...].astype(BF16), pwp_ref[...])
        if final:
            h = _rms(h, fn_ref[...])
        out_ref[...] = h


def _ffn_ple(h, g, w_up, conv_w, conv_b, w_down, p, pg, pwg, pwp, fnorm, final, L, tm=512, nf=2):
    T, D = h.shape
    F = w_down.shape[0]
    fc = F // nf
    assert fc % 128 == 0 and L % tm == 0
    cw = jnp.pad(conv_w, ((0, 5), (0, 0)))
    cwg = cw[:, :F].reshape(8, nf, fc).transpose(1, 0, 2)
    cwv = cw[:, F:].reshape(8, nf, fc).transpose(1, 0, 2)
    cbg = conv_b[:F].reshape(nf, 1, fc)
    cbv = conv_b[F:].reshape(nf, 1, fc)
    tok = pl.BlockSpec((tm, D), lambda i, j: (i, 0))
    return pl.pallas_call(
        functools.partial(_ffn_kernel, L // tm, nf, final),
        grid=(T // tm, nf),
        in_specs=[tok, _full((1, D)),
                  pl.BlockSpec((D, fc), lambda i, j: (0, j)),
                  pl.BlockSpec((D, fc), lambda i, j: (0, nf + j)),
                  pl.BlockSpec((1, 8, fc), lambda i, j: (j, 0, 0)),
                  pl.BlockSpec((1, 8, fc), lambda i, j: (j, 0, 0)),
                  pl.BlockSpec((1, 1, fc), lambda i, j: (j, 0, 0)),
                  pl.BlockSpec((1, 1, fc), lambda i, j: (j, 0, 0)),
                  pl.BlockSpec((fc, D), lambda i, j: (j, 0)),
                  pl.BlockSpec((tm, PLE_DIM), lambda i, j: (i, 0)), _full((1, D)),
                  _full((D, D), single=True), _full((PLE_DIM, D), single=True), _full((1, D))],
        out_specs=tok,
        out_shape=jax.ShapeDtypeStruct((T, D), F32),
        scratch_shapes=[pltpu.VMEM((tm, D), BF16),
                        pltpu.VMEM((nf, 8, fc), F32),
                        pltpu.VMEM((nf, 8, fc), F32)],
        compiler_params=_cparams(("arbitrary", "arbitrary")),
        name="conv_ffn_ple",
    )(h, g, w_up, w_up, cwg, cwv, cbg, cbv, w_down, p, pg, pwg, pwp, fnorm)


def _at_in_kernel(h_ref, g_ref, wcq_ref, wckv_ref, wki_ref, wwi_ref, qn_ref, kvn_ref,
                  wuq_ref, wqi_ref, wuk_ref,
                  qlat_ref, qit_ref, wit_ref, ki_ref, ckv_ref, ckvt_ref):
    tm = h_ref.shape[0]
    xn = _rms(h_ref[...], g_ref[...]).astype(BF16)
    cq = _rms(_dot(xn, wcq_ref[...]), qn_ref[...]).astype(BF16)
    ckv = _rms(_dot(xn, wckv_ref[...]), kvn_ref[...])
    ckv_ref[...] = ckv.astype(BF16)
    for c in range(tm // KEY_TILE):
        ckvt_ref[c] = jnp.concatenate([ckv[c * KEY_TILE:(c + 1) * KEY_TILE, :].T.astype(BF16),
                                       jnp.ones((ONES_ROWS, KEY_TILE), BF16)], axis=0)
    ki_ref[...] = _dot(xn, wki_ref[...]).astype(BF16)
    nq = tm // QBLK
    wit = _dot_nt(wwi_ref[...], xn) * (IDX_HEADS ** -0.5 * IDX_DIM ** -0.5)
    qit = _dot_nt(wqi_ref[...], cq).astype(BF16)
    for c in range(nq):
        wit_ref[c] = wit[:, c * QBLK:(c + 1) * QBLK]
        qit_ref[c] = qit[:, c * QBLK:(c + 1) * QBLK]
    qn = _dot(cq, wuq_ref[...]).astype(BF16)
    for pr in range(AT_HEADS // 2):
        ql = (_dot_nt(wuk_ref[pr], qn[:, pr * 128:(pr + 1) * 128])
              * (AT_DH ** -0.5 * LOG2E)).astype(BF16)
        for c in range(nq):
            qlat_ref[c, 2 * pr] = ql[:KV_LORA, c * QBLK:(c + 1) * QBLK]
            qlat_ref[c, 2 * pr + 1] = ql[KV_LORA:, c * QBLK:(c + 1) * QBLK]


def _at_in(h, g, wcq, wckv, wki, wwi_t, qn, kvn, wuq, wqi_t, wuk_bd_t, tm=512):
    T, D = h.shape
    nk = tm // KEY_TILE
    ins = [h, g, wcq, wckv, wki, wwi_t, qn, kvn, wuq, wqi_t, wuk_bd_t]
    in_specs = [pl.BlockSpec((tm, D), lambda i: (i, 0))] + [_full(a.shape) for a in ins[1:]]
    return pl.pallas_call(
        _at_in_kernel,
        grid=(T // tm,),
        in_specs=in_specs,
        out_specs=[pl.BlockSpec((tm // QBLK, AT_HEADS, KV_LORA, QBLK), lambda i: (i, 0, 0, 0)),
                   pl.BlockSpec((tm // QBLK, IDX_HEADS * IDX_DIM, QBLK), lambda i: (i, 0, 0)),
                   pl.BlockSpec((tm // QBLK, IDX_HEADS, QBLK), lambda i: (i, 0, 0)),
                   pl.BlockSpec((tm, IDX_DIM), lambda i: (i, 0)),
                   pl.BlockSpec((tm, KV_LORA), lambda i: (i, 0)),
                   pl.BlockSpec((nk, KV_LORA + ONES_ROWS, KEY_TILE), lambda i: (i, 0, 0))],
        out_shape=[jax.ShapeDtypeStruct((T // QBLK, AT_HEADS, KV_LORA, QBLK), BF16),
                   jax.ShapeDtypeStruct((T // QBLK, IDX_HEADS * IDX_DIM, QBLK), BF16),
                   jax.ShapeDtypeStruct((T // QBLK, IDX_HEADS, QBLK), F32),
                   jax.ShapeDtypeStruct((T, IDX_DIM), BF16),
                   jax.ShapeDtypeStruct((T, KV_LORA), BF16),
                   jax.ShapeDtypeStruct((T // KEY_TILE, KV_LORA + ONES_ROWS, KEY_TILE), BF16)],
        compiler_params=_cparams(("parallel",)),
        name="at_in",
    )(*ins)


def _at_core_kernel(topk, qlat_ref, qit_ref, wit_ref, ki_ref, ckv_ref, ckvt_ref, bias_ref,
                    wuvt_ref, wo_ref, h_ref, out_ref, ik_ref, planes_ref, mask_ref, acc_ref, ot_ref):
    H, TK = AT_HEADS, KEY_TILE
    assert TK == 32 * 8
    qb = pl.program_id(1)
    t0 = qb * QBLK
    nkt = qb // (TK // QBLK) + 1
    lane_q = lax.broadcasted_iota(I32, (TK, QBLK), 1)
    row_k = lax.broadcasted_iota(I32, (TK, QBLK), 0)

    wit = wit_ref[...]

    def score_tile(kt, carry):
        k0 = pl.multiple_of(kt * TK, TK)
        kk = ki_ref[pl.ds(k0, TK), :]
        acc = jnp.zeros((TK, 2 * QBLK), F32)
        for pr in range(IDX_HEADS // 2):
            qa = qit_ref[(2 * pr) * IDX_DIM:(2 * pr + 1) * IDX_DIM, :]
            qb_ = qit_ref[(2 * pr + 1) * IDX_DIM:(2 * pr + 2) * IDX_DIM, :]
            rel = jnp.maximum(_dot(kk, jnp.concatenate([qa, qb_], axis=1)), 0.0)
            w2 = jnp.concatenate([wit[2 * pr:2 * pr + 1, :], wit[2 * pr + 1:2 * pr + 2, :]], axis=1)
            acc = acc + rel * w2
        score = acc[:, :QBLK] + acc[:, QBLK:] + 0.0
        bits = pltpu.bitcast(score, I32)
        ikey = bits ^ ((bits >> 31) & 0x7FFFFFFF)
        causal = (k0 + row_k) <= (t0 + lane_q)
        xk = jnp.where(causal, ikey, INT_MIN)
        ik_ref[pl.ds(k0, TK), :] = xk
        u = (xk ^ INT_MIN).reshape(TK // 8, 8, QBLK)
        a = [u[v] for v in range(32)]
        m, j = 0x0000FFFF, 16
        while j:
            k = 0
            while k < 32:
                t = (a[k] ^ lax.shift_right_logical(a[k + j], jnp.full_like(a[k], j))) & m
                a[k] = a[k] ^ t
                a[k + j] = a[k + j] ^ (t << j)
                k = (k + j + 1) & ~j
            j >>= 1
            m = m ^ ((m << j) & 0xFFFFFFFF)
        for i in range(32):
            planes_ref[i, pl.ds(kt, 1)] = a[i][None]
        return carry

    @pl.when((pl.program_id(0) == 0) & (qb == 0))
    def _():
        planes_ref[...] = jnp.zeros_like(planes_ref)

    lax.fori_loop(0, nkt, score_tile, 0)

    CT = 2 * TK
    nct = (nkt + 1) // 2

    @pl.when(nkt % 2 == 1)
    def _():
        ik_ref[pl.ds(pl.multiple_of(nkt * TK, TK), TK), :] = jnp.full((TK, QBLK), INT_MIN, I32)

    row_c = lax.broadcasted_iota(I32, (CT, QBLK), 0)

    def count(pred):
        def body(ct, c):
            k0 = pl.multiple_of(ct * CT, CT)
            hit = jnp.where(pred(ik_ref[pl.ds(k0, CT), :], k0 + row_c), 1, 0)
            return c + jnp.sum(hit.reshape(CT // 8, 8, QBLK), axis=0)
        c = lax.fori_loop(0, nct, body, jnp.zeros((8, QBLK), I32))
        return jnp.sum(c, axis=0, keepdims=True)

    NT = planes_ref.shape[1]

    def bit_pass(i, st):
        alive, rem, tau_u = st
        t = alive & planes_ref[i]
        c = jnp.sum(jnp.sum(lax.population_count(t), axis=0), axis=0, keepdims=True)
        take = c >= rem
        return (jnp.where(take, t, alive ^ t), jnp.where(take, rem, rem - c),
                jnp.where(take, tau_u | (1 << (31 - i)), tau_u))

    alive0 = jnp.where(lax.broadcasted_iota(I32, (NT, 8, QBLK), 0) < nkt, -1, 0)
    _, _, tau_u = lax.fori_loop(0, 32, bit_pass, (alive0, jnp.full((1, QBLK), topk, I32),
                                                  jnp.zeros((1, QBLK), I32)))
    tau = tau_u ^ INT_MIN
    n_gt = count(lambda x, k0: x > tau)
    n_ge = count(lambda x, k0: x >= tau)
    need = topk - n_gt

    def tie_search(_):
        def step(i, jc):
            cand = jc + (1 << (12 - i))
            n = count(lambda x, kidx: (x == tau) & (kidx < cand))
            return jnp.where(n <= need, cand, jc)
        return lax.fori_loop(0, 13, step, jnp.zeros((1, QBLK), I32))

    jcut = lax.cond(jnp.max(n_ge) > topk, tie_search,
                    lambda _: jnp.full((1, QBLK), 2 ** 13, I32), 0)

    def mask_tile(kt, carry):
        k0 = pl.multiple_of(kt * TK, TK)
        x = ik_ref[pl.ds(k0, TK), :]
        kidx = k0 + row_k
        sel = ((x > tau) | ((x == tau) & (kidx < jcut))) & (kidx <= (t0 + lane_q))
        mask_ref[kt] = jnp.where(sel, 0.0, NEG_BIG)
        return carry

    lax.fori_loop(0, nkt, mask_tile, 0)

    nfar = jnp.maximum(nkt - 2, 0)

    NP = H // 2
    acc_ref[...] = jnp.zeros_like(acc_ref)

    def tile(near, kt, m_all):
        k0 = pl.multiple_of(kt * TK, TK)
        am = mask_ref[kt]
        am2 = jnp.concatenate([am, am], axis=1)
        ck = ckv_ref[pl.ds(k0, TK), :]
        ckt = ckvt_ref[kt]
        ms = []
        for pr in range(NP):
            qt = jnp.concatenate([qlat_ref[2 * pr], qlat_ref[2 * pr + 1]], axis=1)
            lg = _dot(ck, qt) + am2
            if near:
                lg = lg + bias_ref[jnp.minimum((t0 - k0) // QBLK, NEAR_TILES), pr]
            m_old = m_all[pr:pr + 1, :]
            m_new = jnp.maximum(m_old, jnp.max(lg, axis=0, keepdims=True))
            p = jnp.exp2((lg - m_new).astype(BF16))
            acc_ref[pr] = acc_ref[pr] * jnp.exp2(m_old - m_new) + _dot(ckt, p)
            ms.append(m_new)
        return jnp.concatenate(ms, axis=0)

    m_all = jnp.full((NP, 2 * QBLK), NEG_BIG, F32)
    m_all = lax.fori_loop(0, nfar, functools.partial(tile, False), m_all)
    m_all = lax.fori_loop(nfar, nkt, functools.partial(tile, True), m_all)
    for pr in range(NP):
        a = acc_ref[pr]
        olat = (a[:KV_LORA] / a[KV_LORA:KV_LORA + 1]).astype(BF16)
        ot_ref[pr * 2 * AT_DH:(pr + 1) * 2 * AT_DH, :] = jnp.concatenate(
            [_dot(wuvt_ref[2 * pr], olat[:, :QBLK]), _dot(wuvt_ref[2 * pr + 1], olat[:, QBLK:])], axis=0)
    out_ref[...] = h_ref[...] + _dot(ot_ref[...].T.astype(BF16), wo_ref[...])


def _at_core(qlat, qit, wit, ki, ckv, ckvt, bias, wuvt, wo, h, B, L, topk):
    T, D = h.shape
    nblk = L // QBLK
    nkt = L // KEY_TILE
    H = AT_HEADS
    assert L % (2 * KEY_TILE) == 0
    return pl.pallas_call(
        functools.partial(_at_core_kernel, topk),
        grid=(B, nblk),
        in_specs=[pl.BlockSpec((None, H, KV_LORA, QBLK), lambda b, q: (b * nblk + q, 0, 0, 0)),
                  pl.BlockSpec((None, IDX_HEADS * IDX_DIM, QBLK), lambda b, q: (b * nblk + q, 0, 0)),
                  pl.BlockSpec((None, IDX_HEADS, QBLK), lambda b, q: (b * nblk + q, 0, 0)),
                  pl.BlockSpec((L, IDX_DIM), lambda b, q: (b, 0)),
                  pl.BlockSpec((L, KV_LORA), lambda b, q: (b, 0)),
                  pl.BlockSpec((nkt, KV_LORA + ONES_ROWS, KEY_TILE), lambda b, q: (b, 0, 0)),
                  _full(bias.shape, single=True), _full(wuvt.shape, single=True),
                  _full(wo.shape, single=True),
                  pl.BlockSpec((QBLK, D), lambda b, q: (b * nblk + q, 0))],
        out_specs=pl.BlockSpec((QBLK, D), lambda b, q: (b * nblk + q, 0)),
        out_shape=jax.ShapeDtypeStruct((T, D), F32),
        scratch_shapes=[pltpu.VMEM((L, QBLK), I32),
                        pltpu.VMEM((32, nkt, 8, QBLK), I32),
                        pltpu.VMEM((nkt, KEY_TILE, QBLK), F32),
                        pltpu.VMEM((H // 2, KV_LORA + ONES_ROWS, 2 * QBLK), F32),
                        pltpu.VMEM((H * AT_DH, QBLK), F32)],
        compiler_params=_cparams(("arbitrary", "arbitrary")),
        name="at_core",
    )(qlat, qit, wit, ki, ckv, ckvt, bias, wuvt, wo, h)


def _rel_bucket(n):
    max_exact = REL_BUCKETS // 2
    nf = jnp.maximum(n, 1).astype(F32)
    large = max_exact + (jnp.log(nf / max_exact) / math.log(REL_MAX_DIST / max_exact)
                         * (REL_BUCKETS - max_exact)).astype(I32)
    large = jnp.minimum(large, REL_BUCKETS - 1)
    return jnp.where(n < max_exact, n, large)


def _bias_tiles(rel_bias):
    H = rel_bias.shape[1]
    span = KEY_TILE + QBLK - 1
    m = jnp.arange(NEAR_TILES * QBLK + span, dtype=I32)
    tab = rel_bias[_rel_bucket(jnp.maximum(m - (KEY_TILE - 1), 0))] - rel_bias[REL_BUCKETS - 1]
    tiles = []
    for d in range(NEAR_TILES):
        u = tab[d * QBLK:d * QBLK + span][::-1]
        r = jnp.tile(u, (KEY_TILE + 1, 1))[:KEY_TILE * (span + 1)].reshape(KEY_TILE, span + 1, H)
        tiles.append(r[:, :QBLK][:, ::-1])
    tiles.append(jnp.zeros_like(tiles[0]))
    t = jnp.stack(tiles).astype(F32) * LOG2E
    t = jnp.transpose(t, (0, 3, 1, 2))
    return jnp.concatenate([t[:, 0::2], t[:, 1::2]], axis=-1)


def kernel(x, p, hg_norm, hg_w_in, hg_lb, hg_onorm, hg_w_out, at_norm, at_w_in, at_q_norm,
           at_kv_norm, at_w_uq, at_w_uk, at_w_uv, at_w_qidx, at_w_out, rel_bias, ff_norm, ff_w_up,
           ff_conv_w, ff_conv_b, ff_w_down, ple_norm, ple_w_gate, ple_w_proj, final_norm):
    B, L, D = x.shape
    T = B * L
    depth = p.shape[0]
    row = lambda a: a.reshape(1, -1)
    h = x.reshape(T, D)
    for i in range(depth):
        j = i // 2
        if i % 2 == 0:
            q, k, v, gs, lf = _hg_in(h, row(hg_norm[j]), hg_lb, hg_w_in[j].astype(BF16), i)
            h = _hg_rec(q, k, v, gs, lf, h, row(hg_onorm[j]), hg_w_out[j].astype(BF16), B, L)
        else:
            w_in = at_w_in[j].astype(BF16)
            o1, o2, o3 = Q_LORA, Q_LORA + KV_LORA, Q_LORA + KV_LORA + IDX_DIM
            wuk = at_w_uk[j].astype(BF16)
            z = jnp.zeros_like(wuk[0::2])
            wuk_bd = jnp.concatenate([jnp.concatenate([wuk[0::2], z], axis=2),
                                      jnp.concatenate([z, wuk[1::2]], axis=2)], axis=1)
            wuvt = jnp.transpose(at_w_uv[j].astype(BF16), (0, 2, 1))
            qlat, qit, wit, ki, ckv, ckvt = _at_in(
                h, row(at_norm[j]), w_in[:, :o1], w_in[:, o1:o2], w_in[:, o2:o3], w_in[:, o3:].T,
                row(at_q_norm[j]), row(at_kv_norm[j]), at_w_uq[j].astype(BF16),
                at_w_qidx[j].astype(BF16).T, jnp.transpose(wuk_bd, (0, 2, 1)))
            topk = max(1, min(TOPK_MAX, L // 4))
            h = _at_core(qlat, qit, wit, ki, ckv, ckvt, _bias_tiles(rel_bias), wuvt,
                         at_w_out[j].astype(BF16), h, B, L, topk)
        h = _ffn_ple(h, row(ff_norm[i]), ff_w_up[i].astype(BF16), ff_conv_w[i], ff_conv_b[i],
                     ff_w_down[i].astype(BF16), p[i].reshape(T, PLE_DIM), row(ple_norm[i]),
                     ple_w_gate[i].astype(BF16), ple_w_proj[i].astype(BF16), row(final_norm),
                     i == depth - 1, L)
    return h.reshape(B, L, D)
```

```python
import functools
import math

import jax
import jax.numpy as jnp
import numpy as np
from jax import lax
from jax.experimental import pallas as pl
from jax.experimental.pallas import tpu as pltpu

F32 = jnp.float32
BF16 = jnp.bfloat16
I32 = jnp.int32

D_MODEL = 1024
HG_HEADS = 8
HG_DK = D_MODEL // HG_HEADS
AT_HEADS = 16
AT_DH = 64
Q_LORA = 384
KV_LORA = 256
IDX_HEADS = 8
IDX_DIM = 64
TOPK_MAX = 256
QBLK = 128
REL_BUCKETS = 32
REL_MAX_DIST = 128
D_FF = 2816
PLE_DIM = 256
EPS = 1e-6

VMEM_LIMIT_V7X = 56 * 1024 * 1024
SUB = 16
KEY_TILE = 256
NEAR_TILES = (KEY_TILE + REL_MAX_DIST) // QBLK
NEG_BIG = -1e30
LOG2E = 1.4426950408889634
ONES_ROWS = 16
INT_MIN = -(2 ** 31)


def _cparams(sem):
    return pltpu.CompilerParams(dimension_semantics=sem, vmem_limit_bytes=VMEM_LIMIT_V7X)


def _rms(x, g):
    return x * lax.rsqrt(jnp.mean(x * x, axis=-1, keepdims=True) + EPS) * g


def _dot(a, b):
    return jnp.dot(a, b, preferred_element_type=F32)


def _dot_nt(a, b):
    return lax.dot_general(a, b, (((1,), (1,)), ((), ())), preferred_element_type=F32)


def _dot_tn(a, b):
    return lax.dot_general(a, b, (((0,), (0,)), ((), ())), preferred_element_type=F32)


def _full(shape, single=False):
    n = len(shape)
    if single:
        return pl.BlockSpec(shape, lambda *_: (0,) * n, pipeline_mode=pl.Buffered(1))
    return pl.BlockSpec(shape, lambda *_: (0,) * n)


def _hg_in_kernel(layer, x_ref, g_ref, lbp_ref, w_ref, q_ref, k_ref, v_ref, gs_ref, lf_ref):
    D = D_MODEL
    xn = _rms(x_ref[...], g_ref[...]).astype(BF16)
    lbp = lbp_ref[...]
    e = jnp.exp(lbp - jnp.max(lbp, axis=0, keepdims=True))
    lb = jnp.sum(e[: layer + 1], axis=0, keepdims=True) / jnp.sum(e, axis=0, keepdims=True)
    q = _dot(xn, w_ref[:, 0:D])
    q_ref[...] = (q * jax.nn.sigmoid(q)).astype(BF16)
    f = lb + (1.0 - lb) * jax.nn.sigmoid(_dot(xn, w_ref[:, D:2 * D]))
    lf_ref[...] = jnp.log(f) * LOG2E
    k_ref[...] = (1.0 - f).astype(BF16)
    v_ref[...] = _dot(xn, w_ref[:, 2 * D:3 * D]).astype(BF16)
    g = _dot(xn, w_ref[:, 3 * D:4 * D])
    gs_ref[...] = (g * jax.nn.sigmoid(g)).astype(BF16)


def _hg_in(x, g, lbp, w, layer, tm=512):
    T, D = x.shape
    tok = pl.BlockSpec((tm, D), lambda i: (i, 0))
    bf = jax.ShapeDtypeStruct((T, D), BF16)
    return pl.pallas_call(
        functools.partial(_hg_in_kernel, layer),
        grid=(T // tm,),
        in_specs=[tok, _full((1, D)), _full(lbp.shape), _full(w.shape)],
        out_specs=[tok] * 5,
        out_shape=[bf, bf, bf, bf, jax.ShapeDtypeStruct((T, D), F32)],
        compiler_params=_cparams(("parallel",)),
        name="hg_in",
    )(x, g, lbp, w)


def _hg_rec_kernel(lt, q_ref, k_ref, v_ref, gs_ref, lf_ref, x_ref, on_ref, wo_ref, tri_ref,
                   out_ref, st_ref, b_ref, o_ref):
    H, DK = HG_HEADS, HG_DK

    @pl.when(pl.program_id(1) == 0)
    def _():
        st_ref[...] = jnp.zeros_like(st_ref)

    lf = lf_ref[...]
    hi = lf.astype(BF16)
    r1 = lf - hi.astype(F32)
    mid = r1.astype(BF16)
    lo = (r1 - mid.astype(F32)).astype(BF16)
    tri = tri_ref[...]
    b_ref[...] = _dot(tri, hi) + _dot(tri, mid) + _dot(tri, lo)

    HALF = SUB // 2
    row_h = lax.broadcasted_iota(I32, (HALF, DK), 0)

    def sub_chunk(c, carry):
        r0 = pl.multiple_of(c * SUB, SUB)
        for h in range(H):
            ls = slice(h * DK, (h + 1) * DK)
            b = b_ref[pl.ds(r0, SUB), ls]
            q = q_ref[pl.ds(r0, SUB), ls].astype(F32)
            k = k_ref[pl.ds(r0, SUB), ls].astype(F32)
            v = v_ref[pl.ds(r0, SUB), ls].astype(F32)
            be = b[SUB - 1:SUB, :]
            st = st_ref[h]
            o = _dot_nt((q * jnp.exp2(b)).astype(BF16), st.astype(BF16))
            o_top = jnp.zeros((HALF, DK), F32)
            o_bot = jnp.zeros((HALF, DK), F32)
            for s in range(SUB):
                bs, ks, vs = b[s:s + 1, :], k[s:s + 1, :], v[s:s + 1, :]
                if s < HALF:
                    e = jnp.exp2(jnp.where(row_h >= s, b[:HALF] - bs, NEG_BIG))
                    o_top = o_top + jnp.sum(q[:HALF] * e * ks, axis=-1, keepdims=True) * vs
                    e = jnp.exp2(b[HALF:] - bs)
                else:
                    e = jnp.exp2(jnp.where(row_h >= s - HALF, b[HALF:] - bs, NEG_BIG))
                o_bot = o_bot + jnp.sum(q[HALF:] * e * ks, axis=-1, keepdims=True) * vs
            o_ref[pl.ds(r0, SUB), ls] = o + jnp.concatenate([o_top, o_bot], axis=0)
            kd = (k * jnp.exp2(be - b)).astype(BF16)
            st_ref[h] = st * jnp.exp2(be) + _dot_tn(v.astype(BF16), kd)
        return carry

    lax.fori_loop(0, lt // SUB, sub_chunk, 0)

    on = on_ref[...]
    for h in range(H):
        ls = slice(h * DK, (h + 1) * DK)
        oh = _rms(o_ref[:, ls], on[:, ls]) * gs_ref[:, ls].astype(F32)
        o_ref[:, ls] = oh
    out_ref[...] = x_ref[...] + _dot(o_ref[...].astype(BF16), wo_ref[...])


def _hg_rec(q, k, v, gs, lf, x, onorm, wo, B, L, lt=256):
    T, D = x.shape
    nl = L // lt
    tok = pl.BlockSpec((lt, D), lambda b, j: (b * nl + j, 0))
    tri = (np.arange(lt)[:, None] >= np.arange(lt)[None, :]) & (
        np.arange(lt)[:, None] // SUB == np.arange(lt)[None, :] // SUB)
    tri = jnp.asarray(tri, BF16)
    return pl.pallas_call(
        functools.partial(_hg_rec_kernel, lt),
        grid=(B, nl),
        in_specs=[tok] * 6 + [_full((1, D)), _full((D, D)), _full((lt, lt))],
        out_specs=tok,
        out_shape=jax.ShapeDtypeStruct((T, D), F32),
        scratch_shapes=[pltpu.VMEM((HG_HEADS, HG_DK, HG_DK), F32),
                        pltpu.VMEM((lt, D), F32),
                        pltpu.VMEM((lt, D), F32)],
        compiler_params=_cparams(("arbitrary", "arbitrary")),
        name="hg_rec",
    )(q, k, v, gs, lf, x, onorm, wo, tri)


def _ffn_kernel(nseq, nf, final, h_ref, g_ref, wg_ref, wv_ref, cwg_ref, cwv_ref, cbg_ref, cbv_ref,
                wd_ref, p_ref, pg_ref, pwg_ref, pwp_ref, fn_ref, out_ref, xn_ref, cg_ref, cv_ref):
    i, j = pl.program_id(0), pl.program_id(1)

    @pl.when(j == 0)
    def _():
        xn_ref[...] = _rms(h_ref[...], g_ref[...]).astype(BF16)
        out_ref[...] = h_ref[...]

    seq_start = (i % nseq) == 0
    xn = xn_ref[...]

    def conv(u, w_ref, b_ref, carry_ref):
        fc = u.shape[1]
        carry = jnp.where(seq_start, 0.0, carry_ref[j])
        carry_ref[j] = u[u.shape[0] - 8:, :]
        r1 = pltpu.roll(u, 1, 0)
        r2 = pltpu.roll(u, 2, 0)
        rows = lax.broadcasted_iota(I32, (8, fc), 0)
        top1 = jnp.where(rows < 1, pltpu.roll(carry, 1, 0), r1[0:8])
        top2 = jnp.where(rows < 2, pltpu.roll(carry, 2, 0), r2[0:8])
        p1 = jnp.concatenate([top1, r1[8:]], axis=0)
        p2 = jnp.concatenate([top2, r2[8:]], axis=0)
        w = w_ref[0]
        return u * w[2:3, :] + p1 * w[1:2, :] + p2 * w[0:1, :] + b_ref[0]

    ug = conv(_dot(xn, wg_ref[...]), cwg_ref, cbg_ref, cg_ref)
    uv = conv(_dot(xn, wv_ref[...]), cwv_ref, cbv_ref, cv_ref)
    act = (ug * jax.nn.sigmoid(ug) * uv).astype(BF16)
    out_ref[...] += _dot(act, wd_ref[...])

    @pl.when(j == nf - 1)
    def _():
        h = out_ref[...]
        gate = jax.nn.sigmoid(_dot(_rms(h, pg_ref[...]).astype(BF16), pwg_ref[...]))
        h = h + gate * _dot(p_ref[...].astype(BF16), pwp_ref[...])
        if final:
            h = _rms(h, fn_ref[...])
        out_ref[...] = h


def _ffn_ple(h, g, w_up, conv_w, conv_b, w_down, p, pg, pwg, pwp, fnorm, final, L, tm=512, nf=2):
    T, D = h.shape
    F = w_down.shape[0]
    fc = F // nf
    assert fc % 128 == 0 and L % tm == 0
    cw = jnp.pad(conv_w, ((0, 5), (0, 0)))
    cwg = cw[:, :F].reshape(8, nf, fc).transpose(1, 0, 2)
    cwv = cw[:, F:].reshape(8, nf, fc).transpose(1, 0, 2)
    cbg = conv_b[:F].reshape(nf, 1, fc)
    cbv = conv_b[F:].reshape(nf, 1, fc)
    tok = pl.BlockSpec((tm, D), lambda i, j: (i, 0))
    return pl.pallas_call(
        functools.partial(_ffn_kernel, L // tm, nf, final),
        grid=(T // tm, nf),
        in_specs=[tok, _full((1, D)),
                  pl.BlockSpec((D, fc), lambda i, j: (0, j)),
                  pl.BlockSpec((D, fc), lambda i, j: (0, nf + j)),
                  pl.BlockSpec((1, 8, fc), lambda i, j: (j, 0, 0)),
                  pl.BlockSpec((1, 8, fc), lambda i, j: (j, 0, 0)),
                  pl.BlockSpec((1, 1, fc), lambda i, j: (j, 0, 0)),
                  pl.BlockSpec((1, 1, fc), lambda i, j: (j, 0, 0)),
                  pl.BlockSpec((fc, D), lambda i, j: (j, 0)),
                  pl.BlockSpec((tm, PLE_DIM), lambda i, j: (i, 0)), _full((1, D)),
                  _full((D, D), single=True), _full((PLE_DIM, D), single=True), _full((1, D))],
        out_specs=tok,
        out_shape=jax.ShapeDtypeStruct((T, D), F32),
        scratch_shapes=[pltpu.VMEM((tm, D), BF16),
                        pltpu.VMEM((nf, 8, fc), F32),
                        pltpu.VMEM((nf, 8, fc), F32)],
        compiler_params=_cparams(("arbitrary", "arbitrary")),
        name="conv_ffn_ple",
    )(h, g, w_up, w_up, cwg, cwv, cbg, cbv, w_down, p, pg, pwg, pwp, fnorm)


def _at_in_kernel(h_ref, g_ref, wcq_ref, wckv_ref, wki_ref, wwi_ref, qn_ref, kvn_ref,
                  wuq_ref, wqi_ref, wuk_ref,
                  qlat_ref, qit_ref, wit_ref, ki_ref, ckv_ref, ckvt_ref):
    tm = h_ref.shape[0]
    xn = _rms(h_ref[...], g_ref[...]).astype(BF16)
    cq = _rms(_dot(xn, wcq_ref[...]), qn_ref[...]).astype(BF16)
    ckv = _rms(_dot(xn, wckv_ref[...]), kvn_ref[...])
    ckv_ref[...] = ckv.astype(BF16)
    for c in range(tm // KEY_TILE):
        ckvt_ref[c] = jnp.concatenate([ckv[c * KEY_TILE:(c + 1) * KEY_TILE, :].T.astype(BF16),
                                       jnp.ones((ONES_ROWS, KEY_TILE), BF16)], axis=0)
    ki_ref[...] = _dot(xn, wki_ref[...]).astype(BF16)
    nq = tm // QBLK
    wit = _dot_nt(wwi_ref[...], xn) * (IDX_HEADS ** -0.5 * IDX_DIM ** -0.5)
    qit = _dot_nt(wqi_ref[...], cq).astype(BF16)
    for c in range(nq):
        wit_ref[c] = wit[:, c * QBLK:(c + 1) * QBLK]
        qit_ref[c] = qit[:, c * QBLK:(c + 1) * QBLK]
    qn = _dot(cq, wuq_ref[...]).astype(BF16)
    for pr in range(AT_HEADS // 2):
        ql = (_dot_nt(wuk_ref[pr], qn[:, pr * 128:(pr + 1) * 128])
              * (AT_DH ** -0.5 * LOG2E)).astype(BF16)
        for c in range(nq):
            qlat_ref[c, 2 * pr] = ql[:KV_LORA, c * QBLK:(c + 1) * QBLK]
            qlat_ref[c, 2 * pr + 1] = ql[KV_LORA:, c * QBLK:(c + 1) * QBLK]


def _at_in(h, g, wcq, wckv, wki, wwi_t, qn, kvn, wuq, wqi_t, wuk_bd_t, tm=512):
    T, D = h.shape
    nk = tm // KEY_TILE
    ins = [h, g, wcq, wckv, wki, wwi_t, qn, kvn, wuq, wqi_t, wuk_bd_t]
    in_specs = [pl.BlockSpec((tm, D), lambda i: (i, 0))] + [_full(a.shape) for a in ins[1:]]
    return pl.pallas_call(
        _at_in_kernel,
        grid=(T // tm,),
        in_specs=in_specs,
        out_specs=[pl.BlockSpec((tm // QBLK, AT_HEADS, KV_LORA, QBLK), lambda i: (i, 0, 0, 0)),
                   pl.BlockSpec((tm // QBLK, IDX_HEADS * IDX_DIM, QBLK), lambda i: (i, 0, 0)),
                   pl.BlockSpec((tm // QBLK, IDX_HEADS, QBLK), lambda i: (i, 0, 0)),
                   pl.BlockSpec((tm, IDX_DIM), lambda i: (i, 0)),
                   pl.BlockSpec((tm, KV_LORA), lambda i: (i, 0)),
                   pl.BlockSpec((nk, KV_LORA + ONES_ROWS, KEY_TILE), lambda i: (i, 0, 0))],
        out_shape=[jax.ShapeDtypeStruct((T // QBLK, AT_HEADS, KV_LORA, QBLK), BF16),
                   jax.ShapeDtypeStruct((T // QBLK, IDX_HEADS * IDX_DIM, QBLK), BF16),
                   jax.ShapeDtypeStruct((T // QBLK, IDX_HEADS, QBLK), F32),
                   jax.ShapeDtypeStruct((T, IDX_DIM), BF16),
                   jax.ShapeDtypeStruct((T, KV_LORA), BF16),
                   jax.ShapeDtypeStruct((T // KEY_TILE, KV_LORA + ONES_ROWS, KEY_TILE), BF16)],
        compiler_params=_cparams(("parallel",)),
        name="at_in",
    )(*ins)


def _at_core_kernel(topk, qlat_ref, qit_ref, wit_ref, ki_ref, ckv_ref, ckvt_ref, bias_ref,
                    wuvt_ref, wo_ref, h_ref, out_ref, ik_ref, planes_ref, mask_ref, acc_ref, ot_ref):
    H, TK = AT_HEADS, KEY_TILE
    assert TK == 32 * 8
    qb = pl.program_id(1)
    t0 = qb * QBLK
    nkt = qb // (TK // QBLK) + 1
    lane_q = lax.broadcasted_iota(I32, (TK, QBLK), 1)
    row_k = lax.broadcasted_iota(I32, (TK, QBLK), 0)

    wit = wit_ref[...]

    def score_tile(kt, carry):
        k0 = pl.multiple_of(kt * TK, TK)
        kk = ki_ref[pl.ds(k0, TK), :]
        acc = jnp.zeros((TK, 2 * QBLK), F32)
        for pr in range(IDX_HEADS // 2):
            qa = qit_ref[(2 * pr) * IDX_DIM:(2 * pr + 1) * IDX_DIM, :]
            qb_ = qit_ref[(2 * pr + 1) * IDX_DIM:(2 * pr + 2) * IDX_DIM, :]
            rel = jnp.maximum(_dot(kk, jnp.concatenate([qa, qb_], axis=1)), 0.0)
            w2 = jnp.concatenate([wit[2 * pr:2 * pr + 1, :], wit[2 * pr + 1:2 * pr + 2, :]], axis=1)
            acc = acc + rel * w2
        score = acc[:, :QBLK] + acc[:, QBLK:] + 0.0
        bits = pltpu.bitcast(score, I32)
        ikey = bits ^ ((bits >> 31) & 0x7FFFFFFF)
        causal = (k0 + row_k) <= (t0 + lane_q)
        xk = jnp.where(causal, ikey, INT_MIN)
        ik_ref[pl.ds(k0, TK), :] = xk
        u = (xk ^ INT_MIN).reshape(TK // 8, 8, QBLK)
        a = [u[v] for v in range(32)]
        m, j = 0x0000FFFF, 16
        while j:
            k = 0
            while k < 32:
                t = (a[k] ^ lax.shift_right_logical(a[k + j], jnp.full_like(a[k], j))) & m
                a[k] = a[k] ^ t
                a[k + j] = a[k + j] ^ (t << j)
                k = (k + j + 1) & ~j
            j >>= 1
            m = m ^ ((m << j) & 0xFFFFFFFF)
        for i in range(32):
            planes_ref[i, pl.ds(kt, 1)] = a[i][None]
        return carry

    @pl.when((pl.program_id(0) == 0) & (qb == 0))
    def _():
        planes_ref[...] = jnp.zeros_like(planes_ref)

    CT = 2 * TK
    nct = (nkt + 1) // 2
    lax.fori_loop(0, nct, lambda c, x: score_tile(2 * c + 1, score_tile(2 * c, x)), 0)

    row_c = lax.broadcasted_iota(I32, (CT, QBLK), 0)

    def count(pred):
        def body(ct, c):
            k0 = pl.multiple_of(ct * CT, CT)
            hit = jnp.where(pred(ik_ref[pl.ds(k0, CT), :], k0 + row_c), 1, 0)
            return c + jnp.sum(hit.reshape(CT // 8, 8, QBLK), axis=0)
        c = lax.fori_loop(0, nct, body, jnp.zeros((8, QBLK), I32))
        return jnp.sum(c, axis=0, keepdims=True)

    NT = planes_ref.shape[1]

    def bit_pass(i, st):
        alive, rem, tau_u = st
        t = alive & planes_ref[i]
        c = jnp.sum(jnp.sum(lax.population_count(t), axis=0), axis=0, keepdims=True)
        take = c >= rem
        return (jnp.where(take, t, alive ^ t), jnp.where(take, rem, rem - c),
                jnp.where(take, tau_u | (1 << (31 - i)), tau_u))

    alive0 = jnp.where(lax.broadcasted_iota(I32, (NT, 8, QBLK), 0) < nkt, -1, 0)
    ties, need, tau_u = lax.fori_loop(0, 32, bit_pass, (alive0, jnp.full((1, QBLK), topk, I32),
                                                        jnp.zeros((1, QBLK), I32)))
    tau = tau_u ^ INT_MIN
    n_ties = jnp.sum(jnp.sum(lax.population_count(ties), axis=0), axis=0, keepdims=True)

    def tie_search(_):
        def step(i, jc):
            cand = jc + (1 << (12 - i))
            n = count(lambda x, kidx: (x == tau) & (kidx < cand))
            return jnp.where(n <= need, cand, jc)
        return lax.fori_loop(0, 13, step, jnp.zeros((1, QBLK), I32))

    jcut = lax.cond(jnp.max(n_ties - need) > 0, tie_search,
                    lambda _: jnp.full((1, QBLK), 2 ** 13, I32), 0)

    def mask_tile(kt, carry):
        k0 = pl.multiple_of(kt * TK, TK)
        x = ik_ref[pl.ds(k0, TK), :]
        kidx = k0 + row_k
        sel = ((x > tau) | ((x == tau) & (kidx < jcut))) & (kidx <= (t0 + lane_q))
        mask_ref[kt] = jnp.where(sel, 0.0, NEG_BIG)
        return carry

    lax.fori_loop(0, nkt, mask_tile, 0)

    nfar = jnp.maximum(nkt - 2, 0)

    NP = H // 2
    acc_ref[...] = jnp.zeros_like(acc_ref)

    def tile(near, kt, m_all):
        k0 = pl.multiple_of(kt * TK, TK)
        am = mask_ref[kt]
        am2 = jnp.concatenate([am, am], axis=1)
        ck = ckv_ref[pl.ds(k0, TK), :]
        ckt = ckvt_ref[kt]
        ms = []
        for pr in range(NP):
            qt = jnp.concatenate([qlat_ref[2 * pr], qlat_ref[2 * pr + 1]], axis=1)
            lg = _dot(ck, qt) + am2
            if near:
                lg = lg + bias_ref[jnp.minimum((t0 - k0) // QBLK, NEAR_TILES), pr]
            m_old = m_all[pr:pr + 1, :]
            m_new = jnp.maximum(m_old, jnp.max(lg, axis=0, keepdims=True))
            p = jnp.exp2((lg - m_new).astype(BF16))
            acc_ref[pr] = acc_ref[pr] * jnp.exp2(m_old - m_new) + _dot(ckt, p)
            ms.append(m_new)
        return jnp.concatenate(ms, axis=0)

    m_all = jnp.full((NP, 2 * QBLK), NEG_BIG, F32)
    m_all = lax.fori_loop(0, nfar, functools.partial(tile, False), m_all)
    m_all = lax.fori_loop(nfar, nkt, functools.partial(tile, True), m_all)
    for pr in range(NP):
        a = acc_ref[pr]
        olat = (a[:KV_LORA] / a[KV_LORA:KV_LORA + 1]).astype(BF16)
        ot_ref[pr * 2 * AT_DH:(pr + 1) * 2 * AT_DH, :] = jnp.concatenate(
            [_dot(wuvt_ref[2 * pr], olat[:, :QBLK]), _dot(wuvt_ref[2 * pr + 1], olat[:, QBLK:])], axis=0)
    out_ref[...] = h_ref[...] + _dot(ot_ref[...].T.astype(BF16), wo_ref[...])


def _at_core(qlat, qit, wit, ki, ckv, ckvt, bias, wuvt, wo, h, B, L, topk):
    T, D = h.shape
    nblk = L // QBLK
    nkt = L // KEY_TILE
    H = AT_HEADS
    assert L % (2 * KEY_TILE) == 0
    return pl.pallas_call(
        functools.partial(_at_core_kernel, topk),
        grid=(B, nblk),
        in_specs=[pl.BlockSpec((None, H, KV_LORA, QBLK), lambda b, q: (b * nblk + q, 0, 0, 0)),
                  pl.BlockSpec((None, IDX_HEADS * IDX_DIM, QBLK), lambda b, q: (b * nblk + q, 0, 0)),
                  pl.BlockSpec((None, IDX_HEADS, QBLK), lambda b, q: (b * nblk + q, 0, 0)),
                  pl.BlockSpec((L, IDX_DIM), lambda b, q: (b, 0)),
                  pl.BlockSpec((L, KV_LORA), lambda b, q: (b, 0)),
                  pl.BlockSpec((nkt, KV_LORA + ONES_ROWS, KEY_TILE), lambda b, q: (b, 0, 0)),
                  _full(bias.shape, single=True), _full(wuvt.shape, single=True),
                  _full(wo.shape, single=True),
                  pl.BlockSpec((QBLK, D), lambda b, q: (b * nblk + q, 0))],
        out_specs=pl.BlockSpec((QBLK, D), lambda b, q: (b * nblk + q, 0)),
        out_shape=jax.ShapeDtypeStruct((T, D), F32),
        scratch_shapes=[pltpu.VMEM((L, QBLK), I32),
                        pltpu.VMEM((32, nkt, 8, QBLK), I32),
                        pltpu.VMEM((nkt, KEY_TILE, QBLK), F32),
                        pltpu.VMEM((H // 2, KV_LORA + ONES_ROWS, 2 * QBLK), F32),
                        pltpu.VMEM((H * AT_DH, QBLK), F32)],
        compiler_params=_cparams(("arbitrary", "arbitrary")),
        name="at_core",
    )(qlat, qit, wit, ki, ckv, ckvt, bias, wuvt, wo, h)


def _rel_bucket(n):
    max_exact = REL_BUCKETS // 2
    nf = jnp.maximum(n, 1).astype(F32)
    large = max_exact + (jnp.log(nf / max_exact) / math.log(REL_MAX_DIST / max_exact)
                         * (REL_BUCKETS - max_exact)).astype(I32)
    large = jnp.minimum(large, REL_BUCKETS - 1)
    return jnp.where(n < max_exact, n, large)


def _bias_tiles(rel_bias):
    H = rel_bias.shape[1]
    span = KEY_TILE + QBLK - 1
    m = jnp.arange(NEAR_TILES * QBLK + span, dtype=I32)
    tab = rel_bias[_rel_bucket(jnp.maximum(m - (KEY_TILE - 1), 0))] - rel_bias[REL_BUCKETS - 1]
    tiles = []
    for d in range(NEAR_TILES):
        u = tab[d * QBLK:d * QBLK + span][::-1]
        r = jnp.tile(u, (KEY_TILE + 1, 1))[:KEY_TILE * (span + 1)].reshape(KEY_TILE, span + 1, H)
        tiles.append(r[:, :QBLK][:, ::-1])
    tiles.append(jnp.zeros_like(tiles[0]))
    t = jnp.stack(tiles).astype(F32) * LOG2E
    t = jnp.transpose(t, (0, 3, 1, 2))
    return jnp.concatenate([t[:, 0::2], t[:, 1::2]], axis=-1)


def kernel(x, p, hg_norm, hg_w_in, hg_lb, hg_onorm, hg_w_out, at_norm, at_w_in, at_q_norm,
           at_kv_norm, at_w_uq, at_w_uk, at_w_uv, at_w_qidx, at_w_out, rel_bias, ff_norm, ff_w_up,
           ff_conv_w, ff_conv_b, ff_w_down, ple_norm, ple_w_gate, ple_w_proj, final_norm):
    B, L, D = x.shape
    T = B * L
    depth = p.shape[0]
    row = lambda a: a.reshape(1, -1)
    h = x.reshape(T, D)
    for i in range(depth):
        j = i // 2
        if i % 2 == 0:
            q, k, v, gs, lf = _hg_in(h, row(hg_norm[j]), hg_lb, hg_w_in[j].astype(BF16), i)
            h = _hg_rec(q, k, v, gs, lf, h, row(hg_onorm[j]), hg_w_out[j].astype(BF16), B, L)
        else:
            w_in = at_w_in[j].astype(BF16)
            o1, o2, o3 = Q_LORA, Q_LORA + KV_LORA, Q_LORA + KV_LORA + IDX_DIM
            wuk = at_w_uk[j].astype(BF16)
            z = jnp.zeros_like(wuk[0::2])
            wuk_bd = jnp.concatenate([jnp.concatenate([wuk[0::2], z], axis=2),
                                      jnp.concatenate([z, wuk[1::2]], axis=2)], axis=1)
            wuvt = jnp.transpose(at_w_uv[j].astype(BF16), (0, 2, 1))
            qlat, qit, wit, ki, ckv, ckvt = _at_in(
                h, row(at_norm[j]), w_in[:, :o1], w_in[:, o1:o2], w_in[:, o2:o3], w_in[:, o3:].T,
                row(at_q_norm[j]), row(at_kv_norm[j]), at_w_uq[j].astype(BF16),
                at_w_qidx[j].astype(BF16).T, jnp.transpose(wuk_bd, (0, 2, 1)))
            topk = max(1, min(TOPK_MAX, L // 4))
            h = _at_core(qlat, qit, wit, ki, ckv, ckvt, _bias_tiles(rel_bias), wuvt,
                         at_w_out[j].astype(BF16), h, B, L, topk)
        h = _ffn_ple(h, row(ff_norm[i]), ff_w_up[i].astype(BF16), ff_conv_w[i], ff_conv_b[i],
                     ff_w_down[i].astype(BF16), p[i].reshape(T, PLE_DIM), row(ple_norm[i]),
                     ple_w_gate[i].astype(BF16), ple_w_proj[i].astype(BF16), row(final_norm),
                     i == depth - 1, L)
    return h.reshape(B, L, D)
```

```python
import functools
import math

import jax
import jax.numpy as jnp
import numpy as np
from jax import lax
from jax.experimental import pallas as pl
from jax.experimental.pallas import tpu as pltpu

F32 = jnp.float32
BF16 = jnp.bfloat16
I32 = jnp.int32

D_MODEL = 1024
HG_HEADS = 8
HG_DK = D_MODEL // HG_HEADS
AT_HEADS = 16
AT_DH = 64
Q_LORA = 384
KV_LORA = 256
IDX_HEADS = 8
IDX_DIM = 64
TOPK_MAX = 256
QBLK = 128
REL_BUCKETS = 32
REL_MAX_DIST = 128
PLE_DIM = 256
EPS = 1e-6

VMEM_LIMIT_V7X = 56 * 1024 * 1024
SUB = 16
KEY_TILE = 256
NEAR_TILES = (KEY_TILE + REL_MAX_DIST) // QBLK
NEG_BIG = -1e30
LOG2E = 1.4426950408889634
ONES_ROWS = 16
INT_MIN = -(2 ** 31)


def _cparams(sem):
    return pltpu.CompilerParams(dimension_semantics=sem, vmem_limit_bytes=VMEM_LIMIT_V7X)


def _rms(x, g):
    return x * lax.rsqrt(jnp.mean(x * x, axis=-1, keepdims=True) + EPS) * g


def _dot(a, b):
    return jnp.dot(a, b, preferred_element_type=F32)


def _dot_nt(a, b):
    return lax.dot_general(a, b, (((1,), (1,)), ((), ())), preferred_element_type=F32)


def _dot_tn(a, b):
    return lax.dot_general(a, b, (((0,), (0,)), ((), ())), preferred_element_type=F32)


def _full(shape, single=False):
    n = len(shape)
    if single:
        return pl.BlockSpec(shape, lambda *_: (0,) * n, pipeline_mode=pl.Buffered(1))
    return pl.BlockSpec(shape, lambda *_: (0,) * n)


def _hg_in_kernel(layer, x_ref, g_ref, lbp_ref, w_ref, q_ref, k_ref, v_ref, gs_ref, lf_ref):
    D = D_MODEL
    xn = _rms(x_ref[...], g_ref[...]).astype(BF16)
    lbp = lbp_ref[...]
    e = jnp.exp(lbp - jnp.max(lbp, axis=0, keepdims=True))
    lb = jnp.sum(e[: layer + 1], axis=0, keepdims=True) / jnp.sum(e, axis=0, keepdims=True)
    q = _dot(xn, w_ref[:, 0:D])
    q_ref[...] = (q * jax.nn.sigmoid(q)).astype(BF16)
    f = lb + (1.0 - lb) * jax.nn.sigmoid(_dot(xn, w_ref[:, D:2 * D]))
    lf_ref[...] = jnp.log(f) * LOG2E
    k_ref[...] = (1.0 - f).astype(BF16)
    v_ref[...] = _dot(xn, w_ref[:, 2 * D:3 * D]).astype(BF16)
    g = _dot(xn, w_ref[:, 3 * D:4 * D])
    gs_ref[...] = (g * jax.nn.sigmoid(g)).astype(BF16)


def _hg_in(x, g, lbp, w, layer, tm=512):
    T, D = x.shape
    tok = pl.BlockSpec((tm, D), lambda i: (i, 0))
    bf = jax.ShapeDtypeStruct((T, D), BF16)
    return pl.pallas_call(
        functools.partial(_hg_in_kernel, layer),
        grid=(T // tm,),
        in_specs=[tok, _full((1, D)), _full(lbp.shape), _full(w.shape)],
        out_specs=[tok] * 5,
        out_shape=[bf, bf, bf, bf, jax.ShapeDtypeStruct((T, D), F32)],
        compiler_params=_cparams(("parallel",)),
        name="hg_in",
    )(x, g, lbp, w)


def _hg_rec_kernel(lt, q_ref, k_ref, v_ref, gs_ref, lf_ref, x_ref, on_ref, wo_ref, tri_ref,
                   out_ref, st_ref, b_ref, o_ref):
    H, DK = HG_HEADS, HG_DK

    @pl.when(pl.program_id(1) == 0)
    def _():
        st_ref[...] = jnp.zeros_like(st_ref)

    lf = lf_ref[...]
    hi = lf.astype(BF16)
    r1 = lf - hi.astype(F32)
    mid = r1.astype(BF16)
    lo = (r1 - mid.astype(F32)).astype(BF16)
    tri = tri_ref[...]
    b_ref[...] = _dot(tri, hi) + _dot(tri, mid) + _dot(tri, lo)

    HALF = SUB // 2
    row_h = lax.broadcasted_iota(I32, (HALF, DK), 0)

    def sub_chunk(c, carry):
        r0 = pl.multiple_of(c * SUB, SUB)
        for h in range(H):
            ls = slice(h * DK, (h + 1) * DK)
            b = b_ref[pl.ds(r0, SUB), ls]
            q = q_ref[pl.ds(r0, SUB), ls].astype(F32)
            k = k_ref[pl.ds(r0, SUB), ls].astype(F32)
            v = v_ref[pl.ds(r0, SUB), ls].astype(F32)
            be = b[SUB - 1:SUB, :]
            st = st_ref[h]
            o = _dot_nt((q * jnp.exp2(b)).astype(BF16), st.astype(BF16))
            o_top = jnp.zeros((HALF, DK), F32)
            o_bot = jnp.zeros((HALF, DK), F32)
            for s in range(SUB):
                bs, ks, vs = b[s:s + 1, :], k[s:s + 1, :], v[s:s + 1, :]
                if s < HALF:
                    e = jnp.exp2(jnp.where(row_h >= s, b[:HALF] - bs, NEG_BIG))
                    o_top = o_top + jnp.sum(q[:HALF] * e * ks, axis=-1, keepdims=True) * vs
                    e = jnp.exp2(b[HALF:] - bs)
                else:
                    e = jnp.exp2(jnp.where(row_h >= s - HALF, b[HALF:] - bs, NEG_BIG))
                o_bot = o_bot + jnp.sum(q[HALF:] * e * ks, axis=-1, keepdims=True) * vs
            o_ref[pl.ds(r0, SUB), ls] = o + jnp.concatenate([o_top, o_bot], axis=0)
            kd = (k * jnp.exp2(be - b)).astype(BF16)
            st_ref[h] = st * jnp.exp2(be) + _dot_tn(v.astype(BF16), kd)
        return carry

    lax.fori_loop(0, lt // SUB, sub_chunk, 0)

    on = on_ref[...]
    for h in range(H):
        ls = slice(h * DK, (h + 1) * DK)
        oh = _rms(o_ref[:, ls], on[:, ls]) * gs_ref[:, ls].astype(F32)
        o_ref[:, ls] = oh
    out_ref[...] = x_ref[...] + _dot(o_ref[...].astype(BF16), wo_ref[...])


def _hg_rec(q, k, v, gs, lf, x, onorm, wo, B, L, lt=256):
    T, D = x.shape
    nl = L // lt
    tok = pl.BlockSpec((lt, D), lambda b, j: (b * nl + j, 0))
    tri = (np.arange(lt)[:, None] >= np.arange(lt)[None, :]) & (
        np.arange(lt)[:, None] // SUB == np.arange(lt)[None, :] // SUB)
    tri = jnp.asarray(tri, BF16)
    return pl.pallas_call(
        functools.partial(_hg_rec_kernel, lt),
        grid=(B, nl),
        in_specs=[tok] * 6 + [_full((1, D)), _full((D, D)), _full((lt, lt))],
        out_specs=tok,
        out_shape=jax.ShapeDtypeStruct((T, D), F32),
        scratch_shapes=[pltpu.VMEM((HG_HEADS, HG_DK, HG_DK), F32),
                        pltpu.VMEM((lt, D), F32),
                        pltpu.VMEM((lt, D), F32)],
        compiler_params=_cparams(("arbitrary", "arbitrary")),
        name="hg_rec",
    )(q, k, v, gs, lf, x, onorm, wo, tri)


def _ffn_kernel(nseq, nf, final, h_ref, g_ref, wg_ref, wv_ref, cwg_ref, cwv_ref, cbg_ref, cbv_ref,
                wd_ref, p_ref, pg_ref, pwg_ref, pwp_ref, fn_ref, out_ref, xn_ref, cg_ref, cv_ref):
    i, j = pl.program_id(0), pl.program_id(1)

    @pl.when(j == 0)
    def _():
        xn_ref[...] = _rms(h_ref[...], g_ref[...]).astype(BF16)
        out_ref[...] = h_ref[...]

    seq_start = (i % nseq) == 0
    xn = xn_ref[...]

    def conv(u, w_ref, b_ref, carry_ref):
        fc = u.shape[1]
        carry = jnp.where(seq_start, 0.0, carry_ref[j])
        carry_ref[j] = u[u.shape[0] - 8:, :]
        r1 = pltpu.roll(u, 1, 0)
        r2 = pltpu.roll(u, 2, 0)
        rows = lax.broadcasted_iota(I32, (8, fc), 0)
        top1 = jnp.where(rows < 1, pltpu.roll(carry, 1, 0), r1[0:8])
        top2 = jnp.where(rows < 2, pltpu.roll(carry, 2, 0), r2[0:8])
        p1 = jnp.concatenate([top1, r1[8:]], axis=0)
        p2 = jnp.concatenate([top2, r2[8:]], axis=0)
        w = w_ref[0]
        return u * w[2:3, :] + p1 * w[1:2, :] + p2 * w[0:1, :] + b_ref[0]

    ug = conv(_dot(xn, wg_ref[...]), cwg_ref, cbg_ref, cg_ref)
    uv = conv(_dot(xn, wv_ref[...]), cwv_ref, cbv_ref, cv_ref)
    act = (ug * jax.nn.sigmoid(ug) * uv).astype(BF16)
    out_ref[...] += _dot(act, wd_ref[...])

    @pl.when(j == nf - 1)
    def _():
        h = out_ref[...]
        gate = jax.nn.sigmoid(_dot(_rms(h, pg_ref[...]).astype(BF16), pwg_ref[...]))
        h = h + gate * _dot(p_ref[...].astype(BF16), pwp_ref[...])
        if final:
            h = _rms(h, fn_ref[...])
        out_ref[...] = h


def _ffn_ple(h, g, w_up, conv_w, conv_b, w_down, p, pg, pwg, pwp, fnorm, final, L, tm=512, nf=2):
    T, D = h.shape
    F = w_down.shape[0]
    fc = F // nf
    assert fc % 128 == 0 and L % tm == 0
    cw = jnp.pad(conv_w, ((0, 5), (0, 0)))
    cwg = cw[:, :F].reshape(8, nf, fc).transpose(1, 0, 2)
    cwv = cw[:, F:].reshape(8, nf, fc).transpose(1, 0, 2)
    cbg = conv_b[:F].reshape(nf, 1, fc)
    cbv = conv_b[F:].reshape(nf, 1, fc)
    tok = pl.BlockSpec((tm, D), lambda i, j: (i, 0))
    return pl.pallas_call(
        functools.partial(_ffn_kernel, L // tm, nf, final),
        grid=(T // tm, nf),
        in_specs=[tok, _full((1, D)),
                  pl.BlockSpec((D, fc), lambda i, j: (0, j)),
                  pl.BlockSpec((D, fc), lambda i, j: (0, nf + j)),
                  pl.BlockSpec((1, 8, fc), lambda i, j: (j, 0, 0)),
                  pl.BlockSpec((1, 8, fc), lambda i, j: (j, 0, 0)),
                  pl.BlockSpec((1, 1, fc), lambda i, j: (j, 0, 0)),
                  pl.BlockSpec((1, 1, fc), lambda i, j: (j, 0, 0)),
                  pl.BlockSpec((fc, D), lambda i, j: (j, 0)),
                  pl.BlockSpec((tm, PLE_DIM), lambda i, j: (i, 0)), _full((1, D)),
                  _full((D, D), single=True), _full((PLE_DIM, D), single=True), _full((1, D))],
        out_specs=tok,
        out_shape=jax.ShapeDtypeStruct((T, D), F32),
        scratch_shapes=[pltpu.VMEM((tm, D), BF16),
                        pltpu.VMEM((nf, 8, fc), F32),
                        pltpu.VMEM((nf, 8, fc), F32)],
        compiler_params=_cparams(("arbitrary", "arbitrary")),
        name="conv_ffn_ple",
    )(h, g, w_up, w_up, cwg, cwv, cbg, cbv, w_down, p, pg, pwg, pwp, fnorm)


def _at_in_kernel(h_ref, g_ref, wcq_ref, wckv_ref, wki_ref, wwi_ref, qn_ref, kvn_ref,
                  wuq_ref, wqi_ref, wuk_ref,
                  qlat_ref, qit_ref, wit_ref, ki_ref, ckv_ref, ckvt_ref):
    tm = h_ref.shape[0]
    xn = _rms(h_ref[...], g_ref[...]).astype(BF16)
    cq = _rms(_dot(xn, wcq_ref[...]), qn_ref[...]).astype(BF16)
    ckv = _rms(_dot(xn, wckv_ref[...]), kvn_ref[...])
    ckv_ref[...] = ckv.astype(BF16)
    for c in range(tm // KEY_TILE):
        ckvt_ref[c] = jnp.concatenate([ckv[c * KEY_TILE:(c + 1) * KEY_TILE, :].T.astype(BF16),
                                       jnp.ones((ONES_ROWS, KEY_TILE), BF16)], axis=0)
    ki_ref[...] = _dot(xn, wki_ref[...]).astype(BF16)
    nq = tm // QBLK
    wit = _dot_nt(wwi_ref[...], xn) * (IDX_HEADS ** -0.5 * IDX_DIM ** -0.5)
    qit = _dot_nt(wqi_ref[...], cq).astype(BF16)
    for c in range(nq):
        wit_ref[c] = wit[:, c * QBLK:(c + 1) * QBLK]
        qit_ref[c] = qit[:, c * QBLK:(c + 1) * QBLK]
    qn = _dot(cq, wuq_ref[...]).astype(BF16)
    for pr in range(AT_HEADS // 2):
        ql = (_dot_nt(wuk_ref[pr], qn[:, pr * 128:(pr + 1) * 128])
              * (AT_DH ** -0.5 * LOG2E)).astype(BF16)
        for c in range(nq):
            qlat_ref[c, 2 * pr] = ql[:KV_LORA, c * QBLK:(c + 1) * QBLK]
            qlat_ref[c, 2 * pr + 1] = ql[KV_LORA:, c * QBLK:(c + 1) * QBLK]


def _at_in(h, g, wcq, wckv, wki, wwi_t, qn, kvn, wuq, wqi_t, wuk_bd_t, tm=512):
    T, D = h.shape
    nk = tm // KEY_TILE
    ins = [h, g, wcq, wckv, wki, wwi_t, qn, kvn, wuq, wqi_t, wuk_bd_t]
    in_specs = [pl.BlockSpec((tm, D), lambda i: (i, 0))] + [_full(a.shape) for a in ins[1:]]
    return pl.pallas_call(
        _at_in_kernel,
        grid=(T // tm,),
        in_specs=in_specs,
        out_specs=[pl.BlockSpec((tm // QBLK, AT_HEADS, KV_LORA, QBLK), lambda i: (i, 0, 0, 0)),
                   pl.BlockSpec((tm // QBLK, IDX_HEADS * IDX_DIM, QBLK), lambda i: (i, 0, 0)),
                   pl.BlockSpec((tm // QBLK, IDX_HEADS, QBLK), lambda i: (i, 0, 0)),
                   pl.BlockSpec((tm, IDX_DIM), lambda i: (i, 0)),
                   pl.BlockSpec((tm, KV_LORA), lambda i: (i, 0)),
                   pl.BlockSpec((nk, KV_LORA + ONES_ROWS, KEY_TILE), lambda i: (i, 0, 0))],
        out_shape=[jax.ShapeDtypeStruct((T // QBLK, AT_HEADS, KV_LORA, QBLK), BF16),
                   jax.ShapeDtypeStruct((T // QBLK, IDX_HEADS * IDX_DIM, QBLK), BF16),
                   jax.ShapeDtypeStruct((T // QBLK, IDX_HEADS, QBLK), F32),
                   jax.ShapeDtypeStruct((T, IDX_DIM), BF16),
                   jax.ShapeDtypeStruct((T, KV_LORA), BF16),
                   jax.ShapeDtypeStruct((T // KEY_TILE, KV_LORA + ONES_ROWS, KEY_TILE), BF16)],
        compiler_params=_cparams(("parallel",)),
        name="at_in",
    )(*ins)


def _at_core_kernel(topk, qlat_ref, qit_ref, wit_ref, ki_ref, ckv_ref, ckvt_ref, bias_ref,
                    wuvt_ref, wo_ref, h_ref, out_ref, ik_ref, planes_ref, mask_ref, acc_ref, ot_ref):
    H, TK = AT_HEADS, KEY_TILE
    assert TK == 32 * 8
    qb = pl.program_id(1)
    t0 = qb * QBLK
    nkt = qb // (TK // QBLK) + 1
    lane_q = lax.broadcasted_iota(I32, (TK, QBLK), 1)
    row_k = lax.broadcasted_iota(I32, (TK, QBLK), 0)

    wit = wit_ref[...]

    def score_tile(kt, carry):
        k0 = pl.multiple_of(kt * TK, TK)
        kk = ki_ref[pl.ds(k0, TK), :]
        acc = jnp.zeros((TK, 2 * QBLK), F32)
        for pr in range(IDX_HEADS // 2):
            qa = qit_ref[(2 * pr) * IDX_DIM:(2 * pr + 1) * IDX_DIM, :]
            qb_ = qit_ref[(2 * pr + 1) * IDX_DIM:(2 * pr + 2) * IDX_DIM, :]
            rel = jnp.maximum(_dot(kk, jnp.concatenate([qa, qb_], axis=1)), 0.0)
            w2 = jnp.concatenate([wit[2 * pr:2 * pr + 1, :], wit[2 * pr + 1:2 * pr + 2, :]], axis=1)
            acc = acc + rel * w2
        score = acc[:, :QBLK] + acc[:, QBLK:] + 0.0
        bits = pltpu.bitcast(score, I32)
        ikey = bits ^ ((bits >> 31) & 0x7FFFFFFF)
        causal = (k0 + row_k) <= (t0 + lane_q)
        xk = jnp.where(causal, ikey, INT_MIN)
        ik_ref[pl.ds(k0, TK), :] = xk
        u = (xk ^ INT_MIN).reshape(TK // 8, 8, QBLK)
        a = [u[v] for v in range(32)]
        m, j = 0x0000FFFF, 16
        while j:
            k = 0
            while k < 32:
                t = (a[k] ^ lax.shift_right_logical(a[k + j], jnp.full_like(a[k], j))) & m
                a[k] = a[k] ^ t
                a[k + j] = a[k + j] ^ (t << j)
                k = (k + j + 1) & ~j
            j >>= 1
            m = m ^ ((m << j) & 0xFFFFFFFF)
        for i in range(32):
            planes_ref[i, pl.ds(kt, 1)] = a[i][None]
        return carry

    @pl.when((pl.program_id(0) == 0) & (qb == 0))
    def _():
        planes_ref[...] = jnp.zeros_like(planes_ref)

    CT = 2 * TK
    nct = (nkt + 1) // 2
    lax.fori_loop(0, nct, lambda c, x: score_tile(2 * c + 1, score_tile(2 * c, x)), 0)

    row_c = lax.broadcasted_iota(I32, (CT, QBLK), 0)

    def count(pred):
        def body(ct, c):
            k0 = pl.multiple_of(ct * CT, CT)
            hit = jnp.where(pred(ik_ref[pl.ds(k0, CT), :], k0 + row_c), 1, 0)
            return c + jnp.sum(hit.reshape(CT // 8, 8, QBLK), axis=0)
        c = lax.fori_loop(0, nct, body, jnp.zeros((8, QBLK), I32))
        return jnp.sum(c, axis=0, keepdims=True)

    NT = planes_ref.shape[1]

    def bit_pass(i, st):
        alive, rem, tau_u = st
        t = alive & planes_ref[i]
        c = jnp.sum(jnp.sum(lax.population_count(t), axis=0), axis=0, keepdims=True)
        take = c >= rem
        return (jnp.where(take, t, alive ^ t), jnp.where(take, rem, rem - c),
                jnp.where(take, tau_u | (1 << (31 - i)), tau_u))

    alive0 = jnp.where(lax.broadcasted_iota(I32, (NT, 8, QBLK), 0) < nkt, -1, 0)
    ties, need, tau_u = lax.fori_loop(0, 32, bit_pass, (alive0, jnp.full((1, QBLK), topk, I32),
                                                        jnp.zeros((1, QBLK), I32)))
    tau = tau_u ^ INT_MIN
    n_ties = jnp.sum(jnp.sum(lax.population_count(ties), axis=0), axis=0, keepdims=True)

    idx_bits = ik_ref.shape[0].bit_length()

    def tie_search(_):
        def step(i, jc):
            cand = jc + (1 << (idx_bits - 1 - i))
            n = count(lambda x, kidx: (x == tau) & (kidx < cand))
            return jnp.where(n <= need, cand, jc)
        return lax.fori_loop(0, idx_bits, step, jnp.zeros((1, QBLK), I32))

    jcut = lax.cond(jnp.max(n_ties - need) > 0, tie_search,
                    lambda _: jnp.full((1, QBLK), 2 ** idx_bits, I32), 0)

    def mask_tile(kt, carry):
        k0 = pl.multiple_of(kt * TK, TK)
        x = ik_ref[pl.ds(k0, TK), :]
        kidx = k0 + row_k
        sel = ((x > tau) | ((x == tau) & (kidx < jcut))) & (kidx <= (t0 + lane_q))
        mask_ref[kt] = jnp.where(sel, 0.0, NEG_BIG)
        return carry

    lax.fori_loop(0, nkt, mask_tile, 0)

    nfar = jnp.maximum(nkt - 2, 0)

    NP = H // 2
    acc_ref[...] = jnp.zeros_like(acc_ref)

    def tile(near, kt, m_all):
        k0 = pl.multiple_of(kt * TK, TK)
        am = mask_ref[kt]
        am2 = jnp.concatenate([am, am], axis=1)
        ck = ckv_ref[pl.ds(k0, TK), :]
        ckt = ckvt_ref[kt]
        ms = []
        for pr in range(NP):
            qt = jnp.concatenate([qlat_ref[2 * pr], qlat_ref[2 * pr + 1]], axis=1)
            lg = _dot(ck, qt) + am2
            if near:
                lg = lg + bias_ref[jnp.minimum((t0 - k0) // QBLK, NEAR_TILES), pr]
            m_old = m_all[pr:pr + 1, :]
            m_new = jnp.maximum(m_old, jnp.max(lg, axis=0, keepdims=True))
            p = jnp.exp2((lg - m_new).astype(BF16))
            acc_ref[pr] = acc_ref[pr] * jnp.exp2(m_old - m_new) + _dot(ckt, p)
            ms.append(m_new)
        return jnp.concatenate(ms, axis=0)

    m_all = jnp.full((NP, 2 * QBLK), NEG_BIG, F32)
    m_all = lax.fori_loop(0, nfar, functools.partial(tile, False), m_all)
    m_all = lax.fori_loop(nfar, nkt, functools.partial(tile, True), m_all)
    for pr in range(NP):
        a = acc_ref[pr]
        olat = (a[:KV_LORA] / a[KV_LORA:KV_LORA + 1]).astype(BF16)
        ot_ref[pr * 2 * AT_DH:(pr + 1) * 2 * AT_DH, :] = jnp.concatenate(
            [_dot(wuvt_ref[2 * pr], olat[:, :QBLK]), _dot(wuvt_ref[2 * pr + 1], olat[:, QBLK:])], axis=0)
    out_ref[...] = h_ref[...] + _dot(ot_ref[...].T.astype(BF16), wo_ref[...])


def _at_core(qlat, qit, wit, ki, ckv, ckvt, bias, wuvt, wo, h, B, L, topk):
    T, D = h.shape
    nblk = L // QBLK
    nkt = L // KEY_TILE
    H = AT_HEADS
    assert L % (2 * KEY_TILE) == 0
    return pl.pallas_call(
        functools.partial(_at_core_kernel, topk),
        grid=(B, nblk),
        in_specs=[pl.BlockSpec((None, H, KV_LORA, QBLK), lambda b, q: (b * nblk + q, 0, 0, 0)),
                  pl.BlockSpec((None, IDX_HEADS * IDX_DIM, QBLK), lambda b, q: (b * nblk + q, 0, 0)),
                  pl.BlockSpec((None, IDX_HEADS, QBLK), lambda b, q: (b * nblk + q, 0, 0)),
                  pl.BlockSpec((L, IDX_DIM), lambda b, q: (b, 0)),
                  pl.BlockSpec((L, KV_LORA), lambda b, q: (b, 0)),
                  pl.BlockSpec((nkt, KV_LORA + ONES_ROWS, KEY_TILE), lambda b, q: (b, 0, 0)),
                  _full(bias.shape, single=True), _full(wuvt.shape, single=True),
                  _full(wo.shape, single=True),
                  pl.BlockSpec((QBLK, D), lambda b, q: (b * nblk + q, 0))],
        out_specs=pl.BlockSpec((QBLK, D), lambda b, q: (b * nblk + q, 0)),
        out_shape=jax.ShapeDtypeStruct((T, D), F32),
        scratch_shapes=[pltpu.VMEM((L, QBLK), I32),
                        pltpu.VMEM((32, nkt, 8, QBLK), I32),
                        pltpu.VMEM((nkt, KEY_TILE, QBLK), F32),
                        pltpu.VMEM((H // 2, KV_LORA + ONES_ROWS, 2 * QBLK), F32),
                        pltpu.VMEM((H * AT_DH, QBLK), F32)],
        compiler_params=_cparams(("arbitrary", "arbitrary")),
        name="at_core",
    )(qlat, qit, wit, ki, ckv, ckvt, bias, wuvt, wo, h)


def _rel_bucket(n):
    max_exact = REL_BUCKETS // 2
    nf = jnp.maximum(n, 1).astype(F32)
    large = max_exact + (jnp.log(nf / max_exact) / math.log(REL_MAX_DIST / max_exact)
                         * (REL_BUCKETS - max_exact)).astype(I32)
    large = jnp.minimum(large, REL_BUCKETS - 1)
    return jnp.where(n < max_exact, n, large)


def _bias_tiles(rel_bias):
    H = rel_bias.shape[1]
    span = KEY_TILE + QBLK - 1
    m = jnp.arange(NEAR_TILES * QBLK + span, dtype=I32)
    tab = rel_bias[_rel_bucket(jnp.maximum(m - (KEY_TILE - 1), 0))] - rel_bias[REL_BUCKETS - 1]
    tiles = []
    for d in range(NEAR_TILES):
        u = tab[d * QBLK:d * QBLK + span][::-1]
        r = jnp.tile(u, (KEY_TILE + 1, 1))[:KEY_TILE * (span + 1)].reshape(KEY_TILE, span + 1, H)
        tiles.append(r[:, :QBLK][:, ::-1])
    tiles.append(jnp.zeros_like(tiles[0]))
    t = jnp.stack(tiles).astype(F32) * LOG2E
    t = jnp.transpose(t, (0, 3, 1, 2))
    return jnp.concatenate([t[:, 0::2], t[:, 1::2]], axis=-1)


def kernel(x, p, hg_norm, hg_w_in, hg_lb, hg_onorm, hg_w_out, at_norm, at_w_in, at_q_norm,
           at_kv_norm, at_w_uq, at_w_uk, at_w_uv, at_w_qidx, at_w_out, rel_bias, ff_norm, ff_w_up,
           ff_conv_w, ff_conv_b, ff_w_down, ple_norm, ple_w_gate, ple_w_proj, final_norm):
    B, L, D = x.shape
    T = B * L
    depth = p.shape[0]
    row = lambda a: a.reshape(1, -1)
    h = x.reshape(T, D)
    for i in range(depth):
        j = i // 2
        if i % 2 == 0:
            q, k, v, gs, lf = _hg_in(h, row(hg_norm[j]), hg_lb, hg_w_in[j].astype(BF16), i)
            h = _hg_rec(q, k, v, gs, lf, h, row(hg_onorm[j]), hg_w_out[j].astype(BF16), B, L)
        else:
            w_in = at_w_in[j].astype(BF16)
            o1, o2, o3 = Q_LORA, Q_LORA + KV_LORA, Q_LORA + KV_LORA + IDX_DIM
            wuk = at_w_uk[j].astype(BF16)
            z = jnp.zeros_like(wuk[0::2])
            wuk_bd = jnp.concatenate([jnp.concatenate([wuk[0::2], z], axis=2),
                                      jnp.concatenate([z, wuk[1::2]], axis=2)], axis=1)
            wuvt = jnp.transpose(at_w_uv[j].astype(BF16), (0, 2, 1))
            qlat, qit, wit, ki, ckv, ckvt = _at_in(
                h, row(at_norm[j]), w_in[:, :o1], w_in[:, o1:o2], w_in[:, o2:o3], w_in[:, o3:].T,
                row(at_q_norm[j]), row(at_kv_norm[j]), at_w_uq[j].astype(BF16),
                at_w_qidx[j].astype(BF16).T, jnp.transpose(wuk_bd, (0, 2, 1)))
            topk = max(1, min(TOPK_MAX, L // 4))
            h = _at_core(qlat, qit, wit, ki, ckv, ckvt, _bias_tiles(rel_bias), wuvt,
                         at_w_out[j].astype(BF16), h, B, L, topk)
        h = _ffn_ple(h, row(ff_norm[i]), ff_w_up[i].astype(BF16), ff_conv_w[i], ff_conv_b[i],
                     ff_w_down[i].astype(BF16), p[i].reshape(T, PLE_DIM), row(ple_norm[i]),
                     ple_w_gate[i].astype(BF16), ple_w_proj[i].astype(BF16), row(final_norm),
                     i == depth - 1, L)
    return h.reshape(B, L, D)
```

```python
import functools
import math

import jax
import jax.numpy as jnp
import numpy as np
from jax import lax
from jax.experimental import pallas as pl
from jax.experimental.pallas import tpu as pltpu

F32 = jnp.float32
BF16 = jnp.bfloat16
I32 = jnp.int32

D_MODEL = 1024
HG_HEADS = 8
HG_DK = D_MODEL // HG_HEADS
AT_HEADS = 16
AT_DH = 64
Q_LORA = 384
KV_LORA = 256
IDX_HEADS = 8
IDX_DIM = 64
TOPK_MAX = 256
QBLK = 128
REL_BUCKETS = 32
REL_MAX_DIST = 128
PLE_DIM = 256
EPS = 1e-6

VMEM_LIMIT_V7X = 56 * 1024 * 1024
SUB = 16
KEY_TILE = 256
NEAR_TILES = (KEY_TILE + REL_MAX_DIST) // QBLK
NEG_BIG = -1e30
LOG2E = 1.4426950408889634
ONES_ROWS = 16
INT_MIN = -(2 ** 31)


def _cparams(sem):
    return pltpu.CompilerParams(dimension_semantics=sem, vmem_limit_bytes=VMEM_LIMIT_V7X)


def _rms(x, g):
    return x * lax.rsqrt(jnp.mean(x * x, axis=-1, keepdims=True) + EPS) * g


def _dot(a, b):
    return jnp.dot(a, b, preferred_element_type=F32)


def _dot_nt(a, b):
    return lax.dot_general(a, b, (((1,), (1,)), ((), ())), preferred_element_type=F32)


def _dot_tn(a, b):
    return lax.dot_general(a, b, (((0,), (0,)), ((), ())), preferred_element_type=F32)


def _full(shape, single=False):
    n = len(shape)
    if single:
        return pl.BlockSpec(shape, lambda *_: (0,) * n, pipeline_mode=pl.Buffered(1))
    return pl.BlockSpec(shape, lambda *_: (0,) * n)


def _hg_in_kernel(layer, x_ref, g_ref, lbp_ref, w_ref, q_ref, k_ref, v_ref, gs_ref, lf_ref):
    D = D_MODEL
    xn = _rms(x_ref[...], g_ref[...]).astype(BF16)
    lbp = lbp_ref[...]
    e = jnp.exp(lbp - jnp.max(lbp, axis=0, keepdims=True))
    lb = jnp.sum(e[: layer + 1], axis=0, keepdims=True) / jnp.sum(e, axis=0, keepdims=True)
    q = _dot(xn, w_ref[:, 0:D])
    q_ref[...] = (q * jax.nn.sigmoid(q)).astype(BF16)
    f = lb + (1.0 - lb) * jax.nn.sigmoid(_dot(xn, w_ref[:, D:2 * D]))
    lf_ref[...] = jnp.log(f) * LOG2E
    k_ref[...] = (1.0 - f).astype(BF16)
    v_ref[...] = _dot(xn, w_ref[:, 2 * D:3 * D]).astype(BF16)
    g = _dot(xn, w_ref[:, 3 * D:4 * D])
    gs_ref[...] = (g * jax.nn.sigmoid(g)).astype(BF16)


def _hg_in(x, g, lbp, w, layer, tm=512):
    T, D = x.shape
    tok = pl.BlockSpec((tm, D), lambda i: (i, 0))
    bf = jax.ShapeDtypeStruct((T, D), BF16)
    return pl.pallas_call(
        functools.partial(_hg_in_kernel, layer),
        grid=(T // tm,),
        in_specs=[tok, _full((1, D)), _full(lbp.shape), _full(w.shape)],
        out_specs=[tok] * 5,
        out_shape=[bf, bf, bf, bf, jax.ShapeDtypeStruct((T, D), F32)],
        compiler_params=_cparams(("parallel",)),
        name="hg_in",
    )(x, g, lbp, w)


def _hg_rec_kernel(lt, q_ref, k_ref, v_ref, gs_ref, lf_ref, x_ref, on_ref, wo_ref, tri_ref,
                   out_ref, st_ref, b_ref, o_ref):
    H, DK = HG_HEADS, HG_DK

    @pl.when(pl.program_id(1) == 0)
    def _():
        st_ref[...] = jnp.zeros_like(st_ref)

    lf = lf_ref[...]
    hi = lf.astype(BF16)
    r1 = lf - hi.astype(F32)
    mid = r1.astype(BF16)
    lo = (r1 - mid.astype(F32)).astype(BF16)
    tri = tri_ref[...]
    b_ref[...] = _dot(tri, hi) + _dot(tri, mid) + _dot(tri, lo)

    HALF = SUB // 2
    row_h = lax.broadcasted_iota(I32, (HALF, DK), 0)

    def sub_chunk(c, carry):
        r0 = pl.multiple_of(c * SUB, SUB)
        for h in range(H):
            ls = slice(h * DK, (h + 1) * DK)
            b = b_ref[pl.ds(r0, SUB), ls]
            q = q_ref[pl.ds(r0, SUB), ls].astype(F32)
            k = k_ref[pl.ds(r0, SUB), ls].astype(F32)
            v = v_ref[pl.ds(r0, SUB), ls].astype(F32)
            be = b[SUB - 1:SUB, :]
            st = st_ref[h]
            o = _dot_nt((q * jnp.exp2(b)).astype(BF16), st.astype(BF16))
            o_top = jnp.zeros((HALF, DK), F32)
            o_bot = jnp.zeros((HALF, DK), F32)
            for s in range(SUB):
                bs, ks, vs = b[s:s + 1, :], k[s:s + 1, :], v[s:s + 1, :]
                if s < HALF:
                    e = jnp.exp2(jnp.where(row_h >= s, b[:HALF] - bs, NEG_BIG))
                    o_top = o_top + jnp.sum(q[:HALF] * e * ks, axis=-1, keepdims=True) * vs
                    e = jnp.exp2(b[HALF:] - bs)
                else:
                    e = jnp.exp2(jnp.where(row_h >= s - HALF, b[HALF:] - bs, NEG_BIG))
                o_bot = o_bot + jnp.sum(q[HALF:] * e * ks, axis=-1, keepdims=True) * vs
            o_ref[pl.ds(r0, SUB), ls] = o + jnp.concatenate([o_top, o_bot], axis=0)
            kd = (k * jnp.exp2(be - b)).astype(BF16)
            st_ref[h] = st * jnp.exp2(be) + _dot_tn(v.astype(BF16), kd)
        return carry

    lax.fori_loop(0, lt // SUB, sub_chunk, 0)

    on = on_ref[...]
    for h in range(H):
        ls = slice(h * DK, (h + 1) * DK)
        oh = _rms(o_ref[:, ls], on[:, ls]) * gs_ref[:, ls].astype(F32)
        o_ref[:, ls] = oh
    out_ref[...] = x_ref[...] + _dot(o_ref[...].astype(BF16), wo_ref[...])


def _hg_rec(q, k, v, gs, lf, x, onorm, wo, B, L, lt=256):
    T, D = x.shape
    nl = L // lt
    tok = pl.BlockSpec((lt, D), lambda b, j: (b * nl + j, 0))
    tri = (np.arange(lt)[:, None] >= np.arange(lt)[None, :]) & (
        np.arange(lt)[:, None] // SUB == np.arange(lt)[None, :] // SUB)
    tri = jnp.asarray(tri, BF16)
    return pl.pallas_call(
        functools.partial(_hg_rec_kernel, lt),
        grid=(B, nl),
        in_specs=[tok] * 6 + [_full((1, D)), _full((D, D)), _full((lt, lt))],
        out_specs=tok,
        out_shape=jax.ShapeDtypeStruct((T, D), F32),
        scratch_shapes=[pltpu.VMEM((HG_HEADS, HG_DK, HG_DK), F32),
                        pltpu.VMEM((lt, D), F32),
                        pltpu.VMEM((lt, D), F32)],
        compiler_params=_cparams(("arbitrary", "arbitrary")),
        name="hg_rec",
    )(q, k, v, gs, lf, x, onorm, wo, tri)


def _ffn_kernel(nseq, nf, final, h_ref, g_ref, wg_ref, wv_ref, cwg_ref, cwv_ref, cbg_ref, cbv_ref,
                wd_ref, p_ref, pg_ref, pwg_ref, pwp_ref, fn_ref, out_ref, xn_ref, cg_ref, cv_ref):
    i, j = pl.program_id(0), pl.program_id(1)

    @pl.when(j == 0)
    def _():
        xn_ref[...] = _rms(h_ref[...], g_ref[...]).astype(BF16)
        out_ref[...] = h_ref[...]

    seq_start = (i % nseq) == 0
    xn = xn_ref[...]

    def conv(u, w_ref, b_ref, carry_ref):
        fc = u.shape[1]
        carry = jnp.where(seq_start, 0.0, carry_ref[j])
        carry_ref[j] = u[u.shape[0] - 8:, :]
        r1 = pltpu.roll(u, 1, 0)
        r2 = pltpu.roll(u, 2, 0)
        rows = lax.broadcasted_iota(I32, (8, fc), 0)
        top1 = jnp.where(rows < 1, pltpu.roll(carry, 1, 0), r1[0:8])
        top2 = jnp.where(rows < 2, pltpu.roll(carry, 2, 0), r2[0:8])
        p1 = jnp.concatenate([top1, r1[8:]], axis=0)
        p2 = jnp.concatenate([top2, r2[8:]], axis=0)
        w = w_ref[0]
        return u * w[2:3, :] + p1 * w[1:2, :] + p2 * w[0:1, :] + b_ref[0]

    ug = conv(_dot(xn, wg_ref[...]), cwg_ref, cbg_ref, cg_ref)
    uv = conv(_dot(xn, wv_ref[...]), cwv_ref, cbv_ref, cv_ref)
    act = (ug * jax.nn.sigmoid(ug) * uv).astype(BF16)
    out_ref[...] += _dot(act, wd_ref[...])

    @pl.when(j == nf - 1)
    def _():
        h = out_ref[...]
        gate = jax.nn.sigmoid(_dot(_rms(h, pg_ref[...]).astype(BF16), pwg_ref[...]))
        h = h + gate * _dot(p_ref[...].astype(BF16), pwp_ref[...])
        if final:
            h = _rms(h, fn_ref[...])
        out_ref[...] = h


def _ffn_ple(h, g, w_up, conv_w, conv_b, w_down, p, pg, pwg, pwp, fnorm, final, L, tm=512, nf=2):
    T, D = h.shape
    F = w_down.shape[0]
    fc = F // nf
    assert fc % 128 == 0 and L % tm == 0
    cw = jnp.pad(conv_w, ((0, 5), (0, 0)))
    cwg = cw[:, :F].reshape(8, nf, fc).transpose(1, 0, 2)
    cwv = cw[:, F:].reshape(8, nf, fc).transpose(1, 0, 2)
    cbg = conv_b[:F].reshape(nf, 1, fc)
    cbv = conv_b[F:].reshape(nf, 1, fc)
    tok = pl.BlockSpec((tm, D), lambda i, j: (i, 0))
    return pl.pallas_call(
        functools.partial(_ffn_kernel, L // tm, nf, final),
        grid=(T // tm, nf),
        in_specs=[tok, _full((1, D)),
                  pl.BlockSpec((D, fc), lambda i, j: (0, j)),
                  pl.BlockSpec((D, fc), lambda i, j: (0, nf + j)),
                  pl.BlockSpec((1, 8, fc), lambda i, j: (j, 0, 0)),
                  pl.BlockSpec((1, 8, fc), lambda i, j: (j, 0, 0)),
                  pl.BlockSpec((1, 1, fc), lambda i, j: (j, 0, 0)),
                  pl.BlockSpec((1, 1, fc), lambda i, j: (j, 0, 0)),
                  pl.BlockSpec((fc, D), lambda i, j: (j, 0)),
                  pl.BlockSpec((tm, PLE_DIM), lambda i, j: (i, 0)), _full((1, D)),
                  _full((D, D), single=True), _full((PLE_DIM, D), single=True), _full((1, D))],
        out_specs=tok,
        out_shape=jax.ShapeDtypeStruct((T, D), F32),
        scratch_shapes=[pltpu.VMEM((tm, D), BF16),
                        pltpu.VMEM((nf, 8, fc), F32),
                        pltpu.VMEM((nf, 8, fc), F32)],
        compiler_params=_cparams(("arbitrary", "arbitrary")),
        name="conv_ffn_ple",
    )(h, g, w_up, w_up, cwg, cwv, cbg, cbv, w_down, p, pg, pwg, pwp, fnorm)


def _at_in_kernel(h_ref, g_ref, wcq_ref, wckv_ref, wki_ref, wwi_ref, qn_ref, kvn_ref,
                  wuq_ref, wqi_ref, wuk_ref,
                  qlat_ref, qit_ref, wit_ref, ki_ref, ckv_ref, ckvt_ref):
    tm = h_ref.shape[0]
    xn = _rms(h_ref[...], g_ref[...]).astype(BF16)
    cq = _rms(_dot(xn, wcq_ref[...]), qn_ref[...]).astype(BF16)
    ckv = _rms(_dot(xn, wckv_ref[...]), kvn_ref[...])
    ckv_ref[...] = ckv.astype(BF16)
    for c in range(tm // KEY_TILE):
        ckvt_ref[c] = jnp.concatenate([ckv[c * KEY_TILE:(c + 1) * KEY_TILE, :].T.astype(BF16),
                                       jnp.ones((ONES_ROWS, KEY_TILE), BF16)], axis=0)
    ki_ref[...] = _dot(xn, wki_ref[...]).astype(BF16)
    nq = tm // QBLK
    wit = _dot_nt(wwi_ref[...], xn) * (IDX_HEADS ** -0.5 * IDX_DIM ** -0.5)
    qit = _dot_nt(wqi_ref[...], cq).astype(BF16)
    for c in range(nq):
        wit_ref[c] = wit[:, c * QBLK:(c + 1) * QBLK]
        qit_ref[c] = qit[:, c * QBLK:(c + 1) * QBLK]
    qn = _dot(cq, wuq_ref[...]).astype(BF16)
    for pr in range(AT_HEADS // 2):
        ql = (_dot_nt(wuk_ref[pr], qn[:, pr * 128:(pr + 1) * 128])
              * (AT_DH ** -0.5 * LOG2E)).astype(BF16)
        for c in range(nq):
            qlat_ref[c, 2 * pr] = ql[:KV_LORA, c * QBLK:(c + 1) * QBLK]
            qlat_ref[c, 2 * pr + 1] = ql[KV_LORA:, c * QBLK:(c + 1) * QBLK]


def _at_in(h, g, wcq, wckv, wki, wwi_t, qn, kvn, wuq, wqi_t, wuk_bd_t, tm=512):
    T, D = h.shape
    nk = tm // KEY_TILE
    ins = [h, g, wcq, wckv, wki, wwi_t, qn, kvn, wuq, wqi_t, wuk_bd_t]
    in_specs = [pl.BlockSpec((tm, D), lambda i: (i, 0))] + [_full(a.shape) for a in ins[1:]]
    return pl.pallas_call(
        _at_in_kernel,
        grid=(T // tm,),
        in_specs=in_specs,
        out_specs=[pl.BlockSpec((tm // QBLK, AT_HEADS, KV_LORA, QBLK), lambda i: (i, 0, 0, 0)),
                   pl.BlockSpec((tm // QBLK, IDX_HEADS * IDX_DIM, QBLK), lambda i: (i, 0, 0)),
                   pl.BlockSpec((tm // QBLK, IDX_HEADS, QBLK), lambda i: (i, 0, 0)),
                   pl.BlockSpec((tm, IDX_DIM), lambda i: (i, 0)),
                   pl.BlockSpec((tm, KV_LORA), lambda i: (i, 0)),
                   pl.BlockSpec((nk, KV_LORA + ONES_ROWS, KEY_TILE), lambda i: (i, 0, 0))],
        out_shape=[jax.ShapeDtypeStruct((T // QBLK, AT_HEADS, KV_LORA, QBLK), BF16),
                   jax.ShapeDtypeStruct((T // QBLK, IDX_HEADS * IDX_DIM, QBLK), BF16),
                   jax.ShapeDtypeStruct((T // QBLK, IDX_HEADS, QBLK), F32),
                   jax.ShapeDtypeStruct((T, IDX_DIM), BF16),
                   jax.ShapeDtypeStruct((T, KV_LORA), BF16),
                   jax.ShapeDtypeStruct((T // KEY_TILE, KV_LORA + ONES_ROWS, KEY_TILE), BF16)],
        compiler_params=_cparams(("parallel",)),
        name="at_in",
    )(*ins)


def _at_core_kernel(topk, qlat_ref, qit_ref, wit_ref, ki_ref, ckv_ref, ckvt_ref, bias_ref,
                    wuvt_ref, wo_ref, h_ref, out_ref, ik_ref, planes_ref, mask_ref, acc_ref, ot_ref):
    H, TK = AT_HEADS, KEY_TILE
    assert TK == 32 * 8
    qb = pl.program_id(1)
    t0 = qb * QBLK
    nkt = qb // (TK // QBLK) + 1
    lane_q = lax.broadcasted_iota(I32, (TK, QBLK), 1)
    row_k = lax.broadcasted_iota(I32, (TK, QBLK), 0)

    wit = wit_ref[...]

    def score_tile(kt, carry):
        k0 = pl.multiple_of(kt * TK, TK)
        kk = ki_ref[pl.ds(k0, TK), :]
        acc = jnp.zeros((TK, 2 * QBLK), F32)
        for pr in range(IDX_HEADS // 2):
            qa = qit_ref[(2 * pr) * IDX_DIM:(2 * pr + 1) * IDX_DIM, :]
            qb_ = qit_ref[(2 * pr + 1) * IDX_DIM:(2 * pr + 2) * IDX_DIM, :]
            rel = jnp.maximum(_dot(kk, jnp.concatenate([qa, qb_], axis=1)), 0.0)
            w2 = jnp.concatenate([wit[2 * pr:2 * pr + 1, :], wit[2 * pr + 1:2 * pr + 2, :]], axis=1)
            acc = acc + rel * w2
        score = acc[:, :QBLK] + acc[:, QBLK:] + 0.0
        bits = pltpu.bitcast(score, I32)
        ikey = bits ^ ((bits >> 31) & 0x7FFFFFFF)
        causal = (k0 + row_k) <= (t0 + lane_q)
        xk = jnp.where(causal, ikey, INT_MIN)
        ik_ref[pl.ds(k0, TK), :] = xk
        u = (xk ^ INT_MIN).reshape(TK // 8, 8, QBLK)
        a = [u[v] for v in range(32)]
        m, j = 0x0000FFFF, 16
        while j:
            k = 0
            while k < 32:
                t = (a[k] ^ lax.shift_right_logical(a[k + j], jnp.full_like(a[k], j))) & m
                a[k] = a[k] ^ t
                a[k + j] = a[k + j] ^ (t << j)
                k = (k + j + 1) & ~j
            j >>= 1
            m = m ^ ((m << j) & 0xFFFFFFFF)
        for i in range(32):
            planes_ref[i, pl.ds(kt, 1)] = a[i][None]
        return carry

    @pl.when((pl.program_id(0) == 0) & (qb == 0))
    def _():
        planes_ref[...] = jnp.zeros_like(planes_ref)

    CT = 2 * TK
    nct = (nkt + 1) // 2
    lax.fori_loop(0, nct, lambda c, x: score_tile(2 * c + 1, score_tile(2 * c, x)), 0)

    row_c = lax.broadcasted_iota(I32, (CT, QBLK), 0)

    def count(pred):
        def body(ct, c):
            k0 = pl.multiple_of(ct * CT, CT)
            hit = jnp.where(pred(ik_ref[pl.ds(k0, CT), :], k0 + row_c), 1, 0)
            return c + jnp.sum(hit.reshape(CT // 8, 8, QBLK), axis=0)
        c = lax.fori_loop(0, nct, body, jnp.zeros((8, QBLK), I32))
        return jnp.sum(c, axis=0, keepdims=True)

    NT = planes_ref.shape[1]

    def bit_pass(i, st):
        alive, rem, tau_u = st
        t = alive & planes_ref[i]
        c = jnp.sum(jnp.sum(lax.population_count(t), axis=0), axis=0, keepdims=True)
        take = c >= rem
        return (jnp.where(take, t, alive ^ t), jnp.where(take, rem, rem - c),
                jnp.where(take, tau_u | (1 << (31 - i)), tau_u))

    alive0 = jnp.where(lax.broadcasted_iota(I32, (NT, 8, QBLK), 0) < nkt, -1, 0)
    ties, need, tau_u = lax.fori_loop(0, 32, bit_pass, (alive0, jnp.full((1, QBLK), topk, I32),
                                                        jnp.zeros((1, QBLK), I32)))
    tau = tau_u ^ INT_MIN
    n_ties = jnp.sum(jnp.sum(lax.population_count(ties), axis=0), axis=0, keepdims=True)

    idx_bits = ik_ref.shape[0].bit_length()

    def tie_search(_):
        def step(i, jc):
            cand = jc + (1 << (idx_bits - 1 - i))
            n = count(lambda x, kidx: (x == tau) & (kidx < cand))
            return jnp.where(n <= need, cand, jc)
        return lax.fori_loop(0, idx_bits, step, jnp.zeros((1, QBLK), I32))

    jcut = lax.cond(jnp.max(n_ties - need) > 0, tie_search,
                    lambda _: jnp.full((1, QBLK), 2 ** idx_bits, I32), 0)

    def mask_tile(kt, carry):
        k0 = pl.multiple_of(kt * TK, TK)
        x = ik_ref[pl.ds(k0, TK), :]
        kidx = k0 + row_k
        sel = ((x > tau) | ((x == tau) & (kidx < jcut))) & (kidx <= (t0 + lane_q))
        mask_ref[kt] = jnp.where(sel, 0.0, NEG_BIG)
        return carry

    lax.fori_loop(0, nkt, mask_tile, 0)

    nfar = jnp.maximum(nkt - 2, 0)

    NP = H // 2
    acc_ref[...] = jnp.zeros_like(acc_ref)

    def tile(near, kt, m_all):
        k0 = pl.multiple_of(kt * TK, TK)
        am = mask_ref[kt]
        am2 = jnp.concatenate([am, am], axis=1)
        ck = ckv_ref[pl.ds(k0, TK), :]
        ckt = ckvt_ref[kt]
        ms = []
        for pr in range(NP):
            qt = jnp.concatenate([qlat_ref[2 * pr], qlat_ref[2 * pr + 1]], axis=1)
            lg = _dot(ck, qt) + am2
            if near:
                lg = lg + bias_ref[jnp.minimum((t0 - k0) // QBLK, NEAR_TILES), pr]
            m_old = m_all[pr:pr + 1, :]
            m_new = jnp.maximum(m_old, jnp.max(lg, axis=0, keepdims=True))
            p = jnp.exp2((lg - m_new).astype(BF16))
            acc_ref[pr] = acc_ref[pr] * jnp.exp2(m_old - m_new) + _dot(ckt, p)
            ms.append(m_new)
        return jnp.concatenate(ms, axis=0)

    m_all = jnp.full((NP, 2 * QBLK), NEG_BIG, F32)
    m_all = lax.fori_loop(0, nfar // 2, lambda c, m: tile(False, 2 * c + 1, tile(False, 2 * c, m)), m_all)
    m_all = lax.fori_loop(2 * (nfar // 2), nfar, functools.partial(tile, False), m_all)
    m_all = lax.fori_loop(nfar, nkt, functools.partial(tile, True), m_all)
    for pr in range(NP):
        a = acc_ref[pr]
        olat = (a[:KV_LORA] / a[KV_LORA:KV_LORA + 1]).astype(BF16)
        ot_ref[pr * 2 * AT_DH:(pr + 1) * 2 * AT_DH, :] = jnp.concatenate(
            [_dot(wuvt_ref[2 * pr], olat[:, :QBLK]), _dot(wuvt_ref[2 * pr + 1], olat[:, QBLK:])], axis=0)
    out_ref[...] = h_ref[...] + _dot(ot_ref[...].T.astype(BF16), wo_ref[...])


def _at_core(qlat, qit, wit, ki, ckv, ckvt, bias, wuvt, wo, h, B, L, topk):
    T, D = h.shape
    nblk = L // QBLK
    nkt = L // KEY_TILE
    H = AT_HEADS
    assert L % (2 * KEY_TILE) == 0
    return pl.pallas_call(
        functools.partial(_at_core_kernel, topk),
        grid=(B, nblk),
        in_specs=[pl.BlockSpec((None, H, KV_LORA, QBLK), lambda b, q: (b * nblk + q, 0, 0, 0)),
                  pl.BlockSpec((None, IDX_HEADS * IDX_DIM, QBLK), lambda b, q: (b * nblk + q, 0, 0)),
                  pl.BlockSpec((None, IDX_HEADS, QBLK), lambda b, q: (b * nblk + q, 0, 0)),
                  pl.BlockSpec((L, IDX_DIM), lambda b, q: (b, 0)),
                  pl.BlockSpec((L, KV_LORA), lambda b, q: (b, 0)),
                  pl.BlockSpec((nkt, KV_LORA + ONES_ROWS, KEY_TILE), lambda b, q: (b, 0, 0)),
                  _full(bias.shape, single=True), _full(wuvt.shape, single=True),
                  _full(wo.shape, single=True),
                  pl.BlockSpec((QBLK, D), lambda b, q: (b * nblk + q, 0))],
        out_specs=pl.BlockSpec((QBLK, D), lambda b, q: (b * nblk + q, 0)),
        out_shape=jax.ShapeDtypeStruct((T, D), F32),
        scratch_shapes=[pltpu.VMEM((L, QBLK), I32),
                        pltpu.VMEM((32, nkt, 8, QBLK), I32),
                        pltpu.VMEM((nkt, KEY_TILE, QBLK), F32),
                        pltpu.VMEM((H // 2, KV_LORA + ONES_ROWS, 2 * QBLK), F32),
                        pltpu.VMEM((H * AT_DH, QBLK), F32)],
        compiler_params=_cparams(("arbitrary", "arbitrary")),
        name="at_core",
    )(qlat, qit, wit, ki, ckv, ckvt, bias, wuvt, wo, h)


def _rel_bucket(n):
    max_exact = REL_BUCKETS // 2
    nf = jnp.maximum(n, 1).astype(F32)
    large = max_exact + (jnp.log(nf / max_exact) / math.log(REL_MAX_DIST / max_exact)
                         * (REL_BUCKETS - max_exact)).astype(I32)
    large = jnp.minimum(large, REL_BUCKETS - 1)
    return jnp.where(n < max_exact, n, large)


def _bias_tiles(rel_bias):
    H = rel_bias.shape[1]
    span = KEY_TILE + QBLK - 1
    m = jnp.arange(NEAR_TILES * QBLK + span, dtype=I32)
    tab = rel_bias[_rel_bucket(jnp.maximum(m - (KEY_TILE - 1), 0))] - rel_bias[REL_BUCKETS - 1]
    tiles = []
    for d in range(NEAR_TILES):
        u = tab[d * QBLK:d * QBLK + span][::-1]
        r = jnp.tile(u, (KEY_TILE + 1, 1))[:KEY_TILE * (span + 1)].reshape(KEY_TILE, span + 1, H)
        tiles.append(r[:, :QBLK][:, ::-1])
    tiles.append(jnp.zeros_like(tiles[0]))
    t = jnp.stack(tiles).astype(F32) * LOG2E
    t = jnp.transpose(t, (0, 3, 1, 2))
    return jnp.concatenate([t[:, 0::2], t[:, 1::2]], axis=-1)


def kernel(x, p, hg_norm, hg_w_in, hg_lb, hg_onorm, hg_w_out, at_norm, at_w_in, at_q_norm,
           at_kv_norm, at_w_uq, at_w_uk, at_w_uv, at_w_qidx, at_w_out, rel_bias, ff_norm, ff_w_up,
           ff_conv_w, ff_conv_b, ff_w_down, ple_norm, ple_w_gate, ple_w_proj, final_norm):
    B, L, D = x.shape
    T = B * L
    depth = p.shape[0]
    row = lambda a: a.reshape(1, -1)
    h = x.reshape(T, D)
    for i in range(depth):
        j = i // 2
        if i % 2 == 0:
            q, k, v, gs, lf = _hg_in(h, row(hg_norm[j]), hg_lb, hg_w_in[j].astype(BF16), i)
            h = _hg_rec(q, k, v, gs, lf, h, row(hg_onorm[j]), hg_w_out[j].astype(BF16), B, L)
        else:
            w_in = at_w_in[j].astype(BF16)
            o1, o2, o3 = Q_LORA, Q_LORA + KV_LORA, Q_LORA + KV_LORA + IDX_DIM
            wuk = at_w_uk[j].astype(BF16)
            z = jnp.zeros_like(wuk[0::2])
            wuk_bd = jnp.concatenate([jnp.concatenate([wuk[0::2], z], axis=2),
                                      jnp.concatenate([z, wuk[1::2]], axis=2)], axis=1)
            wuvt = jnp.transpose(at_w_uv[j].astype(BF16), (0, 2, 1))
            qlat, qit, wit, ki, ckv, ckvt = _at_in(
                h, row(at_norm[j]), w_in[:, :o1], w_in[:, o1:o2], w_in[:, o2:o3], w_in[:, o3:].T,
                row(at_q_norm[j]), row(at_kv_norm[j]), at_w_uq[j].astype(BF16),
                at_w_qidx[j].astype(BF16).T, jnp.transpose(wuk_bd, (0, 2, 1)))
            topk = max(1, min(TOPK_MAX, L // 4))
            h = _at_core(qlat, qit, wit, ki, ckv, ckvt, _bias_tiles(rel_bias), wuvt,
                         at_w_out[j].astype(BF16), h, B, L, topk)
        h = _ffn_ple(h, row(ff_norm[i]), ff_w_up[i].astype(BF16), ff_conv_w[i], ff_conv_b[i],
                     ff_w_down[i].astype(BF16), p[i].reshape(T, PLE_DIM), row(ple_norm[i]),
                     ple_w_gate[i].astype(BF16), ple_w_proj[i].astype(BF16), row(final_norm),
                     i == depth - 1, L)
    return h.reshape(B, L, D)
```

```python
import functools
import math

import jax
import jax.numpy as jnp
import numpy as np
from jax import lax
from jax.experimental import pallas as pl
from jax.experimental.pallas import tpu as pltpu

F32 = jnp.float32
BF16 = jnp.bfloat16
I32 = jnp.int32

D_MODEL = 1024
HG_HEADS = 8
HG_DK = D_MODEL // HG_HEADS
AT_HEADS = 16
AT_DH = 64
Q_LORA = 384
KV_LORA = 256
IDX_HEADS = 8
IDX_DIM = 64
TOPK_MAX = 256
QBLK = 128
REL_BUCKETS = 32
REL_MAX_DIST = 128
PLE_DIM = 256
EPS = 1e-6

VMEM_LIMIT_V7X = 56 * 1024 * 1024
SUB = 16
KEY_TILE = 256
NEAR_TILES = (KEY_TILE + REL_MAX_DIST) // QBLK
NEG_BIG = -1e30
LOG2E = 1.4426950408889634
ONES_ROWS = 16
INT_MIN = -(2 ** 31)


def _cparams(sem):
    return pltpu.CompilerParams(dimension_semantics=sem, vmem_limit_bytes=VMEM_LIMIT_V7X)


def _rms(x, g):
    return x * lax.rsqrt(jnp.mean(x * x, axis=-1, keepdims=True) + EPS) * g


def _dot(a, b):
    return jnp.dot(a, b, preferred_element_type=F32)


def _dot_nt(a, b):
    return lax.dot_general(a, b, (((1,), (1,)), ((), ())), preferred_element_type=F32)


def _dot_tn(a, b):
    return lax.dot_general(a, b, (((0,), (0,)), ((), ())), preferred_element_type=F32)


def _full(shape, single=False):
    n = len(shape)
    if single:
        return pl.BlockSpec(shape, lambda *_: (0,) * n, pipeline_mode=pl.Buffered(1))
    return pl.BlockSpec(shape, lambda *_: (0,) * n)


def _hg_in_kernel(layer, x_ref, g_ref, lbp_ref, w_ref, q_ref, k_ref, v_ref, gs_ref, lf_ref):
    D = D_MODEL
    xn = _rms(x_ref[...], g_ref[...]).astype(BF16)
    lbp = lbp_ref[...]
    e = jnp.exp(lbp - jnp.max(lbp, axis=0, keepdims=True))
    lb = jnp.sum(e[: layer + 1], axis=0, keepdims=True) / jnp.sum(e, axis=0, keepdims=True)
    q = _dot(xn, w_ref[:, 0:D])
    q_ref[...] = (q * jax.nn.sigmoid(q)).astype(BF16)
    f = lb + (1.0 - lb) * jax.nn.sigmoid(_dot(xn, w_ref[:, D:2 * D]))
    lf_ref[...] = jnp.log(f) * LOG2E
    k_ref[...] = (1.0 - f).astype(BF16)
    v_ref[...] = _dot(xn, w_ref[:, 2 * D:3 * D]).astype(BF16)
    g = _dot(xn, w_ref[:, 3 * D:4 * D])
    gs_ref[...] = (g * jax.nn.sigmoid(g)).astype(BF16)


def _hg_in(x, g, lbp, w, layer, tm=512):
    T, D = x.shape
    tok = pl.BlockSpec((tm, D), lambda i: (i, 0))
    bf = jax.ShapeDtypeStruct((T, D), BF16)
    return pl.pallas_call(
        functools.partial(_hg_in_kernel, layer),
        grid=(T // tm,),
        in_specs=[tok, _full((1, D)), _full(lbp.shape), _full(w.shape)],
        out_specs=[tok] * 5,
        out_shape=[bf, bf, bf, bf, jax.ShapeDtypeStruct((T, D), F32)],
        compiler_params=_cparams(("parallel",)),
        name="hg_in",
    )(x, g, lbp, w)


def _hg_rec_kernel(lt, q_ref, k_ref, v_ref, gs_ref, lf_ref, x_ref, on_ref, wo_ref, tri_ref,
                   out_ref, st_ref, b_ref, o_ref):
    H, DK = HG_HEADS, HG_DK

    @pl.when(pl.program_id(1) == 0)
    def _():
        st_ref[...] = jnp.zeros_like(st_ref)

    lf = lf_ref[...]
    hi = lf.astype(BF16)
    r1 = lf - hi.astype(F32)
    mid = r1.astype(BF16)
    lo = (r1 - mid.astype(F32)).astype(BF16)
    tri = tri_ref[...]
    b_ref[...] = _dot(tri, hi) + _dot(tri, mid) + _dot(tri, lo)

    HALF = SUB // 2
    row_h = lax.broadcasted_iota(I32, (HALF, DK), 0)

    def sub_chunk(c, carry):
        r0 = pl.multiple_of(c * SUB, SUB)
        for h in range(H):
            ls = slice(h * DK, (h + 1) * DK)
            b = b_ref[pl.ds(r0, SUB), ls]
            q = q_ref[pl.ds(r0, SUB), ls].astype(F32)
            k = k_ref[pl.ds(r0, SUB), ls].astype(F32)
            v = v_ref[pl.ds(r0, SUB), ls].astype(F32)
            be = b[SUB - 1:SUB, :]
            st = st_ref[h]
            o = _dot_nt((q * jnp.exp2(b)).astype(BF16), st.astype(BF16))
            o_top = jnp.zeros((HALF, DK), F32)
            o_bot = jnp.zeros((HALF, DK), F32)
            for s in range(SUB):
                bs, ks, vs = b[s:s + 1, :], k[s:s + 1, :], v[s:s + 1, :]
                if s < HALF:
                    e = jnp.exp2(jnp.where(row_h >= s, b[:HALF] - bs, NEG_BIG))
                    o_top = o_top + jnp.sum(q[:HALF] * e * ks, axis=-1, keepdims=True) * vs
                    e = jnp.exp2(b[HALF:] - bs)
                else:
                    e = jnp.exp2(jnp.where(row_h >= s - HALF, b[HALF:] - bs, NEG_BIG))
                o_bot = o_bot + jnp.sum(q[HALF:] * e * ks, axis=-1, keepdims=True) * vs
            o_ref[pl.ds(r0, SUB), ls] = o + jnp.concatenate([o_top, o_bot], axis=0)
            kd = (k * jnp.exp2(be - b)).astype(BF16)
            st_ref[h] = st * jnp.exp2(be) + _dot_tn(v.astype(BF16), kd)
        return carry

    lax.fori_loop(0, lt // SUB, sub_chunk, 0)

    on = on_ref[...]
    for h in range(H):
        ls = slice(h * DK, (h + 1) * DK)
        oh = _rms(o_ref[:, ls], on[:, ls]) * gs_ref[:, ls].astype(F32)
        o_ref[:, ls] = oh
    out_ref[...] = x_ref[...] + _dot(o_ref[...].astype(BF16), wo_ref[...])


def _hg_rec(q, k, v, gs, lf, x, onorm, wo, B, L, lt=256):
    T, D = x.shape
    nl = L // lt
    tok = pl.BlockSpec((lt, D), lambda b, j: (b * nl + j, 0))
    tri = (np.arange(lt)[:, None] >= np.arange(lt)[None, :]) & (
        np.arange(lt)[:, None] // SUB == np.arange(lt)[None, :] // SUB)
    tri = jnp.asarray(tri, BF16)
    return pl.pallas_call(
        functools.partial(_hg_rec_kernel, lt),
        grid=(B, nl),
        in_specs=[tok] * 6 + [_full((1, D)), _full((D, D)), _full((lt, lt))],
        out_specs=tok,
        out_shape=jax.ShapeDtypeStruct((T, D), F32),
        scratch_shapes=[pltpu.VMEM((HG_HEADS, HG_DK, HG_DK), F32),
                        pltpu.VMEM((lt, D), F32),
                        pltpu.VMEM((lt, D), F32)],
        compiler_params=_cparams(("arbitrary", "arbitrary")),
        name="hg_rec",
    )(q, k, v, gs, lf, x, onorm, wo, tri)


def _ffn_kernel(nseq, nf, final, h_ref, g_ref, wg_ref, wv_ref, cwg_ref, cwv_ref, cbg_ref, cbv_ref,
                wd_ref, p_ref, pg_ref, pwg_ref, pwp_ref, fn_ref, out_ref, xn_ref, cg_ref, cv_ref):
    i, j = pl.program_id(0), pl.program_id(1)

    @pl.when(j == 0)
    def _():
        xn_ref[...] = _rms(h_ref[...], g_ref[...]).astype(BF16)
        out_ref[...] = h_ref[...]

    seq_start = (i % nseq) == 0
    xn = xn_ref[...]

    def conv(u, w_ref, b_ref, carry_ref):
        fc = u.shape[1]
        carry = jnp.where(seq_start, 0.0, carry_ref[j])
        carry_ref[j] = u[u.shape[0] - 8:, :]
        r1 = pltpu.roll(u, 1, 0)
        r2 = pltpu.roll(u, 2, 0)
        rows = lax.broadcasted_iota(I32, (8, fc), 0)
        top1 = jnp.where(rows < 1, pltpu.roll(carry, 1, 0), r1[0:8])
        top2 = jnp.where(rows < 2, pltpu.roll(carry, 2, 0), r2[0:8])
        p1 = jnp.concatenate([top1, r1[8:]], axis=0)
        p2 = jnp.concatenate([top2, r2[8:]], axis=0)
        w = w_ref[0]
        return u * w[2:3, :] + p1 * w[1:2, :] + p2 * w[0:1, :] + b_ref[0]

    ug = conv(_dot(xn, wg_ref[...]), cwg_ref, cbg_ref, cg_ref)
    uv = conv(_dot(xn, wv_ref[...]), cwv_ref, cbv_ref, cv_ref)
    act = (ug * jax.nn.sigmoid(ug) * uv).astype(BF16)
    out_ref[...] += _dot(act, wd_ref[...])

    @pl.when(j == nf - 1)
    def _():
        h = out_ref[...]
        gate = jax.nn.sigmoid(_dot(_rms(h, pg_ref[...]).astype(BF16), pwg_ref[...]))
        h = h + gate * _dot(p_ref[...].astype(BF16), pwp_ref[...])
        if final:
            h = _rms(h, fn_ref[...])
        out_ref[...] = h


def _ffn_ple(h, g, w_up, conv_w, conv_b, w_down, p, pg, pwg, pwp, fnorm, final, L, tm=512, nf=2):
    T, D = h.shape
    F = w_down.shape[0]
    fc = F // nf
    assert fc % 128 == 0 and L % tm == 0
    cw = jnp.pad(conv_w, ((0, 5), (0, 0)))
    cwg = cw[:, :F].reshape(8, nf, fc).transpose(1, 0, 2)
    cwv = cw[:, F:].reshape(8, nf, fc).transpose(1, 0, 2)
    cbg = conv_b[:F].reshape(nf, 1, fc)
    cbv = conv_b[F:].reshape(nf, 1, fc)
    tok = pl.BlockSpec((tm, D), lambda i, j: (i, 0))
    return pl.pallas_call(
        functools.partial(_ffn_kernel, L // tm, nf, final),
        grid=(T // tm, nf),
        in_specs=[tok, _full((1, D)),
                  pl.BlockSpec((D, fc), lambda i, j: (0, j)),
                  pl.BlockSpec((D, fc), lambda i, j: (0, nf + j)),
                  pl.BlockSpec((1, 8, fc), lambda i, j: (j, 0, 0)),
                  pl.BlockSpec((1, 8, fc), lambda i, j: (j, 0, 0)),
                  pl.BlockSpec((1, 1, fc), lambda i, j: (j, 0, 0)),
                  pl.BlockSpec((1, 1, fc), lambda i, j: (j, 0, 0)),
                  pl.BlockSpec((fc, D), lambda i, j: (j, 0)),
                  pl.BlockSpec((tm, PLE_DIM), lambda i, j: (i, 0)), _full((1, D)),
                  _full((D, D), single=True), _full((PLE_DIM, D), single=True), _full((1, D))],
        out_specs=tok,
        out_shape=jax.ShapeDtypeStruct((T, D), F32),
        scratch_shapes=[pltpu.VMEM((tm, D), BF16),
                        pltpu.VMEM((nf, 8, fc), F32),
                        pltpu.VMEM((nf, 8, fc), F32)],
        compiler_params=_cparams(("arbitrary", "arbitrary")),
        name="conv_ffn_ple",
    )(h, g, w_up, w_up, cwg, cwv, cbg, cbv, w_down, p, pg, pwg, pwp, fnorm)


def _at_in_kernel(h_ref, g_ref, wcq_ref, wckv_ref, wki_ref, wwi_ref, qn_ref, kvn_ref,
                  wuq_ref, wqi_ref, wuk_ref,
                  qlat_ref, qit_ref, wit_ref, ki_ref, ckv_ref, ckvt_ref):
    tm = h_ref.shape[0]
    xn = _rms(h_ref[...], g_ref[...]).astype(BF16)
    cq = _rms(_dot(xn, wcq_ref[...]), qn_ref[...]).astype(BF16)
    ckv = _rms(_dot(xn, wckv_ref[...]), kvn_ref[...])
    ckv_ref[...] = ckv.astype(BF16)
    for c in range(tm // KEY_TILE):
        ckvt_ref[c] = jnp.concatenate([ckv[c * KEY_TILE:(c + 1) * KEY_TILE, :].T.astype(BF16),
                                       jnp.ones((ONES_ROWS, KEY_TILE), BF16)], axis=0)
    ki_ref[...] = _dot(xn, wki_ref[...]).astype(BF16)
    nq = tm // QBLK
    wit = _dot_nt(wwi_ref[...], xn) * (IDX_HEADS ** -0.5 * IDX_DIM ** -0.5)
    qit = _dot_nt(wqi_ref[...], cq).astype(BF16)
    for c in range(nq):
        wit_ref[c] = wit[:, c * QBLK:(c + 1) * QBLK]
        qit_ref[c] = qit[:, c * QBLK:(c + 1) * QBLK]
    qn = _dot(cq, wuq_ref[...]).astype(BF16)
    for pr in range(AT_HEADS // 2):
        ql = (_dot_nt(wuk_ref[pr], qn[:, pr * 128:(pr + 1) * 128])
              * (AT_DH ** -0.5 * LOG2E)).astype(BF16)
        for c in range(nq):
            qlat_ref[c, 2 * pr] = ql[:KV_LORA, c * QBLK:(c + 1) * QBLK]
            qlat_ref[c, 2 * pr + 1] = ql[KV_LORA:, c * QBLK:(c + 1) * QBLK]


def _at_in(h, g, wcq, wckv, wki, wwi_t, qn, kvn, wuq, wqi_t, wuk_bd_t, tm=512):
    T, D = h.shape
    nk = tm // KEY_TILE
    ins = [h, g, wcq, wckv, wki, wwi_t, qn, kvn, wuq, wqi_t, wuk_bd_t]
    in_specs = [pl.BlockSpec((tm, D), lambda i: (i, 0))] + [_full(a.shape) for a in ins[1:]]
    return pl.pallas_call(
        _at_in_kernel,
        grid=(T // tm,),
        in_specs=in_specs,
        out_specs=[pl.BlockSpec((tm // QBLK, AT_HEADS, KV_LORA, QBLK), lambda i: (i, 0, 0, 0)),
                   pl.BlockSpec((tm // QBLK, IDX_HEADS * IDX_DIM, QBLK), lambda i: (i, 0, 0)),
                   pl.BlockSpec((tm // QBLK, IDX_HEADS, QBLK), lambda i: (i, 0, 0)),
                   pl.BlockSpec((tm, IDX_DIM), lambda i: (i, 0)),
                   pl.BlockSpec((tm, KV_LORA), lambda i: (i, 0)),
                   pl.BlockSpec((nk, KV_LORA + ONES_ROWS, KEY_TILE), lambda i: (i, 0, 0))],
        out_shape=[jax.ShapeDtypeStruct((T // QBLK, AT_HEADS, KV_LORA, QBLK), BF16),
                   jax.ShapeDtypeStruct((T // QBLK, IDX_HEADS * IDX_DIM, QBLK), BF16),
                   jax.ShapeDtypeStruct((T // QBLK, IDX_HEADS, QBLK), F32),
                   jax.ShapeDtypeStruct((T, IDX_DIM), BF16),
                   jax.ShapeDtypeStruct((T, KV_LORA), BF16),
                   jax.ShapeDtypeStruct((T // KEY_TILE, KV_LORA + ONES_ROWS, KEY_TILE), BF16)],
        compiler_params=_cparams(("parallel",)),
        name="at_in",
    )(*ins)


def _at_core_kernel(topk, qlat_ref, qit_ref, wit_ref, ki_ref, ckv_ref, ckvt_ref, bias_ref,
                    wuvt_ref, wo_ref, h_ref, out_ref, ik_ref, planes_ref, mask_ref, acc_ref, ot_ref):
    H, TK = AT_HEADS, KEY_TILE
    assert TK == 32 * 8
    qb = pl.program_id(1)
    t0 = qb * QBLK
    nkt = qb // (TK // QBLK) + 1
    lane_q = lax.broadcasted_iota(I32, (TK, QBLK), 1)
    row_k = lax.broadcasted_iota(I32, (TK, QBLK), 0)

    wit = wit_ref[...]

    def score_tile(kt, carry):
        k0 = pl.multiple_of(kt * TK, TK)
        kk = ki_ref[pl.ds(k0, TK), :]
        acc = jnp.zeros((TK, 2 * QBLK), F32)
        for pr in range(IDX_HEADS // 2):
            qa = qit_ref[(2 * pr) * IDX_DIM:(2 * pr + 1) * IDX_DIM, :]
            qb_ = qit_ref[(2 * pr + 1) * IDX_DIM:(2 * pr + 2) * IDX_DIM, :]
            rel = jnp.maximum(_dot(kk, jnp.concatenate([qa, qb_], axis=1)), 0.0)
            w2 = jnp.concatenate([wit[2 * pr:2 * pr + 1, :], wit[2 * pr + 1:2 * pr + 2, :]], axis=1)
            acc = acc + rel * w2
        score = acc[:, :QBLK] + acc[:, QBLK:] + 0.0
        bits = pltpu.bitcast(score, I32)
        ikey = bits ^ ((bits >> 31) & 0x7FFFFFFF)
        causal = (k0 + row_k) <= (t0 + lane_q)
        xk = jnp.where(causal, ikey, INT_MIN)
        ik_ref[pl.ds(k0, TK), :] = xk
        u = (xk ^ INT_MIN).reshape(TK // 8, 8, QBLK)
        a = [u[v] for v in range(32)]
        m, j = 0x0000FFFF, 16
        while j:
            k = 0
            while k < 32:
                t = (a[k] ^ lax.shift_right_logical(a[k + j], jnp.full_like(a[k], j))) & m
                a[k] = a[k] ^ t
                a[k + j] = a[k + j] ^ (t << j)
                k = (k + j + 1) & ~j
            j >>= 1
            m = m ^ ((m << j) & 0xFFFFFFFF)
        for i in range(32):
            planes_ref[i, pl.ds(kt, 1)] = a[i][None]
        return carry

    @pl.when((pl.program_id(0) == 0) & (qb == 0))
    def _():
        planes_ref[...] = jnp.zeros_like(planes_ref)

    CT = 2 * TK
    nct = (nkt + 1) // 2
    lax.fori_loop(0, nct, lambda c, x: score_tile(2 * c + 1, score_tile(2 * c, x)), 0)

    row_c = lax.broadcasted_iota(I32, (CT, QBLK), 0)

    def count(pred):
        def body(ct, c):
            k0 = pl.multiple_of(ct * CT, CT)
            hit = jnp.where(pred(ik_ref[pl.ds(k0, CT), :], k0 + row_c), 1, 0)
            return c + jnp.sum(hit.reshape(CT // 8, 8, QBLK), axis=0)
        c = lax.fori_loop(0, nct, body, jnp.zeros((8, QBLK), I32))
        return jnp.sum(c, axis=0, keepdims=True)

    NT = planes_ref.shape[1]

    def bit_pass(i, st):
        alive, rem, tau_u = st
        t = alive & planes_ref[i]
        c = jnp.sum(jnp.sum(lax.population_count(t), axis=0), axis=0, keepdims=True)
        take = c >= rem
        return (jnp.where(take, t, alive ^ t), jnp.where(take, rem, rem - c),
                jnp.where(take, tau_u | (1 << (31 - i)), tau_u))

    alive0 = jnp.where(lax.broadcasted_iota(I32, (NT, 8, QBLK), 0) < nkt, -1, 0)
    ties, need, tau_u = lax.fori_loop(0, 32, bit_pass, (alive0, jnp.full((1, QBLK), topk, I32),
                                                        jnp.zeros((1, QBLK), I32)))
    tau = tau_u ^ INT_MIN
    n_ties = jnp.sum(jnp.sum(lax.population_count(ties), axis=0), axis=0, keepdims=True)

    idx_bits = ik_ref.shape[0].bit_length()

    def tie_search(_):
        def step(i, jc):
            cand = jc + (1 << (idx_bits - 1 - i))
            n = count(lambda x, kidx: (x == tau) & (kidx < cand))
            return jnp.where(n <= need, cand, jc)
        return lax.fori_loop(0, idx_bits, step, jnp.zeros((1, QBLK), I32))

    jcut = lax.cond(jnp.max(n_ties - need) > 0, tie_search,
                    lambda _: jnp.full((1, QBLK), 2 ** idx_bits, I32), 0)

    def mask_tile(kt, carry):
        k0 = pl.multiple_of(kt * TK, TK)
        x = ik_ref[pl.ds(k0, TK), :]
        kidx = k0 + row_k
        sel = ((x > tau) | ((x == tau) & (kidx < jcut))) & (kidx <= (t0 + lane_q))
        mask_ref[kt] = jnp.where(sel, 0.0, NEG_BIG)
        return carry

    lax.fori_loop(0, nkt, mask_tile, 0)

    near_key_tiles = -(-NEAR_TILES * QBLK // TK)
    nfar = jnp.maximum(nkt - near_key_tiles, 0)

    NP = H // 2
    acc_ref[...] = jnp.zeros_like(acc_ref)

    def tile(near, kt, m_all):
        k0 = pl.multiple_of(kt * TK, TK)
        am = mask_ref[kt]
        am2 = jnp.concatenate([am, am], axis=1)
        ck = ckv_ref[pl.ds(k0, TK), :]
        ckt = ckvt_ref[kt]
        ms = []
        for pr in range(NP):
            qt = jnp.concatenate([qlat_ref[2 * pr], qlat_ref[2 * pr + 1]], axis=1)
            lg = _dot(ck, qt) + am2
            if near:
                lg = lg + bias_ref[jnp.minimum((t0 - k0) // QBLK, NEAR_TILES), pr]
            m_old = m_all[pr:pr + 1, :]
            m_new = jnp.maximum(m_old, jnp.max(lg, axis=0, keepdims=True))
            p = jnp.exp2((lg - m_new).astype(BF16))
            acc_ref[pr] = acc_ref[pr] * jnp.exp2(m_old - m_new) + _dot(ckt, p)
            ms.append(m_new)
        return jnp.concatenate(ms, axis=0)

    m_all = jnp.full((NP, 2 * QBLK), NEG_BIG, F32)
    m_all = lax.fori_loop(0, nfar // 2, lambda c, m: tile(False, 2 * c + 1, tile(False, 2 * c, m)), m_all)
    m_all = lax.fori_loop(2 * (nfar // 2), nfar, functools.partial(tile, False), m_all)
    m_all = lax.fori_loop(nfar, nkt, functools.partial(tile, True), m_all)
    for pr in range(NP):
        a = acc_ref[pr]
        olat = (a[:KV_LORA] / a[KV_LORA:KV_LORA + 1]).astype(BF16)
        ot_ref[pr * 2 * AT_DH:(pr + 1) * 2 * AT_DH, :] = jnp.concatenate(
            [_dot(wuvt_ref[2 * pr], olat[:, :QBLK]), _dot(wuvt_ref[2 * pr + 1], olat[:, QBLK:])], axis=0)
    out_ref[...] = h_ref[...] + _dot(ot_ref[...].T.astype(BF16), wo_ref[...])


def _at_core(qlat, qit, wit, ki, ckv, ckvt, bias, wuvt, wo, h, B, L, topk):
    T, D = h.shape
    nblk = L // QBLK
    nkt = L // KEY_TILE
    H = AT_HEADS
    assert L % (2 * KEY_TILE) == 0
    return pl.pallas_call(
        functools.partial(_at_core_kernel, topk),
        grid=(B, nblk),
        in_specs=[pl.BlockSpec((None, H, KV_LORA, QBLK), lambda b, q: (b * nblk + q, 0, 0, 0)),
                  pl.BlockSpec((None, IDX_HEADS * IDX_DIM, QBLK), lambda b, q: (b * nblk + q, 0, 0)),
                  pl.BlockSpec((None, IDX_HEADS, QBLK), lambda b, q: (b * nblk + q, 0, 0)),
                  pl.BlockSpec((L, IDX_DIM), lambda b, q: (b, 0)),
                  pl.BlockSpec((L, KV_LORA), lambda b, q: (b, 0)),
                  pl.BlockSpec((nkt, KV_LORA + ONES_ROWS, KEY_TILE), lambda b, q: (b, 0, 0)),
                  _full(bias.shape, single=True), _full(wuvt.shape, single=True),
                  _full(wo.shape, single=True),
                  pl.BlockSpec((QBLK, D), lambda b, q: (b * nblk + q, 0))],
        out_specs=pl.BlockSpec((QBLK, D), lambda b, q: (b * nblk + q, 0)),
        out_shape=jax.ShapeDtypeStruct((T, D), F32),
        scratch_shapes=[pltpu.VMEM((L, QBLK), I32),
                        pltpu.VMEM((32, nkt, 8, QBLK), I32),
                        pltpu.VMEM((nkt, KEY_TILE, QBLK), F32),
                        pltpu.VMEM((H // 2, KV_LORA + ONES_ROWS, 2 * QBLK), F32),
                        pltpu.VMEM((H * AT_DH, QBLK), F32)],
        compiler_params=_cparams(("arbitrary", "arbitrary")),
        name="at_core",
    )(qlat, qit, wit, ki, ckv, ckvt, bias, wuvt, wo, h)


def _rel_bucket(n):
    max_exact = REL_BUCKETS // 2
    nf = jnp.maximum(n, 1).astype(F32)
    large = max_exact + (jnp.log(nf / max_exact) / math.log(REL_MAX_DIST / max_exact)
                         * (REL_BUCKETS - max_exact)).astype(I32)
    large = jnp.minimum(large, REL_BUCKETS - 1)
    return jnp.where(n < max_exact, n, large)


def _bias_tiles(rel_bias):
    H = rel_bias.shape[1]
    span = KEY_TILE + QBLK - 1
    m = jnp.arange(NEAR_TILES * QBLK + span, dtype=I32)
    tab = rel_bias[_rel_bucket(jnp.maximum(m - (KEY_TILE - 1), 0))] - rel_bias[REL_BUCKETS - 1]
    tiles = []
    for d in range(NEAR_TILES):
        u = tab[d * QBLK:d * QBLK + span][::-1]
        r = jnp.tile(u, (KEY_TILE + 1, 1))[:KEY_TILE * (span + 1)].reshape(KEY_TILE, span + 1, H)
        tiles.append(r[:, :QBLK][:, ::-1])
    tiles.append(jnp.zeros_like(tiles[0]))
    t = jnp.stack(tiles).astype(F32) * LOG2E
    t = jnp.transpose(t, (0, 3, 1, 2))
    return jnp.concatenate([t[:, 0::2], t[:, 1::2]], axis=-1)


def kernel(x, p, hg_norm, hg_w_in, hg_lb, hg_onorm, hg_w_out, at_norm, at_w_in, at_q_norm,
           at_kv_norm, at_w_uq, at_w_uk, at_w_uv, at_w_qidx, at_w_out, rel_bias, ff_norm, ff_w_up,
           ff_conv_w, ff_conv_b, ff_w_down, ple_norm, ple_w_gate, ple_w_proj, final_norm):
    B, L, D = x.shape
    T = B * L
    depth = p.shape[0]
    row = lambda a: a.reshape(1, -1)
    h = x.reshape(T, D)
    for i in range(depth):
        j = i // 2
        if i % 2 == 0:
            q, k, v, gs, lf = _hg_in(h, row(hg_norm[j]), hg_lb, hg_w_in[j].astype(BF16), i)
            h = _hg_rec(q, k, v, gs, lf, h, row(hg_onorm[j]), hg_w_out[j].astype(BF16), B, L)
        else:
            w_in = at_w_in[j].astype(BF16)
            o1, o2, o3 = Q_LORA, Q_LORA + KV_LORA, Q_LORA + KV_LORA + IDX_DIM
            wuk = at_w_uk[j].astype(BF16)
            z = jnp.zeros_like(wuk[0::2])
            wuk_bd = jnp.concatenate([jnp.concatenate([wuk[0::2], z], axis=2),
                                      jnp.concatenate([z, wuk[1::2]], axis=2)], axis=1)
            wuvt = jnp.transpose(at_w_uv[j].astype(BF16), (0, 2, 1))
            qlat, qit, wit, ki, ckv, ckvt = _at_in(
                h, row(at_norm[j]), w_in[:, :o1], w_in[:, o1:o2], w_in[:, o2:o3], w_in[:, o3:].T,
                row(at_q_norm[j]), row(at_kv_norm[j]), at_w_uq[j].astype(BF16),
                at_w_qidx[j].astype(BF16).T, jnp.transpose(wuk_bd, (0, 2, 1)))
            topk = max(1, min(TOPK_MAX, L // 4))
            h = _at_core(qlat, qit, wit, ki, ckv, ckvt, _bias_tiles(rel_bias), wuvt,
                         at_w_out[j].astype(BF16), h, B, L, topk)
        h = _ffn_ple(h, row(ff_norm[i]), ff_w_up[i].astype(BF16), ff_conv_w[i], ff_conv_b[i],
                     ff_w_down[i].astype(BF16), p[i].reshape(T, PLE_DIM), row(ple_norm[i]),
                     ple_w_gate[i].astype(BF16), ple_w_proj[i].astype(BF16), row(final_norm),
                     i == depth - 1, L)
    return h.reshape(B, L, D)
```

```python
import functools
import math

import jax
import jax.numpy as jnp
import numpy as np
from jax import lax
from jax.experimental import pallas as pl
from jax.experimental.pallas import tpu as pltpu

F32 = jnp.float32
BF16 = jnp.bfloat16
I32 = jnp.int32

D_MODEL = 1024
HG_HEADS = 8
HG_DK = D_MODEL // HG_HEADS
AT_HEADS = 16
AT_DH = 64
Q_LORA = 384
KV_LORA = 256
IDX_HEADS = 8
IDX_DIM = 64
TOPK_MAX = 256
QBLK = 128
REL_BUCKETS = 32
REL_MAX_DIST = 128
PLE_DIM = 256
EPS = 1e-6

VMEM_LIMIT_V7X = 56 * 1024 * 1024
SUB = 16
KEY_TILE = 256
NEAR_TILES = (KEY_TILE + REL_MAX_DIST) // QBLK
NEG_BIG = -1e30
LOG2E = 1.4426950408889634
ONES_ROWS = 16
INT_MIN = -(2 ** 31)


def _cparams(sem):
    return pltpu.CompilerParams(dimension_semantics=sem, vmem_limit_bytes=VMEM_LIMIT_V7X)


def _rms(x, g):
    return x * lax.rsqrt(jnp.mean(x * x, axis=-1, keepdims=True) + EPS) * g


def _dot(a, b):
    return jnp.dot(a, b, preferred_element_type=F32)


def _dot_nt(a, b):
    return lax.dot_general(a, b, (((1,), (1,)), ((), ())), preferred_element_type=F32)


def _dot_tn(a, b):
    return lax.dot_general(a, b, (((0,), (0,)), ((), ())), preferred_element_type=F32)


def _full(shape, single=False):
    n = len(shape)
    if single:
        return pl.BlockSpec(shape, lambda *_: (0,) * n, pipeline_mode=pl.Buffered(1))
    return pl.BlockSpec(shape, lambda *_: (0,) * n)


def _hg_in_kernel(layer, x_ref, g_ref, lbp_ref, w_ref, q_ref, k_ref, v_ref, gs_ref, lf_ref):
    D = D_MODEL
    xn = _rms(x_ref[...], g_ref[...]).astype(BF16)
    lbp = lbp_ref[...]
    e = jnp.exp(lbp - jnp.max(lbp, axis=0, keepdims=True))
    lb = jnp.sum(e[: layer + 1], axis=0, keepdims=True) / jnp.sum(e, axis=0, keepdims=True)
    q = _dot(xn, w_ref[:, 0:D])
    q_ref[...] = (q * jax.nn.sigmoid(q)).astype(BF16)
    f = lb + (1.0 - lb) * jax.nn.sigmoid(_dot(xn, w_ref[:, D:2 * D]))
    lf_ref[...] = jnp.log(f) * LOG2E
    k_ref[...] = (1.0 - f).astype(BF16)
    v_ref[...] = _dot(xn, w_ref[:, 2 * D:3 * D]).astype(BF16)
    g = _dot(xn, w_ref[:, 3 * D:4 * D])
    gs_ref[...] = (g * jax.nn.sigmoid(g)).astype(BF16)


def _hg_in(x, g, lbp, w, layer, tm=1024):
    T, D = x.shape
    tok = pl.BlockSpec((tm, D), lambda i: (i, 0))
    bf = jax.ShapeDtypeStruct((T, D), BF16)
    return pl.pallas_call(
        functools.partial(_hg_in_kernel, layer),
        grid=(T // tm,),
        in_specs=[tok, _full((1, D)), _full(lbp.shape), _full(w.shape)],
        out_specs=[tok] * 5,
        out_shape=[bf, bf, bf, bf, jax.ShapeDtypeStruct((T, D), F32)],
        compiler_params=_cparams(("parallel",)),
        name="hg_in",
    )(x, g, lbp, w)


def _hg_rec_kernel(lt, q_ref, k_ref, v_ref, gs_ref, lf_ref, x_ref, on_ref, wo_ref, tri_ref,
                   out_ref, st_ref, b_ref, o_ref):
    H, DK = HG_HEADS, HG_DK

    @pl.when(pl.program_id(1) == 0)
    def _():
        st_ref[...] = jnp.zeros_like(st_ref)

    lf = lf_ref[...]
    hi = lf.astype(BF16)
    r1 = lf - hi.astype(F32)
    mid = r1.astype(BF16)
    lo = (r1 - mid.astype(F32)).astype(BF16)
    tri = tri_ref[...]
    b_ref[...] = _dot(tri, hi) + _dot(tri, mid) + _dot(tri, lo)

    HALF = SUB // 2
    row_h = lax.broadcasted_iota(I32, (HALF, DK), 0)

    def sub_chunk(c, carry):
        r0 = pl.multiple_of(c * SUB, SUB)
        for h in range(H):
            ls = slice(h * DK, (h + 1) * DK)
            b = b_ref[pl.ds(r0, SUB), ls]
            q = q_ref[pl.ds(r0, SUB), ls].astype(F32)
            k = k_ref[pl.ds(r0, SUB), ls].astype(F32)
            v = v_ref[pl.ds(r0, SUB), ls].astype(F32)
            be = b[SUB - 1:SUB, :]
            st = st_ref[h]
            o = _dot_nt((q * jnp.exp2(b)).astype(BF16), st.astype(BF16))
            o_top = jnp.zeros((HALF, DK), F32)
            o_bot = jnp.zeros((HALF, DK), F32)
            for s in range(SUB):
                bs, ks, vs = b[s:s + 1, :], k[s:s + 1, :], v[s:s + 1, :]
                if s < HALF:
                    e = jnp.exp2(jnp.where(row_h >= s, b[:HALF] - bs, NEG_BIG))
                    o_top = o_top + jnp.sum(q[:HALF] * e * ks, axis=-1, keepdims=True) * vs
                    e = jnp.exp2(b[HALF:] - bs)
                else:
                    e = jnp.exp2(jnp.where(row_h >= s - HALF, b[HALF:] - bs, NEG_BIG))
                o_bot = o_bot + jnp.sum(q[HALF:] * e * ks, axis=-1, keepdims=True) * vs
            o_ref[pl.ds(r0, SUB), ls] = o + jnp.concatenate([o_top, o_bot], axis=0)
            kd = (k * jnp.exp2(be - b)).astype(BF16)
            st_ref[h] = st * jnp.exp2(be) + _dot_tn(v.astype(BF16), kd)
        return carry

    lax.fori_loop(0, lt // (2 * SUB), lambda c, x: sub_chunk(2 * c + 1, sub_chunk(2 * c, x)), 0)

    on = on_ref[...]
    for h in range(H):
        ls = slice(h * DK, (h + 1) * DK)
        oh = _rms(o_ref[:, ls], on[:, ls]) * gs_ref[:, ls].astype(F32)
        o_ref[:, ls] = oh
    out_ref[...] = x_ref[...] + _dot(o_ref[...].astype(BF16), wo_ref[...])


def _hg_rec(q, k, v, gs, lf, x, onorm, wo, B, L, lt=256):
    T, D = x.shape
    nl = L // lt
    tok = pl.BlockSpec((lt, D), lambda b, j: (b * nl + j, 0))
    tri = (np.arange(lt)[:, None] >= np.arange(lt)[None, :]) & (
        np.arange(lt)[:, None] // SUB == np.arange(lt)[None, :] // SUB)
    tri = jnp.asarray(tri, BF16)
    return pl.pallas_call(
        functools.partial(_hg_rec_kernel, lt),
        grid=(B, nl),
        in_specs=[tok] * 6 + [_full((1, D)), _full((D, D)), _full((lt, lt))],
        out_specs=tok,
        out_shape=jax.ShapeDtypeStruct((T, D), F32),
        scratch_shapes=[pltpu.VMEM((HG_HEADS, HG_DK, HG_DK), F32),
                        pltpu.VMEM((lt, D), F32),
                        pltpu.VMEM((lt, D), F32)],
        compiler_params=_cparams(("arbitrary", "arbitrary")),
        name="hg_rec",
    )(q, k, v, gs, lf, x, onorm, wo, tri)


def _ffn_kernel(nseq, nf, final, h_ref, g_ref, wg_ref, wv_ref, cwg_ref, cwv_ref, cbg_ref, cbv_ref,
                wd_ref, p_ref, pg_ref, pwg_ref, pwp_ref, fn_ref, out_ref, xn_ref, cg_ref, cv_ref):
    i, j = pl.program_id(0), pl.program_id(1)

    @pl.when(j == 0)
    def _():
        xn_ref[...] = _rms(h_ref[...], g_ref[...]).astype(BF16)
        out_ref[...] = h_ref[...]

    seq_start = (i % nseq) == 0
    xn = xn_ref[...]

    def conv(u, w_ref, b_ref, carry_ref):
        fc = u.shape[1]
        carry = jnp.where(seq_start, 0.0, carry_ref[j])
        carry_ref[j] = u[u.shape[0] - 8:, :]
        r1 = pltpu.roll(u, 1, 0)
        r2 = pltpu.roll(u, 2, 0)
        rows = lax.broadcasted_iota(I32, (8, fc), 0)
        top1 = jnp.where(rows < 1, pltpu.roll(carry, 1, 0), r1[0:8])
        top2 = jnp.where(rows < 2, pltpu.roll(carry, 2, 0), r2[0:8])
        p1 = jnp.concatenate([top1, r1[8:]], axis=0)
        p2 = jnp.concatenate([top2, r2[8:]], axis=0)
        w = w_ref[0]
        return u * w[2:3, :] + p1 * w[1:2, :] + p2 * w[0:1, :] + b_ref[0]

    ug = conv(_dot(xn, wg_ref[...]), cwg_ref, cbg_ref, cg_ref)
    uv = conv(_dot(xn, wv_ref[...]), cwv_ref, cbv_ref, cv_ref)
    act = (ug * jax.nn.sigmoid(ug) * uv).astype(BF16)
    out_ref[...] += _dot(act, wd_ref[...])

    @pl.when(j == nf - 1)
    def _():
        h = out_ref[...]
        gate = jax.nn.sigmoid(_dot(_rms(h, pg_ref[...]).astype(BF16), pwg_ref[...]))
        h = h + gate * _dot(p_ref[...].astype(BF16), pwp_ref[...])
        if final:
            h = _rms(h, fn_ref[...])
        out_ref[...] = h


def _ffn_ple(h, g, w_up, conv_w, conv_b, w_down, p, pg, pwg, pwp, fnorm, final, L, tm=512, nf=2):
    T, D = h.shape
    F = w_down.shape[0]
    fc = F // nf
    assert fc % 128 == 0 and L % tm == 0
    cw = jnp.pad(conv_w, ((0, 5), (0, 0)))
    cwg = cw[:, :F].reshape(8, nf, fc).transpose(1, 0, 2)
    cwv = cw[:, F:].reshape(8, nf, fc).transpose(1, 0, 2)
    cbg = conv_b[:F].reshape(nf, 1, fc)
    cbv = conv_b[F:].reshape(nf, 1, fc)
    tok = pl.BlockSpec((tm, D), lambda i, j: (i, 0))
    return pl.pallas_call(
        functools.partial(_ffn_kernel, L // tm, nf, final),
        grid=(T // tm, nf),
        in_specs=[tok, _full((1, D)),
                  pl.BlockSpec((D, fc), lambda i, j: (0, j)),
                  pl.BlockSpec((D, fc), lambda i, j: (0, nf + j)),
                  pl.BlockSpec((1, 8, fc), lambda i, j: (j, 0, 0)),
                  pl.BlockSpec((1, 8, fc), lambda i, j: (j, 0, 0)),
                  pl.BlockSpec((1, 1, fc), lambda i, j: (j, 0, 0)),
                  pl.BlockSpec((1, 1, fc), lambda i, j: (j, 0, 0)),
                  pl.BlockSpec((fc, D), lambda i, j: (j, 0)),
                  pl.BlockSpec((tm, PLE_DIM), lambda i, j: (i, 0)), _full((1, D)),
                  _full((D, D), single=True), _full((PLE_DIM, D), single=True), _full((1, D))],
        out_specs=tok,
        out_shape=jax.ShapeDtypeStruct((T, D), F32),
        scratch_shapes=[pltpu.VMEM((tm, D), BF16),
                        pltpu.VMEM((nf, 8, fc), F32),
                        pltpu.VMEM((nf, 8, fc), F32)],
        compiler_params=_cparams(("arbitrary", "arbitrary")),
        name="conv_ffn_ple",
    )(h, g, w_up, w_up, cwg, cwv, cbg, cbv, w_down, p, pg, pwg, pwp, fnorm)


def _at_in_kernel(h_ref, g_ref, wcq_ref, wckv_ref, wki_ref, wwi_ref, qn_ref, kvn_ref,
                  wuq_ref, wqi_ref, wuk_ref,
                  qlat_ref, qit_ref, wit_ref, ki_ref, ckv_ref, ckvt_ref):
    tm = h_ref.shape[0]
    xn = _rms(h_ref[...], g_ref[...]).astype(BF16)
    cq = _rms(_dot(xn, wcq_ref[...]), qn_ref[...]).astype(BF16)
    ckv = _rms(_dot(xn, wckv_ref[...]), kvn_ref[...])
    ckv_ref[...] = ckv.astype(BF16)
    for c in range(tm // KEY_TILE):
        ckvt_ref[c] = jnp.concatenate([ckv[c * KEY_TILE:(c + 1) * KEY_TILE, :].T.astype(BF16),
                                       jnp.ones((ONES_ROWS, KEY_TILE), BF16)], axis=0)
    ki_ref[...] = _dot(xn, wki_ref[...]).astype(BF16)
    nq = tm // QBLK
    wit = _dot_nt(wwi_ref[...], xn) * (IDX_HEADS ** -0.5 * IDX_DIM ** -0.5)
    qit = _dot_nt(wqi_ref[...], cq).astype(BF16)
    for c in range(nq):
        wit_ref[c] = wit[:, c * QBLK:(c + 1) * QBLK]
        qit_ref[c] = qit[:, c * QBLK:(c + 1) * QBLK]
    qn = _dot(cq, wuq_ref[...]).astype(BF16)
    for pr in range(AT_HEADS // 2):
        ql = (_dot_nt(wuk_ref[pr], qn[:, pr * 128:(pr + 1) * 128])
              * (AT_DH ** -0.5 * LOG2E)).astype(BF16)
        for c in range(nq):
            qlat_ref[c, 2 * pr] = ql[:KV_LORA, c * QBLK:(c + 1) * QBLK]
            qlat_ref[c, 2 * pr + 1] = ql[KV_LORA:, c * QBLK:(c + 1) * QBLK]


def _at_in(h, g, wcq, wckv, wki, wwi_t, qn, kvn, wuq, wqi_t, wuk_bd_t, tm=512):
    T, D = h.shape
    nk = tm // KEY_TILE
    ins = [h, g, wcq, wckv, wki, wwi_t, qn, kvn, wuq, wqi_t, wuk_bd_t]
    in_specs = [pl.BlockSpec((tm, D), lambda i: (i, 0))] + [_full(a.shape) for a in ins[1:]]
    return pl.pallas_call(
        _at_in_kernel,
        grid=(T // tm,),
        in_specs=in_specs,
        out_specs=[pl.BlockSpec((tm // QBLK, AT_HEADS, KV_LORA, QBLK), lambda i: (i, 0, 0, 0)),
                   pl.BlockSpec((tm // QBLK, IDX_HEADS * IDX_DIM, QBLK), lambda i: (i, 0, 0)),
                   pl.BlockSpec((tm // QBLK, IDX_HEADS, QBLK), lambda i: (i, 0, 0)),
                   pl.BlockSpec((tm, IDX_DIM), lambda i: (i, 0)),
                   pl.BlockSpec((tm, KV_LORA), lambda i: (i, 0)),
                   pl.BlockSpec((nk, KV_LORA + ONES_ROWS, KEY_TILE), lambda i: (i, 0, 0))],
        out_shape=[jax.ShapeDtypeStruct((T // QBLK, AT_HEADS, KV_LORA, QBLK), BF16),
                   jax.ShapeDtypeStruct((T // QBLK, IDX_HEADS * IDX_DIM, QBLK), BF16),
                   jax.ShapeDtypeStruct((T // QBLK, IDX_HEADS, QBLK), F32),
                   jax.ShapeDtypeStruct((T, IDX_DIM), BF16),
                   jax.ShapeDtypeStruct((T, KV_LORA), BF16),
                   jax.ShapeDtypeStruct((T // KEY_TILE, KV_LORA + ONES_ROWS, KEY_TILE), BF16)],
        compiler_params=_cparams(("parallel",)),
        name="at_in",
    )(*ins)


def _at_core_kernel(topk, qlat_ref, qit_ref, wit_ref, ki_ref, ckv_ref, ckvt_ref, bias_ref,
                    wuvt_ref, wo_ref, h_ref, out_ref, ik_ref, planes_ref, mask_ref, acc_ref, ot_ref):
    H, TK = AT_HEADS, KEY_TILE
    assert TK == 32 * 8
    qb = pl.program_id(1)
    t0 = qb * QBLK
    nkt = qb // (TK // QBLK) + 1
    lane_q = lax.broadcasted_iota(I32, (TK, QBLK), 1)
    row_k = lax.broadcasted_iota(I32, (TK, QBLK), 0)

    wit = wit_ref[...]

    def score_tile(kt, carry):
        k0 = pl.multiple_of(kt * TK, TK)
        kk = ki_ref[pl.ds(k0, TK), :]
        acc = jnp.zeros((TK, 2 * QBLK), F32)
        for pr in range(IDX_HEADS // 2):
            qa = qit_ref[(2 * pr) * IDX_DIM:(2 * pr + 1) * IDX_DIM, :]
            qb_ = qit_ref[(2 * pr + 1) * IDX_DIM:(2 * pr + 2) * IDX_DIM, :]
            rel = jnp.maximum(_dot(kk, jnp.concatenate([qa, qb_], axis=1)), 0.0)
            w2 = jnp.concatenate([wit[2 * pr:2 * pr + 1, :], wit[2 * pr + 1:2 * pr + 2, :]], axis=1)
            acc = acc + rel * w2
        score = acc[:, :QBLK] + acc[:, QBLK:] + 0.0
        bits = pltpu.bitcast(score, I32)
        ikey = bits ^ ((bits >> 31) & 0x7FFFFFFF)
        causal = (k0 + row_k) <= (t0 + lane_q)
        xk = jnp.where(causal, ikey, INT_MIN)
        ik_ref[pl.ds(k0, TK), :] = xk
        u = (xk ^ INT_MIN).reshape(TK // 8, 8, QBLK)
        a = [u[v] for v in range(32)]
        m, j = 0x0000FFFF, 16
        while j:
            k = 0
            while k < 32:
                t = (a[k] ^ lax.shift_right_logical(a[k + j], jnp.full_like(a[k], j))) & m
                a[k] = a[k] ^ t
                a[k + j] = a[k + j] ^ (t << j)
                k = (k + j + 1) & ~j
            j >>= 1
            m = m ^ ((m << j) & 0xFFFFFFFF)
        for i in range(32):
            planes_ref[i, pl.ds(kt, 1)] = a[i][None]
        return carry

    @pl.when((pl.program_id(0) == 0) & (qb == 0))
    def _():
        planes_ref[...] = jnp.zeros_like(planes_ref)

    CT = 2 * TK
    nct = (nkt + 1) // 2
    lax.fori_loop(0, nct, lambda c, x: score_tile(2 * c + 1, score_tile(2 * c, x)), 0)

    row_c = lax.broadcasted_iota(I32, (CT, QBLK), 0)

    def count(pred):
        def body(ct, c):
            k0 = pl.multiple_of(ct * CT, CT)
            hit = jnp.where(pred(ik_ref[pl.ds(k0, CT), :], k0 + row_c), 1, 0)
            return c + jnp.sum(hit.reshape(CT // 8, 8, QBLK), axis=0)
        c = lax.fori_loop(0, nct, body, jnp.zeros((8, QBLK), I32))
        return jnp.sum(c, axis=0, keepdims=True)

    NT = planes_ref.shape[1]

    def bit_pass(i, st):
        alive, rem, tau_u = st
        t = alive & planes_ref[i]
        c = jnp.sum(jnp.sum(lax.population_count(t), axis=0), axis=0, keepdims=True)
        take = c >= rem
        return (jnp.where(take, t, alive ^ t), jnp.where(take, rem, rem - c),
                jnp.where(take, tau_u | (1 << (31 - i)), tau_u))

    alive0 = jnp.where(lax.broadcasted_iota(I32, (NT, 8, QBLK), 0) < nkt, -1, 0)
    ties, need, tau_u = lax.fori_loop(0, 32, bit_pass, (alive0, jnp.full((1, QBLK), topk, I32),
                                                        jnp.zeros((1, QBLK), I32)))
    tau = tau_u ^ INT_MIN
    n_ties = jnp.sum(jnp.sum(lax.population_count(ties), axis=0), axis=0, keepdims=True)

    idx_bits = ik_ref.shape[0].bit_length()

    def tie_search(_):
        def step(i, jc):
            cand = jc + (1 << (idx_bits - 1 - i))
            n = count(lambda x, kidx: (x == tau) & (kidx < cand))
            return jnp.where(n <= need, cand, jc)
        return lax.fori_loop(0, idx_bits, step, jnp.zeros((1, QBLK), I32))

    jcut = lax.cond(jnp.max(n_ties - need) > 0, tie_search,
                    lambda _: jnp.full((1, QBLK), 2 ** idx_bits, I32), 0)

    def mask_tile(kt, carry):
        k0 = pl.multiple_of(kt * TK, TK)
        x = ik_ref[pl.ds(k0, TK), :]
        kidx = k0 + row_k
        sel = ((x > tau) | ((x == tau) & (kidx < jcut))) & (kidx <= (t0 + lane_q))
        mask_ref[kt] = jnp.where(sel, 0.0, NEG_BIG)
        return carry

    lax.fori_loop(0, nkt, mask_tile, 0)

    near_key_tiles = -(-NEAR_TILES * QBLK // TK)
    nfar = jnp.maximum(nkt - near_key_tiles, 0)

    NP = H // 2
    acc_ref[...] = jnp.zeros_like(acc_ref)

    def tile(near, kt, m_all):
        k0 = pl.multiple_of(kt * TK, TK)
        am = mask_ref[kt]
        am2 = jnp.concatenate([am, am], axis=1)
        ck = ckv_ref[pl.ds(k0, TK), :]
        ckt = ckvt_ref[kt]
        ms = []
        for pr in range(NP):
            qt = jnp.concatenate([qlat_ref[2 * pr], qlat_ref[2 * pr + 1]], axis=1)
            lg = _dot(ck, qt) + am2
            if near:
                lg = lg + bias_ref[jnp.minimum((t0 - k0) // QBLK, NEAR_TILES), pr]
            m_old = m_all[pr:pr + 1, :]
            m_new = jnp.maximum(m_old, jnp.max(lg, axis=0, keepdims=True))
            p = jnp.exp2((lg - m_new).astype(BF16))
            acc_ref[pr] = acc_ref[pr] * jnp.exp2(m_old - m_new) + _dot(ckt, p)
            ms.append(m_new)
        return jnp.concatenate(ms, axis=0)

    m_all = jnp.full((NP, 2 * QBLK), NEG_BIG, F32)
    m_all = lax.fori_loop(0, nfar // 2, lambda c, m: tile(False, 2 * c + 1, tile(False, 2 * c, m)), m_all)
    m_all = lax.fori_loop(2 * (nfar // 2), nfar, functools.partial(tile, False), m_all)
    m_all = lax.fori_loop(nfar, nkt, functools.partial(tile, True), m_all)
    for pr in range(NP):
        a = acc_ref[pr]
        olat = (a[:KV_LORA] / a[KV_LORA:KV_LORA + 1]).astype(BF16)
        ot_ref[pr * 2 * AT_DH:(pr + 1) * 2 * AT_DH, :] = jnp.concatenate(
            [_dot(wuvt_ref[2 * pr], olat[:, :QBLK]), _dot(wuvt_ref[2 * pr + 1], olat[:, QBLK:])], axis=0)
    out_ref[...] = h_ref[...] + _dot(ot_ref[...].T.astype(BF16), wo_ref[...])


def _at_core(qlat, qit, wit, ki, ckv, ckvt, bias, wuvt, wo, h, B, L, topk):
    T, D = h.shape
    nblk = L // QBLK
    nkt = L // KEY_TILE
    H = AT_HEADS
    assert L % (2 * KEY_TILE) == 0
    return pl.pallas_call(
        functools.partial(_at_core_kernel, topk),
        grid=(B, nblk),
        in_specs=[pl.BlockSpec((None, H, KV_LORA, QBLK), lambda b, q: (b * nblk + q, 0, 0, 0)),
                  pl.BlockSpec((None, IDX_HEADS * IDX_DIM, QBLK), lambda b, q: (b * nblk + q, 0, 0)),
                  pl.BlockSpec((None, IDX_HEADS, QBLK), lambda b, q: (b * nblk + q, 0, 0)),
                  pl.BlockSpec((L, IDX_DIM), lambda b, q: (b, 0)),
                  pl.BlockSpec((L, KV_LORA), lambda b, q: (b, 0)),
                  pl.BlockSpec((nkt, KV_LORA + ONES_ROWS, KEY_TILE), lambda b, q: (b, 0, 0)),
                  _full(bias.shape, single=True), _full(wuvt.shape, single=True),
                  _full(wo.shape, single=True),
                  pl.BlockSpec((QBLK, D), lambda b, q: (b * nblk + q, 0))],
        out_specs=pl.BlockSpec((QBLK, D), lambda b, q: (b * nblk + q, 0)),
        out_shape=jax.ShapeDtypeStruct((T, D), F32),
        scratch_shapes=[pltpu.VMEM((L, QBLK), I32),
                        pltpu.VMEM((32, nkt, 8, QBLK), I32),
                        pltpu.VMEM((nkt, KEY_TILE, QBLK), F32),
                        pltpu.VMEM((H // 2, KV_LORA + ONES_ROWS, 2 * QBLK), F32),
                        pltpu.VMEM((H * AT_DH, QBLK), F32)],
        compiler_params=_cparams(("arbitrary", "arbitrary")),
        name="at_core",
    )(qlat, qit, wit, ki, ckv, ckvt, bias, wuvt, wo, h)


def _rel_bucket(n):
    max_exact = REL_BUCKETS // 2
    nf = jnp.maximum(n, 1).astype(F32)
    large = max_exact + (jnp.log(nf / max_exact) / math.log(REL_MAX_DIST / max_exact)
                         * (REL_BUCKETS - max_exact)).astype(I32)
    large = jnp.minimum(large, REL_BUCKETS - 1)
    return jnp.where(n < max_exact, n, large)


def _bias_tiles(rel_bias):
    H = rel_bias.shape[1]
    span = KEY_TILE + QBLK - 1
    m = jnp.arange(NEAR_TILES * QBLK + span, dtype=I32)
    tab = rel_bias[_rel_bucket(jnp.maximum(m - (KEY_TILE - 1), 0))] - rel_bias[REL_BUCKETS - 1]
    tiles = []
    for d in range(NEAR_TILES):
        u = tab[d * QBLK:d * QBLK + span][::-1]
        r = jnp.tile(u, (KEY_TILE + 1, 1))[:KEY_TILE * (span + 1)].reshape(KEY_TILE, span + 1, H)
        tiles.append(r[:, :QBLK][:, ::-1])
    tiles.append(jnp.zeros_like(tiles[0]))
    t = jnp.stack(tiles).astype(F32) * LOG2E
    t = jnp.transpose(t, (0, 3, 1, 2))
    return jnp.concatenate([t[:, 0::2], t[:, 1::2]], axis=-1)


def kernel(x, p, hg_norm, hg_w_in, hg_lb, hg_onorm, hg_w_out, at_norm, at_w_in, at_q_norm,
           at_kv_norm, at_w_uq, at_w_uk, at_w_uv, at_w_qidx, at_w_out, rel_bias, ff_norm, ff_w_up,
           ff_conv_w, ff_conv_b, ff_w_down, ple_norm, ple_w_gate, ple_w_proj, final_norm):
    B, L, D = x.shape
    T = B * L
    depth = p.shape[0]
    row = lambda a: a.reshape(1, -1)
    h = x.reshape(T, D)
    for i in range(depth):
        j = i // 2
        if i % 2 == 0:
            q, k, v, gs, lf = _hg_in(h, row(hg_norm[j]), hg_lb, hg_w_in[j].astype(BF16), i)
            h = _hg_rec(q, k, v, gs, lf, h, row(hg_onorm[j]), hg_w_out[j].astype(BF16), B, L)
        else:
            w_in = at_w_in[j].astype(BF16)
            o1, o2, o3 = Q_LORA, Q_LORA + KV_LORA, Q_LORA + KV_LORA + IDX_DIM
            wuk = at_w_uk[j].astype(BF16)
            z = jnp.zeros_like(wuk[0::2])
            wuk_bd = jnp.concatenate([jnp.concatenate([wuk[0::2], z], axis=2),
                                      jnp.concatenate([z, wuk[1::2]], axis=2)], axis=1)
            wuvt = jnp.transpose(at_w_uv[j].astype(BF16), (0, 2, 1))
            qlat, qit, wit, ki, ckv, ckvt = _at_in(
                h, row(at_norm[j]), w_in[:, :o1], w_in[:, o1:o2], w_in[:, o2:o3], w_in[:, o3:].T,
                row(at_q_norm[j]), row(at_kv_norm[j]), at_w_uq[j].astype(BF16),
                at_w_qidx[j].astype(BF16).T, jnp.transpose(wuk_bd, (0, 2, 1)))
            topk = max(1, min(TOPK_MAX, L // 4))
            h = _at_core(qlat, qit, wit, ki, ckv, ckvt, _bias_tiles(rel_bias), wuvt,
                         at_w_out[j].astype(BF16), h, B, L, topk)
        h = _ffn_ple(h, row(ff_norm[i]), ff_w_up[i].astype(BF16), ff_conv_w[i], ff_conv_b[i],
                     ff_w_down[i].astype(BF16), p[i].reshape(T, PLE_DIM), row(ple_norm[i]),
                     ple_w_gate[i].astype(BF16), ple_w_proj[i].astype(BF16), row(final_norm),
                     i == depth - 1, L)
    return h.reshape(B, L, D)
```

```python
import functools
import math

import jax
import jax.numpy as jnp
import numpy as np
from jax import lax
from jax.experimental import pallas as pl
from jax.experimental.pallas import tpu as pltpu

F32 = jnp.float32
BF16 = jnp.bfloat16
I32 = jnp.int32

D_MODEL = 1024
HG_HEADS = 8
HG_DK = D_MODEL // HG_HEADS
AT_HEADS = 16
AT_DH = 64
Q_LORA = 384
KV_LORA = 256
IDX_HEADS = 8
IDX_DIM = 64
TOPK_MAX = 256
QBLK = 128
REL_BUCKETS = 32
REL_MAX_DIST = 128
PLE_DIM = 256
EPS = 1e-6

VMEM_LIMIT_V7X = 56 * 1024 * 1024
SUB = 16
KEY_TILE = 256
NEAR_TILES = (KEY_TILE + REL_MAX_DIST) // QBLK
NEG_BIG = -1e30
LOG2E = 1.4426950408889634
ONES_ROWS = 16
INT_MIN = -(2 ** 31)


def _cparams(sem):
    return pltpu.CompilerParams(dimension_semantics=sem, vmem_limit_bytes=VMEM_LIMIT_V7X)


def _rms(x, g):
    return x * lax.rsqrt(jnp.mean(x * x, axis=-1, keepdims=True) + EPS) * g


def _dot(a, b):
    return jnp.dot(a, b, preferred_element_type=F32)


def _dot_nt(a, b):
    return lax.dot_general(a, b, (((1,), (1,)), ((), ())), preferred_element_type=F32)


def _dot_tn(a, b):
    return lax.dot_general(a, b, (((0,), (0,)), ((), ())), preferred_element_type=F32)


def _full(shape, single=False):
    n = len(shape)
    if single:
        return pl.BlockSpec(shape, lambda *_: (0,) * n, pipeline_mode=pl.Buffered(1))
    return pl.BlockSpec(shape, lambda *_: (0,) * n)


def _hg_in_kernel(layer, x_ref, g_ref, lbp_ref, w_ref, q_ref, k_ref, v_ref, gs_ref, lf_ref):
    D = D_MODEL
    xn = _rms(x_ref[...], g_ref[...]).astype(BF16)
    lbp = lbp_ref[...]
    e = jnp.exp(lbp - jnp.max(lbp, axis=0, keepdims=True))
    lb = jnp.sum(e[: layer + 1], axis=0, keepdims=True) / jnp.sum(e, axis=0, keepdims=True)
    q = _dot(xn, w_ref[:, 0:D])
    q_ref[...] = (q * jax.nn.sigmoid(q)).astype(BF16)
    f = lb + (1.0 - lb) * jax.nn.sigmoid(_dot(xn, w_ref[:, D:2 * D]))
    lf_ref[...] = jnp.log(f) * LOG2E
    k_ref[...] = (1.0 - f).astype(BF16)
    v_ref[...] = _dot(xn, w_ref[:, 2 * D:3 * D]).astype(BF16)
    g = _dot(xn, w_ref[:, 3 * D:4 * D])
    gs_ref[...] = (g * jax.nn.sigmoid(g)).astype(BF16)


def _hg_in(x, g, lbp, w, layer, tm=1024):
    T, D = x.shape
    tok = pl.BlockSpec((tm, D), lambda i: (i, 0))
    bf = jax.ShapeDtypeStruct((T, D), BF16)
    return pl.pallas_call(
        functools.partial(_hg_in_kernel, layer),
        grid=(T // tm,),
        in_specs=[tok, _full((1, D)), _full(lbp.shape), _full(w.shape)],
        out_specs=[tok] * 5,
        out_shape=[bf, bf, bf, bf, jax.ShapeDtypeStruct((T, D), F32)],
        compiler_params=_cparams(("parallel",)),
        name="hg_in",
    )(x, g, lbp, w)


def _hg_rec_kernel(lt, q_ref, k_ref, v_ref, gs_ref, lf_ref, x_ref, on_ref, wo_ref, tri_ref,
                   out_ref, st_ref, b_ref, o_ref):
    H, DK = HG_HEADS, HG_DK

    @pl.when(pl.program_id(1) == 0)
    def _():
        st_ref[...] = jnp.zeros_like(st_ref)

    lf = lf_ref[...]
    hi = lf.astype(BF16)
    r1 = lf - hi.astype(F32)
    mid = r1.astype(BF16)
    lo = (r1 - mid.astype(F32)).astype(BF16)
    tri = tri_ref[...]
    b_ref[...] = _dot(tri, hi) + _dot(tri, mid) + _dot(tri, lo)

    HALF = SUB // 2
    row_h = lax.broadcasted_iota(I32, (HALF, DK), 0)

    def sub_chunk(c, carry):
        r0 = pl.multiple_of(c * SUB, SUB)
        for h in range(H):
            ls = slice(h * DK, (h + 1) * DK)
            b = b_ref[pl.ds(r0, SUB), ls]
            q = q_ref[pl.ds(r0, SUB), ls].astype(F32)
            k = k_ref[pl.ds(r0, SUB), ls].astype(F32)
            v = v_ref[pl.ds(r0, SUB), ls].astype(F32)
            be = b[SUB - 1:SUB, :]
            st = st_ref[h]
            o = _dot_nt((q * jnp.exp2(b)).astype(BF16), st.astype(BF16))
            o_top = jnp.zeros((HALF, DK), F32)
            o_bot = jnp.zeros((HALF, DK), F32)
            for s in range(SUB):
                bs, ks, vs = b[s:s + 1, :], k[s:s + 1, :], v[s:s + 1, :]
                if s < HALF:
                    e = jnp.exp2(jnp.where(row_h >= s, b[:HALF] - bs, NEG_BIG))
                    o_top = o_top + jnp.sum(q[:HALF] * e * ks, axis=-1, keepdims=True) * vs
                    e = jnp.exp2(b[HALF:] - bs)
                else:
                    e = jnp.exp2(jnp.where(row_h >= s - HALF, b[HALF:] - bs, NEG_BIG))
                o_bot = o_bot + jnp.sum(q[HALF:] * e * ks, axis=-1, keepdims=True) * vs
            o_ref[pl.ds(r0, SUB), ls] = o + jnp.concatenate([o_top, o_bot], axis=0)
            kd = (k * jnp.exp2(be - b)).astype(BF16)
            st_ref[h] = st * jnp.exp2(be) + _dot_tn(v.astype(BF16), kd)
        return carry

    def four_sub_chunks(c, carry):
        for u in range(4):
            carry = sub_chunk(4 * c + u, carry)
        return carry

    lax.fori_loop(0, lt // (4 * SUB), four_sub_chunks, 0)

    on = on_ref[...]
    for h in range(H):
        ls = slice(h * DK, (h + 1) * DK)
        oh = _rms(o_ref[:, ls], on[:, ls]) * gs_ref[:, ls].astype(F32)
        o_ref[:, ls] = oh
    out_ref[...] = x_ref[...] + _dot(o_ref[...].astype(BF16), wo_ref[...])


def _hg_rec(q, k, v, gs, lf, x, onorm, wo, B, L, lt=256):
    T, D = x.shape
    nl = L // lt
    tok = pl.BlockSpec((lt, D), lambda b, j: (b * nl + j, 0))
    tri = (np.arange(lt)[:, None] >= np.arange(lt)[None, :]) & (
        np.arange(lt)[:, None] // SUB == np.arange(lt)[None, :] // SUB)
    tri = jnp.asarray(tri, BF16)
    return pl.pallas_call(
        functools.partial(_hg_rec_kernel, lt),
        grid=(B, nl),
        in_specs=[tok] * 6 + [_full((1, D)), _full((D, D)), _full((lt, lt))],
        out_specs=tok,
        out_shape=jax.ShapeDtypeStruct((T, D), F32),
        scratch_shapes=[pltpu.VMEM((HG_HEADS, HG_DK, HG_DK), F32),
                        pltpu.VMEM((lt, D), F32),
                        pltpu.VMEM((lt, D), F32)],
        compiler_params=_cparams(("arbitrary", "arbitrary")),
        name="hg_rec",
    )(q, k, v, gs, lf, x, onorm, wo, tri)


def _ffn_kernel(nseq, nf, final, h_ref, g_ref, wg_ref, wv_ref, cwg_ref, cwv_ref, cbg_ref, cbv_ref,
                wd_ref, p_ref, pg_ref, pwg_ref, pwp_ref, fn_ref, out_ref, xn_ref, cg_ref, cv_ref):
    i, j = pl.program_id(0), pl.program_id(1)

    @pl.when(j == 0)
    def _():
        xn_ref[...] = _rms(h_ref[...], g_ref[...]).astype(BF16)
        out_ref[...] = h_ref[...]

    seq_start = (i % nseq) == 0
    xn = xn_ref[...]

    def conv(u, w_ref, b_ref, carry_ref):
        fc = u.shape[1]
        carry = jnp.where(seq_start, 0.0, carry_ref[j])
        carry_ref[j] = u[u.shape[0] - 8:, :]
        r1 = pltpu.roll(u, 1, 0)
        r2 = pltpu.roll(u, 2, 0)
        rows = lax.broadcasted_iota(I32, (8, fc), 0)
        top1 = jnp.where(rows < 1, pltpu.roll(carry, 1, 0), r1[0:8])
        top2 = jnp.where(rows < 2, pltpu.roll(carry, 2, 0), r2[0:8])
        p1 = jnp.concatenate([top1, r1[8:]], axis=0)
        p2 = jnp.concatenate([top2, r2[8:]], axis=0)
        w = w_ref[0]
        return u * w[2:3, :] + p1 * w[1:2, :] + p2 * w[0:1, :] + b_ref[0]

    ug = conv(_dot(xn, wg_ref[...]), cwg_ref, cbg_ref, cg_ref)
    uv = conv(_dot(xn, wv_ref[...]), cwv_ref, cbv_ref, cv_ref)
    act = (ug * jax.nn.sigmoid(ug) * uv).astype(BF16)
    out_ref[...] += _dot(act, wd_ref[...])

    @pl.when(j == nf - 1)
    def _():
        h = out_ref[...]
        gate = jax.nn.sigmoid(_dot(_rms(h, pg_ref[...]).astype(BF16), pwg_ref[...]))
        h = h + gate * _dot(p_ref[...].astype(BF16), pwp_ref[...])
        if final:
            h = _rms(h, fn_ref[...])
        out_ref[...] = h


def _ffn_ple(h, g, w_up, conv_w, conv_b, w_down, p, pg, pwg, pwp, fnorm, final, L, tm=512, nf=2):
    T, D = h.shape
    F = w_down.shape[0]
    fc = F // nf
    assert fc % 128 == 0 and L % tm == 0
    cw = jnp.pad(conv_w, ((0, 5), (0, 0)))
    cwg = cw[:, :F].reshape(8, nf, fc).transpose(1, 0, 2)
    cwv = cw[:, F:].reshape(8, nf, fc).transpose(1, 0, 2)
    cbg = conv_b[:F].reshape(nf, 1, fc)
    cbv = conv_b[F:].reshape(nf, 1, fc)
    tok = pl.BlockSpec((tm, D), lambda i, j: (i, 0))
    return pl.pallas_call(
        functools.partial(_ffn_kernel, L // tm, nf, final),
        grid=(T // tm, nf),
        in_specs=[tok, _full((1, D)),
                  pl.BlockSpec((D, fc), lambda i, j: (0, j)),
                  pl.BlockSpec((D, fc), lambda i, j: (0, nf + j)),
                  pl.BlockSpec((1, 8, fc), lambda i, j: (j, 0, 0)),
                  pl.BlockSpec((1, 8, fc), lambda i, j: (j, 0, 0)),
                  pl.BlockSpec((1, 1, fc), lambda i, j: (j, 0, 0)),
                  pl.BlockSpec((1, 1, fc), lambda i, j: (j, 0, 0)),
                  pl.BlockSpec((fc, D), lambda i, j: (j, 0)),
                  pl.BlockSpec((tm, PLE_DIM), lambda i, j: (i, 0)), _full((1, D)),
                  _full((D, D), single=True), _full((PLE_DIM, D), single=True), _full((1, D))],
        out_specs=tok,
        out_shape=jax.ShapeDtypeStruct((T, D), F32),
        scratch_shapes=[pltpu.VMEM((tm, D), BF16),
                        pltpu.VMEM((nf, 8, fc), F32),
                        pltpu.VMEM((nf, 8, fc), F32)],
        compiler_params=_cparams(("arbitrary", "arbitrary")),
        name="conv_ffn_ple",
    )(h, g, w_up, w_up, cwg, cwv, cbg, cbv, w_down, p, pg, pwg, pwp, fnorm)


def _at_in_kernel(h_ref, g_ref, wcq_ref, wckv_ref, wki_ref, wwi_ref, qn_ref, kvn_ref,
                  wuq_ref, wqi_ref, wuk_ref,
                  qlat_ref, qit_ref, wit_ref, ki_ref, ckv_ref, ckvt_ref):
    tm = h_ref.shape[0]
    xn = _rms(h_ref[...], g_ref[...]).astype(BF16)
    cq = _rms(_dot(xn, wcq_ref[...]), qn_ref[...]).astype(BF16)
    ckv = _rms(_dot(xn, wckv_ref[...]), kvn_ref[...])
    ckv_ref[...] = ckv.astype(BF16)
    for c in range(tm // KEY_TILE):
        ckvt_ref[c] = jnp.concatenate([ckv[c * KEY_TILE:(c + 1) * KEY_TILE, :].T.astype(BF16),
                                       jnp.ones((ONES_ROWS, KEY_TILE), BF16)], axis=0)
    ki_ref[...] = _dot(xn, wki_ref[...]).astype(BF16)
    nq = tm // QBLK
    wit = _dot_nt(wwi_ref[...], xn) * (IDX_HEADS ** -0.5 * IDX_DIM ** -0.5)
    qit = _dot_nt(wqi_ref[...], cq).astype(BF16)
    for c in range(nq):
        wit_ref[c] = wit[:, c * QBLK:(c + 1) * QBLK]
        qit_ref[c] = qit[:, c * QBLK:(c + 1) * QBLK]
    qn = _dot(cq, wuq_ref[...]).astype(BF16)
    for pr in range(AT_HEADS // 2):
        ql = (_dot_nt(wuk_ref[pr], qn[:, pr * 128:(pr + 1) * 128])
              * (AT_DH ** -0.5 * LOG2E)).astype(BF16)
        for c in range(nq):
            qlat_ref[c, 2 * pr] = ql[:KV_LORA, c * QBLK:(c + 1) * QBLK]
            qlat_ref[c, 2 * pr + 1] = ql[KV_LORA:, c * QBLK:(c + 1) * QBLK]


def _at_in(h, g, wcq, wckv, wki, wwi_t, qn, kvn, wuq, wqi_t, wuk_bd_t, tm=512):
    T, D = h.shape
    nk = tm // KEY_TILE
    ins = [h, g, wcq, wckv, wki, wwi_t, qn, kvn, wuq, wqi_t, wuk_bd_t]
    in_specs = [pl.BlockSpec((tm, D), lambda i: (i, 0))] + [_full(a.shape) for a in ins[1:]]
    return pl.pallas_call(
        _at_in_kernel,
        grid=(T // tm,),
        in_specs=in_specs,
        out_specs=[pl.BlockSpec((tm // QBLK, AT_HEADS, KV_LORA, QBLK), lambda i: (i, 0, 0, 0)),
                   pl.BlockSpec((tm // QBLK, IDX_HEADS * IDX_DIM, QBLK), lambda i: (i, 0, 0)),
                   pl.BlockSpec((tm // QBLK, IDX_HEADS, QBLK), lambda i: (i, 0, 0)),
                   pl.BlockSpec((tm, IDX_DIM), lambda i: (i, 0)),
                   pl.BlockSpec((tm, KV_LORA), lambda i: (i, 0)),
                   pl.BlockSpec((nk, KV_LORA + ONES_ROWS, KEY_TILE), lambda i: (i, 0, 0))],
        out_shape=[jax.ShapeDtypeStruct((T // QBLK, AT_HEADS, KV_LORA, QBLK), BF16),
                   jax.ShapeDtypeStruct((T // QBLK, IDX_HEADS * IDX_DIM, QBLK), BF16),
                   jax.ShapeDtypeStruct((T // QBLK, IDX_HEADS, QBLK), F32),
                   jax.ShapeDtypeStruct((T, IDX_DIM), BF16),
                   jax.ShapeDtypeStruct((T, KV_LORA), BF16),
                   jax.ShapeDtypeStruct((T // KEY_TILE, KV_LORA + ONES_ROWS, KEY_TILE), BF16)],
        compiler_params=_cparams(("parallel",)),
        name="at_in",
    )(*ins)


def _at_core_kernel(topk, qlat_ref, qit_ref, wit_ref, ki_ref, ckv_ref, ckvt_ref, bias_ref,
                    wuvt_ref, wo_ref, h_ref, out_ref, ik_ref, planes_ref, mask_ref, acc_ref, ot_ref):
    H, TK = AT_HEADS, KEY_TILE
    assert TK == 32 * 8
    qb = pl.program_id(1)
    t0 = qb * QBLK
    nkt = qb // (TK // QBLK) + 1
    lane_q = lax.broadcasted_iota(I32, (TK, QBLK), 1)
    row_k = lax.broadcasted_iota(I32, (TK, QBLK), 0)

    wit = wit_ref[...]

    def score_tile(kt, carry):
        k0 = pl.multiple_of(kt * TK, TK)
        kk = ki_ref[pl.ds(k0, TK), :]
        acc = jnp.zeros((TK, 2 * QBLK), F32)
        for pr in range(IDX_HEADS // 2):
            qa = qit_ref[(2 * pr) * IDX_DIM:(2 * pr + 1) * IDX_DIM, :]
            qb_ = qit_ref[(2 * pr + 1) * IDX_DIM:(2 * pr + 2) * IDX_DIM, :]
            rel = jnp.maximum(_dot(kk, jnp.concatenate([qa, qb_], axis=1)), 0.0)
            w2 = jnp.concatenate([wit[2 * pr:2 * pr + 1, :], wit[2 * pr + 1:2 * pr + 2, :]], axis=1)
            acc = acc + rel * w2
        score = acc[:, :QBLK] + acc[:, QBLK:] + 0.0
        bits = pltpu.bitcast(score, I32)
        ikey = bits ^ ((bits >> 31) & 0x7FFFFFFF)
        causal = (k0 + row_k) <= (t0 + lane_q)
        xk = jnp.where(causal, ikey, INT_MIN)
        ik_ref[pl.ds(k0, TK), :] = xk
        u = (xk ^ INT_MIN).reshape(TK // 8, 8, QBLK)
        a = [u[v] for v in range(32)]
        m, j = 0x0000FFFF, 16
        while j:
            k = 0
            while k < 32:
                t = (a[k] ^ lax.shift_right_logical(a[k + j], jnp.full_like(a[k], j))) & m
                a[k] = a[k] ^ t
                a[k + j] = a[k + j] ^ (t << j)
                k = (k + j + 1) & ~j
            j >>= 1
            m = m ^ ((m << j) & 0xFFFFFFFF)
        for i in range(32):
            planes_ref[i, pl.ds(kt, 1)] = a[i][None]
        return carry

    @pl.when((pl.program_id(0) == 0) & (qb == 0))
    def _():
        planes_ref[...] = jnp.zeros_like(planes_ref)

    CT = 2 * TK
    nct = (nkt + 1) // 2
    lax.fori_loop(0, nct, lambda c, x: score_tile(2 * c + 1, score_tile(2 * c, x)), 0)

    row_c = lax.broadcasted_iota(I32, (CT, QBLK), 0)

    def count(pred):
        def body(ct, c):
            k0 = pl.multiple_of(ct * CT, CT)
            hit = jnp.where(pred(ik_ref[pl.ds(k0, CT), :], k0 + row_c), 1, 0)
            return c + jnp.sum(hit.reshape(CT // 8, 8, QBLK), axis=0)
        c = lax.fori_loop(0, nct, body, jnp.zeros((8, QBLK), I32))
        return jnp.sum(c, axis=0, keepdims=True)

    NT = planes_ref.shape[1]

    def bit_pass(i, st):
        alive, rem, tau_u = st
        t = alive & planes_ref[i]
        c = jnp.sum(jnp.sum(lax.population_count(t), axis=0), axis=0, keepdims=True)
        take = c >= rem
        return (jnp.where(take, t, alive ^ t), jnp.where(take, rem, rem - c),
                jnp.where(take, tau_u | (1 << (31 - i)), tau_u))

    alive0 = jnp.where(lax.broadcasted_iota(I32, (NT, 8, QBLK), 0) < nkt, -1, 0)
    ties, need, tau_u = lax.fori_loop(0, 32, bit_pass, (alive0, jnp.full((1, QBLK), topk, I32),
                                                        jnp.zeros((1, QBLK), I32)))
    tau = tau_u ^ INT_MIN
    n_ties = jnp.sum(jnp.sum(lax.population_count(ties), axis=0), axis=0, keepdims=True)

    idx_bits = ik_ref.shape[0].bit_length()

    def tie_search(_):
        def step(i, jc):
            cand = jc + (1 << (idx_bits - 1 - i))
            n = count(lambda x, kidx: (x == tau) & (kidx < cand))
            return jnp.where(n <= need, cand, jc)
        return lax.fori_loop(0, idx_bits, step, jnp.zeros((1, QBLK), I32))

    jcut = lax.cond(jnp.max(n_ties - need) > 0, tie_search,
                    lambda _: jnp.full((1, QBLK), 2 ** idx_bits, I32), 0)

    def mask_tile(kt, carry):
        k0 = pl.multiple_of(kt * TK, TK)
        x = ik_ref[pl.ds(k0, TK), :]
        kidx = k0 + row_k
        sel = ((x > tau) | ((x == tau) & (kidx < jcut))) & (kidx <= (t0 + lane_q))
        mask_ref[kt] = jnp.where(sel, 0.0, NEG_BIG)
        return carry

    lax.fori_loop(0, nkt, mask_tile, 0)

    near_key_tiles = -(-NEAR_TILES * QBLK // TK)
    nfar = jnp.maximum(nkt - near_key_tiles, 0)

    NP = H // 2
    acc_ref[...] = jnp.zeros_like(acc_ref)

    def tile(near, kt, m_all):
        k0 = pl.multiple_of(kt * TK, TK)
        am = mask_ref[kt]
        am2 = jnp.concatenate([am, am], axis=1)
        ck = ckv_ref[pl.ds(k0, TK), :]
        ckt = ckvt_ref[kt]
        ms = []
        for pr in range(NP):
            qt = jnp.concatenate([qlat_ref[2 * pr], qlat_ref[2 * pr + 1]], axis=1)
            lg = _dot(ck, qt) + am2
            if near:
                lg = lg + bias_ref[jnp.minimum((t0 - k0) // QBLK, NEAR_TILES), pr]
            m_old = m_all[pr:pr + 1, :]
            m_new = jnp.maximum(m_old, jnp.max(lg, axis=0, keepdims=True))
            p = jnp.exp2((lg - m_new).astype(BF16))
            acc_ref[pr] = acc_ref[pr] * jnp.exp2(m_old - m_new) + _dot(ckt, p)
            ms.append(m_new)
        return jnp.concatenate(ms, axis=0)

    m_all = jnp.full((NP, 2 * QBLK), NEG_BIG, F32)
    m_all = lax.fori_loop(0, nfar // 2, lambda c, m: tile(False, 2 * c + 1, tile(False, 2 * c, m)), m_all)
    m_all = lax.fori_loop(2 * (nfar // 2), nfar, functools.partial(tile, False), m_all)
    m_all = lax.fori_loop(nfar, nkt, functools.partial(tile, True), m_all)
    for pr in range(NP):
        a = acc_ref[pr]
        olat = (a[:KV_LORA] / a[KV_LORA:KV_LORA + 1]).astype(BF16)
        ot_ref[pr * 2 * AT_DH:(pr + 1) * 2 * AT_DH, :] = jnp.concatenate(
            [_dot(wuvt_ref[2 * pr], olat[:, :QBLK]), _dot(wuvt_ref[2 * pr + 1], olat[:, QBLK:])], axis=0)
    out_ref[...] = h_ref[...] + _dot(ot_ref[...].T.astype(BF16), wo_ref[...])


def _at_core(qlat, qit, wit, ki, ckv, ckvt, bias, wuvt, wo, h, B, L, topk):
    T, D = h.shape
    nblk = L // QBLK
    nkt = L // KEY_TILE
    H = AT_HEADS
    assert L % (2 * KEY_TILE) == 0
    return pl.pallas_call(
        functools.partial(_at_core_kernel, topk),
        grid=(B, nblk),
        in_specs=[pl.BlockSpec((None, H, KV_LORA, QBLK), lambda b, q: (b * nblk + q, 0, 0, 0)),
                  pl.BlockSpec((None, IDX_HEADS * IDX_DIM, QBLK), lambda b, q: (b * nblk + q, 0, 0)),
                  pl.BlockSpec((None, IDX_HEADS, QBLK), lambda b, q: (b * nblk + q, 0, 0)),
                  pl.BlockSpec((L, IDX_DIM), lambda b, q: (b, 0)),
                  pl.BlockSpec((L, KV_LORA), lambda b, q: (b, 0)),
                  pl.BlockSpec((nkt, KV_LORA + ONES_ROWS, KEY_TILE), lambda b, q: (b, 0, 0)),
                  _full(bias.shape, single=True), _full(wuvt.shape, single=True),
                  _full(wo.shape, single=True),
                  pl.BlockSpec((QBLK, D), lambda b, q: (b * nblk + q, 0))],
        out_specs=pl.BlockSpec((QBLK, D), lambda b, q: (b * nblk + q, 0)),
        out_shape=jax.ShapeDtypeStruct((T, D), F32),
        scratch_shapes=[pltpu.VMEM((L, QBLK), I32),
                        pltpu.VMEM((32, nkt, 8, QBLK), I32),
                        pltpu.VMEM((nkt, KEY_TILE, QBLK), F32),
                        pltpu.VMEM((H // 2, KV_LORA + ONES_ROWS, 2 * QBLK), F32),
                        pltpu.VMEM((H * AT_DH, QBLK), F32)],
        compiler_params=_cparams(("arbitrary", "arbitrary")),
        name="at_core",
    )(qlat, qit, wit, ki, ckv, ckvt, bias, wuvt, wo, h)


def _rel_bucket(n):
    max_exact = REL_BUCKETS // 2
    nf = jnp.maximum(n, 1).astype(F32)
    large = max_exact + (jnp.log(nf / max_exact) / math.log(REL_MAX_DIST / max_exact)
                         * (REL_BUCKETS - max_exact)).astype(I32)
    large = jnp.minimum(large, REL_BUCKETS - 1)
    return jnp.where(n < max_exact, n, large)


def _bias_tiles(rel_bias):
    H = rel_bias.shape[1]
    span = KEY_TILE + QBLK - 1
    m = jnp.arange(NEAR_TILES * QBLK + span, dtype=I32)
    tab = rel_bias[_rel_bucket(jnp.maximum(m - (KEY_TILE - 1), 0))] - rel_bias[REL_BUCKETS - 1]
    tiles = []
    for d in range(NEAR_TILES):
        u = tab[d * QBLK:d * QBLK + span][::-1]
        r = jnp.tile(u, (KEY_TILE + 1, 1))[:KEY_TILE * (span + 1)].reshape(KEY_TILE, span + 1, H)
        tiles.append(r[:, :QBLK][:, ::-1])
    tiles.append(jnp.zeros_like(tiles[0]))
    t = jnp.stack(tiles).astype(F32) * LOG2E
    t = jnp.transpose(t, (0, 3, 1, 2))
    return jnp.concatenate([t[:, 0::2], t[:, 1::2]], axis=-1)


def kernel(x, p, hg_norm, hg_w_in, hg_lb, hg_onorm, hg_w_out, at_norm, at_w_in, at_q_norm,
           at_kv_norm, at_w_uq, at_w_uk, at_w_uv, at_w_qidx, at_w_out, rel_bias, ff_norm, ff_w_up,
           ff_conv_w, ff_conv_b, ff_w_down, ple_norm, ple_w_gate, ple_w_proj, final_norm):
    B, L, D = x.shape
    T = B * L
    depth = p.shape[0]
    row = lambda a: a.reshape(1, -1)
    h = x.reshape(T, D)
    for i in range(depth):
        j = i // 2
        if i % 2 == 0:
            q, k, v, gs, lf = _hg_in(h, row(hg_norm[j]), hg_lb, hg_w_in[j].astype(BF16), i)
            h = _hg_rec(q, k, v, gs, lf, h, row(hg_onorm[j]), hg_w_out[j].astype(BF16), B, L)
        else:
            w_in = at_w_in[j].astype(BF16)
            o1, o2, o3 = Q_LORA, Q_LORA + KV_LORA, Q_LORA + KV_LORA + IDX_DIM
            wuk = at_w_uk[j].astype(BF16)
            z = jnp.zeros_like(wuk[0::2])
            wuk_bd = jnp.concatenate([jnp.concatenate([wuk[0::2], z], axis=2),
                                      jnp.concatenate([z, wuk[1::2]], axis=2)], axis=1)
            wuvt = jnp.transpose(at_w_uv[j].astype(BF16), (0, 2, 1))
            qlat, qit, wit, ki, ckv, ckvt = _at_in(
                h, row(at_norm[j]), w_in[:, :o1], w_in[:, o1:o2], w_in[:, o2:o3], w_in[:, o3:].T,
                row(at_q_norm[j]), row(at_kv_norm[j]), at_w_uq[j].astype(BF16),
                at_w_qidx[j].astype(BF16).T, jnp.transpose(wuk_bd, (0, 2, 1)))
            topk = max(1, min(TOPK_MAX, L // 4))
            h = _at_core(qlat, qit, wit, ki, ckv, ckvt, _bias_tiles(rel_bias), wuvt,
                         at_w_out[j].astype(BF16), h, B, L, topk)
        h = _ffn_ple(h, row(ff_norm[i]), ff_w_up[i].astype(BF16), ff_conv_w[i], ff_conv_b[i],
                     ff_w_down[i].astype(BF16), p[i].reshape(T, PLE_DIM), row(ple_norm[i]),
                     ple_w_gate[i].astype(BF16), ple_w_proj[i].astype(BF16), row(final_norm),
                     i == depth - 1, L)
    return h.reshape(B, L, D)
```

```python
import functools
import math

import jax
import jax.numpy as jnp
import numpy as np
from jax import lax
from jax.experimental import pallas as pl
from jax.experimental.pallas import tpu as pltpu

F32 = jnp.float32
BF16 = jnp.bfloat16
I32 = jnp.int32

D_MODEL = 1024
HG_HEADS = 8
HG_DK = D_MODEL // HG_HEADS
AT_HEADS = 16
AT_DH = 64
Q_LORA = 384
KV_LORA = 256
IDX_HEADS = 8
IDX_DIM = 64
TOPK_MAX = 256
QBLK = 128
REL_BUCKETS = 32
REL_MAX_DIST = 128
PLE_DIM = 256
EPS = 1e-6

VMEM_LIMIT_V7X = 56 * 1024 * 1024
SUB = 16
KEY_TILE = 256
NEAR_TILES = (KEY_TILE + REL_MAX_DIST) // QBLK
NEG_BIG = -1e30
LOG2E = 1.4426950408889634
ONES_ROWS = 16
INT_MIN = -(2 ** 31)


def _cparams(sem):
    return pltpu.CompilerParams(dimension_semantics=sem, vmem_limit_bytes=VMEM_LIMIT_V7X)


def _rms(x, g):
    return x * lax.rsqrt(jnp.mean(x * x, axis=-1, keepdims=True) + EPS) * g


def _dot(a, b):
    return jnp.dot(a, b, preferred_element_type=F32)


def _dot_nt(a, b):
    return lax.dot_general(a, b, (((1,), (1,)), ((), ())), preferred_element_type=F32)


def _dot_tn(a, b):
    return lax.dot_general(a, b, (((0,), (0,)), ((), ())), preferred_element_type=F32)


def _full(shape, single=False):
    n = len(shape)
    if single:
        return pl.BlockSpec(shape, lambda *_: (0,) * n, pipeline_mode=pl.Buffered(1))
    return pl.BlockSpec(shape, lambda *_: (0,) * n)


def _hg_in_kernel(layer, x_ref, g_ref, lbp_ref, w_ref, q_ref, k_ref, v_ref, gs_ref, lf_ref):
    D = D_MODEL
    xn = _rms(x_ref[...], g_ref[...]).astype(BF16)
    lbp = lbp_ref[...]
    e = jnp.exp(lbp - jnp.max(lbp, axis=0, keepdims=True))
    lb = jnp.sum(e[: layer + 1], axis=0, keepdims=True) / jnp.sum(e, axis=0, keepdims=True)
    q = _dot(xn, w_ref[:, 0:D])
    q_ref[...] = (q * jax.nn.sigmoid(q)).astype(BF16)
    f = lb + (1.0 - lb) * jax.nn.sigmoid(_dot(xn, w_ref[:, D:2 * D]))
    lf_ref[...] = jnp.log(f) * LOG2E
    k_ref[...] = (1.0 - f).astype(BF16)
    v_ref[...] = _dot(xn, w_ref[:, 2 * D:3 * D]).astype(BF16)
    g = _dot(xn, w_ref[:, 3 * D:4 * D])
    gs_ref[...] = (g * jax.nn.sigmoid(g)).astype(BF16)


def _hg_in(x, g, lbp, w, layer, tm=1024):
    T, D = x.shape
    tok = pl.BlockSpec((tm, D), lambda i: (i, 0))
    bf = jax.ShapeDtypeStruct((T, D), BF16)
    return pl.pallas_call(
        functools.partial(_hg_in_kernel, layer),
        grid=(T // tm,),
        in_specs=[tok, _full((1, D)), _full(lbp.shape), _full(w.shape)],
        out_specs=[tok] * 5,
        out_shape=[bf, bf, bf, bf, jax.ShapeDtypeStruct((T, D), F32)],
        compiler_params=_cparams(("parallel",)),
        name="hg_in",
    )(x, g, lbp, w)


def _hg_rec_kernel(lt, q_ref, k_ref, v_ref, gs_ref, lf_ref, x_ref, on_ref, wo_ref, tri_ref,
                   out_ref, st_ref, b_ref, o_ref):
    H, DK = HG_HEADS, HG_DK

    @pl.when(pl.program_id(1) == 0)
    def _():
        st_ref[...] = jnp.zeros_like(st_ref)

    lf = lf_ref[...]
    hi = lf.astype(BF16)
    r1 = lf - hi.astype(F32)
    mid = r1.astype(BF16)
    lo = (r1 - mid.astype(F32)).astype(BF16)
    tri = tri_ref[...]
    b_ref[...] = _dot(tri, hi) + _dot(tri, mid) + _dot(tri, lo)

    HALF = SUB // 2
    row_h = lax.broadcasted_iota(I32, (HALF, DK), 0)

    def sub_chunk(c, carry):
        r0 = pl.multiple_of(c * SUB, SUB)
        for h in range(H):
            ls = slice(h * DK, (h + 1) * DK)
            b = b_ref[pl.ds(r0, SUB), ls]
            q = q_ref[pl.ds(r0, SUB), ls].astype(F32)
            k = k_ref[pl.ds(r0, SUB), ls].astype(F32)
            v = v_ref[pl.ds(r0, SUB), ls].astype(F32)
            be = b[SUB - 1:SUB, :]
            st = st_ref[h]
            o = _dot_nt((q * jnp.exp2(b)).astype(BF16), st.astype(BF16))
            o_top = jnp.zeros((HALF, DK), F32)
            o_bot = jnp.zeros((HALF, DK), F32)
            for s in range(SUB):
                bs, ks, vs = b[s:s + 1, :], k[s:s + 1, :], v[s:s + 1, :]
                if s < HALF:
                    e = jnp.exp2(jnp.where(row_h >= s, b[:HALF] - bs, NEG_BIG))
                    o_top = o_top + jnp.sum(q[:HALF] * e * ks, axis=-1, keepdims=True) * vs
                    e = jnp.exp2(b[HALF:] - bs)
                else:
                    e = jnp.exp2(jnp.where(row_h >= s - HALF, b[HALF:] - bs, NEG_BIG))
                o_bot = o_bot + jnp.sum(q[HALF:] * e * ks, axis=-1, keepdims=True) * vs
            o_ref[pl.ds(r0, SUB), ls] = o + jnp.concatenate([o_top, o_bot], axis=0)
            kd = (k * jnp.exp2(be - b)).astype(BF16)
            st_ref[h] = st * jnp.exp2(be) + _dot_tn(v.astype(BF16), kd)
        return carry

    def four_sub_chunks(c, carry):
        for u in range(4):
            carry = sub_chunk(4 * c + u, carry)
        return carry

    lax.fori_loop(0, lt // (4 * SUB), four_sub_chunks, 0)

    on = on_ref[...]
    for h in range(H):
        ls = slice(h * DK, (h + 1) * DK)
        oh = _rms(o_ref[:, ls], on[:, ls]) * gs_ref[:, ls].astype(F32)
        o_ref[:, ls] = oh
    out_ref[...] = x_ref[...] + _dot(o_ref[...].astype(BF16), wo_ref[...])


def _hg_rec(q, k, v, gs, lf, x, onorm, wo, B, L, lt=256):
    T, D = x.shape
    nl = L // lt
    tok = pl.BlockSpec((lt, D), lambda b, j: (b * nl + j, 0))
    tri = (np.arange(lt)[:, None] >= np.arange(lt)[None, :]) & (
        np.arange(lt)[:, None] // SUB == np.arange(lt)[None, :] // SUB)
    tri = jnp.asarray(tri, BF16)
    return pl.pallas_call(
        functools.partial(_hg_rec_kernel, lt),
        grid=(B, nl),
        in_specs=[tok] * 6 + [_full((1, D)), _full((D, D)), _full((lt, lt))],
        out_specs=tok,
        out_shape=jax.ShapeDtypeStruct((T, D), F32),
        scratch_shapes=[pltpu.VMEM((HG_HEADS, HG_DK, HG_DK), F32),
                        pltpu.VMEM((lt, D), F32),
                        pltpu.VMEM((lt, D), F32)],
        compiler_params=_cparams(("arbitrary", "arbitrary")),
        name="hg_rec",
    )(q, k, v, gs, lf, x, onorm, wo, tri)


def _ffn_kernel(nseq, nf, final, h_ref, g_ref, wg_ref, wv_ref, cwg_ref, cwv_ref, cbg_ref, cbv_ref,
                wd_ref, p_ref, pg_ref, pwg_ref, pwp_ref, fn_ref, out_ref, xn_ref, cg_ref, cv_ref):
    i, j = pl.program_id(0), pl.program_id(1)

    @pl.when(j == 0)
    def _():
        xn_ref[...] = _rms(h_ref[...], g_ref[...]).astype(BF16)
        out_ref[...] = h_ref[...]

    seq_start = (i % nseq) == 0
    xn = xn_ref[...]

    def conv(u, w_ref, b_ref, carry_ref):
        fc = u.shape[1]
        carry = jnp.where(seq_start, 0.0, carry_ref[j])
        carry_ref[j] = u[u.shape[0] - 8:, :]
        r1 = pltpu.roll(u, 1, 0)
        r2 = pltpu.roll(u, 2, 0)
        rows = lax.broadcasted_iota(I32, (8, fc), 0)
        top1 = jnp.where(rows < 1, pltpu.roll(carry, 1, 0), r1[0:8])
        top2 = jnp.where(rows < 2, pltpu.roll(carry, 2, 0), r2[0:8])
        p1 = jnp.concatenate([top1, r1[8:]], axis=0)
        p2 = jnp.concatenate([top2, r2[8:]], axis=0)
        w = w_ref[0]
        return u * w[2:3, :] + p1 * w[1:2, :] + p2 * w[0:1, :] + b_ref[0]

    ug = conv(_dot(xn, wg_ref[...]), cwg_ref, cbg_ref, cg_ref)
    uv = conv(_dot(xn, wv_ref[...]), cwv_ref, cbv_ref, cv_ref)
    act = (ug * jax.nn.sigmoid(ug) * uv).astype(BF16)
    out_ref[...] += _dot(act, wd_ref[...])

    @pl.when(j == nf - 1)
    def _():
        h = out_ref[...]
        gate = jax.nn.sigmoid(_dot(_rms(h, pg_ref[...]).astype(BF16), pwg_ref[...]))
        h = h + gate * _dot(p_ref[...].astype(BF16), pwp_ref[...])
        if final:
            h = _rms(h, fn_ref[...])
        out_ref[...] = h


def _ffn_ple(h, g, w_up, conv_w, conv_b, w_down, p, pg, pwg, pwp, fnorm, final, L, tm=512, nf=2):
    T, D = h.shape
    F = w_down.shape[0]
    fc = F // nf
    assert fc % 128 == 0 and L % tm == 0
    cw = jnp.pad(conv_w, ((0, 5), (0, 0)))
    cwg = cw[:, :F].reshape(8, nf, fc).transpose(1, 0, 2)
    cwv = cw[:, F:].reshape(8, nf, fc).transpose(1, 0, 2)
    cbg = conv_b[:F].reshape(nf, 1, fc)
    cbv = conv_b[F:].reshape(nf, 1, fc)
    tok = pl.BlockSpec((tm, D), lambda i, j: (i, 0))
    return pl.pallas_call(
        functools.partial(_ffn_kernel, L // tm, nf, final),
        grid=(T // tm, nf),
        in_specs=[tok, _full((1, D)),
                  pl.BlockSpec((D, fc), lambda i, j: (0, j)),
                  pl.BlockSpec((D, fc), lambda i, j: (0, nf + j)),
                  pl.BlockSpec((1, 8, fc), lambda i, j: (j, 0, 0)),
                  pl.BlockSpec((1, 8, fc), lambda i, j: (j, 0, 0)),
                  pl.BlockSpec((1, 1, fc), lambda i, j: (j, 0, 0)),
                  pl.BlockSpec((1, 1, fc), lambda i, j: (j, 0, 0)),
                  pl.BlockSpec((fc, D), lambda i, j: (j, 0)),
                  pl.BlockSpec((tm, PLE_DIM), lambda i, j: (i, 0)), _full((1, D)),
                  _full((D, D), single=True), _full((PLE_DIM, D), single=True), _full((1, D))],
        out_specs=tok,
        out_shape=jax.ShapeDtypeStruct((T, D), F32),
        scratch_shapes=[pltpu.VMEM((tm, D), BF16),
                        pltpu.VMEM((nf, 8, fc), F32),
                        pltpu.VMEM((nf, 8, fc), F32)],
        compiler_params=_cparams(("arbitrary", "arbitrary")),
        name="conv_ffn_ple",
    )(h, g, w_up, w_up, cwg, cwv, cbg, cbv, w_down, p, pg, pwg, pwp, fnorm)


def _at_in_kernel(h_ref, g_ref, wcq_ref, wckv_ref, wki_ref, wwi_ref, qn_ref, kvn_ref,
                  wuq_ref, wqi_ref, wuk_ref,
                  qlat_ref, qit_ref, wit_ref, ki_ref, ckv_ref, ckvt_ref):
    tm = h_ref.shape[0]
    xn = _rms(h_ref[...], g_ref[...]).astype(BF16)
    cq = _rms(_dot(xn, wcq_ref[...]), qn_ref[...]).astype(BF16)
    ckv = _rms(_dot(xn, wckv_ref[...]), kvn_ref[...])
    ckv_ref[...] = ckv.astype(BF16)
    for c in range(tm // KEY_TILE):
        ckvt_ref[c] = jnp.concatenate([ckv[c * KEY_TILE:(c + 1) * KEY_TILE, :].T.astype(BF16),
                                       jnp.ones((ONES_ROWS, KEY_TILE), BF16)], axis=0)
    ki_ref[...] = _dot(xn, wki_ref[...]).astype(BF16)
    nq = tm // QBLK
    wit = _dot_nt(wwi_ref[...], xn) * (IDX_HEADS ** -0.5 * IDX_DIM ** -0.5)
    qit = _dot_nt(wqi_ref[...], cq).astype(BF16)
    for c in range(nq):
        wit_ref[c] = wit[:, c * QBLK:(c + 1) * QBLK]
        qit_ref[c] = qit[:, c * QBLK:(c + 1) * QBLK]
    qn = _dot(cq, wuq_ref[...]).astype(BF16)
    for pr in range(AT_HEADS // 2):
        ql = (_dot_nt(wuk_ref[pr], qn[:, pr * 128:(pr + 1) * 128])
              * (AT_DH ** -0.5 * LOG2E)).astype(BF16)
        for c in range(nq):
            qlat_ref[c, 2 * pr] = ql[:KV_LORA, c * QBLK:(c + 1) * QBLK]
            qlat_ref[c, 2 * pr + 1] = ql[KV_LORA:, c * QBLK:(c + 1) * QBLK]


def _at_in(h, g, wcq, wckv, wki, wwi_t, qn, kvn, wuq, wqi_t, wuk_bd_t, tm=512):
    T, D = h.shape
    nk = tm // KEY_TILE
    ins = [h, g, wcq, wckv, wki, wwi_t, qn, kvn, wuq, wqi_t, wuk_bd_t]
    in_specs = [pl.BlockSpec((tm, D), lambda i: (i, 0))] + [_full(a.shape) for a in ins[1:]]
    return pl.pallas_call(
        _at_in_kernel,
        grid=(T // tm,),
        in_specs=in_specs,
        out_specs=[pl.BlockSpec((tm // QBLK, AT_HEADS, KV_LORA, QBLK), lambda i: (i, 0, 0, 0)),
                   pl.BlockSpec((tm // QBLK, IDX_HEADS * IDX_DIM, QBLK), lambda i: (i, 0, 0)),
                   pl.BlockSpec((tm // QBLK, IDX_HEADS, QBLK), lambda i: (i, 0, 0)),
                   pl.BlockSpec((tm, IDX_DIM), lambda i: (i, 0)),
                   pl.BlockSpec((tm, KV_LORA), lambda i: (i, 0)),
                   pl.BlockSpec((nk, KV_LORA + ONES_ROWS, KEY_TILE), lambda i: (i, 0, 0))],
        out_shape=[jax.ShapeDtypeStruct((T // QBLK, AT_HEADS, KV_LORA, QBLK), BF16),
                   jax.ShapeDtypeStruct((T // QBLK, IDX_HEADS * IDX_DIM, QBLK), BF16),
                   jax.ShapeDtypeStruct((T // QBLK, IDX_HEADS, QBLK), F32),
                   jax.ShapeDtypeStruct((T, IDX_DIM), BF16),
                   jax.ShapeDtypeStruct((T, KV_LORA), BF16),
                   jax.ShapeDtypeStruct((T // KEY_TILE, KV_LORA + ONES_ROWS, KEY_TILE), BF16)],
        compiler_params=_cparams(("parallel",)),
        name="at_in",
    )(*ins)


def _at_core_kernel(topk, qlat_ref, qit_ref, wit_ref, ki_ref, ckv_ref, ckvt_ref, bias_ref,
                    wuvt_ref, wo_ref, h_ref, out_ref, ik_ref, planes_ref, mask_ref, acc_ref, ot_ref):
    H, TK = AT_HEADS, KEY_TILE
    assert TK == 32 * 8
    qb = pl.program_id(1)
    t0 = qb * QBLK
    nkt = qb // (TK // QBLK) + 1
    lane_q = lax.broadcasted_iota(I32, (TK, QBLK), 1)
    row_k = lax.broadcasted_iota(I32, (TK, QBLK), 0)

    wit = wit_ref[...]

    def score_tile(kt, carry):
        k0 = pl.multiple_of(kt * TK, TK)
        kk = ki_ref[pl.ds(k0, TK), :]
        acc = jnp.zeros((TK, 2 * QBLK), F32)
        for pr in range(IDX_HEADS // 2):
            qa = qit_ref[(2 * pr) * IDX_DIM:(2 * pr + 1) * IDX_DIM, :]
            qb_ = qit_ref[(2 * pr + 1) * IDX_DIM:(2 * pr + 2) * IDX_DIM, :]
            rel = jnp.maximum(_dot(kk, jnp.concatenate([qa, qb_], axis=1)), 0.0)
            w2 = jnp.concatenate([wit[2 * pr:2 * pr + 1, :], wit[2 * pr + 1:2 * pr + 2, :]], axis=1)
            acc = acc + rel * w2
        score = acc[:, :QBLK] + acc[:, QBLK:] + 0.0
        bits = pltpu.bitcast(score, I32)
        ikey = bits ^ ((bits >> 31) & 0x7FFFFFFF)
        causal = (k0 + row_k) <= (t0 + lane_q)
        xk = jnp.where(causal, ikey, INT_MIN)
        ik_ref[pl.ds(k0, TK), :] = xk
        u = (xk ^ INT_MIN).reshape(TK // 8, 8, QBLK)
        a = [u[v] for v in range(32)]
        m, j = 0x0000FFFF, 16
        while j:
            k = 0
            while k < 32:
                t = (a[k] ^ lax.shift_right_logical(a[k + j], jnp.full_like(a[k], j))) & m
                a[k] = a[k] ^ t
                a[k + j] = a[k + j] ^ (t << j)
                k = (k + j + 1) & ~j
            j >>= 1
            m = m ^ ((m << j) & 0xFFFFFFFF)
        for i in range(32):
            planes_ref[i, pl.ds(kt, 1)] = a[i][None]
        return carry

    @pl.when((pl.program_id(0) == 0) & (qb == 0))
    def _():
        planes_ref[...] = jnp.zeros_like(planes_ref)

    CT = 2 * TK
    nct = (nkt + 1) // 2
    lax.fori_loop(0, nct, lambda c, x: score_tile(2 * c + 1, score_tile(2 * c, x)), 0)

    row_c = lax.broadcasted_iota(I32, (CT, QBLK), 0)

    def count(pred):
        def body(ct, c):
            k0 = pl.multiple_of(ct * CT, CT)
            hit = jnp.where(pred(ik_ref[pl.ds(k0, CT), :], k0 + row_c), 1, 0)
            return c + jnp.sum(hit.reshape(CT // 8, 8, QBLK), axis=0)
        c = lax.fori_loop(0, nct, body, jnp.zeros((8, QBLK), I32))
        return jnp.sum(c, axis=0, keepdims=True)

    NT = planes_ref.shape[1]

    def bit_pass(i, st):
        alive, rem, tau_u = st
        t = alive & planes_ref[i]
        c = jnp.sum(jnp.sum(lax.population_count(t), axis=0), axis=0, keepdims=True)
        take = c >= rem
        return (jnp.where(take, t, alive ^ t), jnp.where(take, rem, rem - c),
                jnp.where(take, tau_u | (1 << (31 - i)), tau_u))

    alive0 = jnp.where(lax.broadcasted_iota(I32, (NT, 8, QBLK), 0) < nkt, -1, 0)
    ties, need, tau_u = lax.fori_loop(0, 32, bit_pass, (alive0, jnp.full((1, QBLK), topk, I32),
                                                        jnp.zeros((1, QBLK), I32)))
    tau = tau_u ^ INT_MIN
    n_ties = jnp.sum(jnp.sum(lax.population_count(ties), axis=0), axis=0, keepdims=True)

    idx_bits = ik_ref.shape[0].bit_length()

    def tie_search(_):
        def step(i, jc):
            cand = jc + (1 << (idx_bits - 1 - i))
            n = count(lambda x, kidx: (x == tau) & (kidx < cand))
            return jnp.where(n <= need, cand, jc)
        return lax.fori_loop(0, idx_bits, step, jnp.zeros((1, QBLK), I32))

    jcut = lax.cond(jnp.max(n_ties - need) > 0, tie_search,
                    lambda _: jnp.full((1, QBLK), 2 ** idx_bits, I32), 0)

    def mask_tile(kt, carry):
        k0 = pl.multiple_of(kt * TK, TK)
        x = ik_ref[pl.ds(k0, TK), :]
        kidx = k0 + row_k
        sel = ((x > tau) | ((x == tau) & (kidx < jcut))) & (kidx <= (t0 + lane_q))
        mask_ref[kt] = jnp.where(sel, 0.0, NEG_BIG)
        return carry

    lax.fori_loop(0, nkt, mask_tile, 0)

    near_key_tiles = -(-NEAR_TILES * QBLK // TK)
    nfar = jnp.maximum(nkt - near_key_tiles, 0)

    NP = H // 2
    acc_ref[...] = jnp.zeros_like(acc_ref)

    def tile(near, kt, m_all):
        k0 = pl.multiple_of(kt * TK, TK)
        am = mask_ref[kt]
        am2 = jnp.concatenate([am, am], axis=1)
        ck = ckv_ref[pl.ds(k0, TK), :]
        ckt = ckvt_ref[kt]
        ms = []
        for pr in range(NP):
            qt = jnp.concatenate([qlat_ref[2 * pr], qlat_ref[2 * pr + 1]], axis=1)
            lg = _dot(ck, qt) + am2
            if near:
                lg = lg + bias_ref[jnp.minimum((t0 - k0) // QBLK, NEAR_TILES), pr]
            m_old = m_all[pr:pr + 1, :]
            m_new = jnp.maximum(m_old, jnp.max(lg, axis=0, keepdims=True))
            p = jnp.exp2((lg - m_new).astype(BF16))
            acc_ref[pr] = acc_ref[pr] * jnp.exp2(m_old - m_new) + _dot(ckt, p)
            ms.append(m_new)
        return jnp.concatenate(ms, axis=0)

    m_all = jnp.full((NP, 2 * QBLK), NEG_BIG, F32)
    m_all = lax.fori_loop(0, nfar // 2, lambda c, m: tile(False, 2 * c + 1, tile(False, 2 * c, m)), m_all)
    m_all = lax.fori_loop(2 * (nfar // 2), nfar, functools.partial(tile, False), m_all)
    nn2 = (nkt - nfar) // 2
    m_all = lax.fori_loop(0, nn2, lambda c, m: tile(True, nfar + 2 * c + 1, tile(True, nfar + 2 * c, m)),
                          m_all)
    m_all = lax.fori_loop(nfar + 2 * nn2, nkt, functools.partial(tile, True), m_all)
    for pr in range(NP):
        a = acc_ref[pr]
        olat = (a[:KV_LORA] / a[KV_LORA:KV_LORA + 1]).astype(BF16)
        ot_ref[pr * 2 * AT_DH:(pr + 1) * 2 * AT_DH, :] = jnp.concatenate(
            [_dot(wuvt_ref[2 * pr], olat[:, :QBLK]), _dot(wuvt_ref[2 * pr + 1], olat[:, QBLK:])], axis=0)
    out_ref[...] = h_ref[...] + _dot(ot_ref[...].T.astype(BF16), wo_ref[...])


def _at_core(qlat, qit, wit, ki, ckv, ckvt, bias, wuvt, wo, h, B, L, topk):
    T, D = h.shape
    nblk = L // QBLK
    nkt = L // KEY_TILE
    H = AT_HEADS
    assert L % (2 * KEY_TILE) == 0
    return pl.pallas_call(
        functools.partial(_at_core_kernel, topk),
        grid=(B, nblk),
        in_specs=[pl.BlockSpec((None, H, KV_LORA, QBLK), lambda b, q: (b * nblk + q, 0, 0, 0)),
                  pl.BlockSpec((None, IDX_HEADS * IDX_DIM, QBLK), lambda b, q: (b * nblk + q, 0, 0)),
                  pl.BlockSpec((None, IDX_HEADS, QBLK), lambda b, q: (b * nblk + q, 0, 0)),
                  pl.BlockSpec((L, IDX_DIM), lambda b, q: (b, 0)),
                  pl.BlockSpec((L, KV_LORA), lambda b, q: (b, 0)),
                  pl.BlockSpec((nkt, KV_LORA + ONES_ROWS, KEY_TILE), lambda b, q: (b, 0, 0)),
                  _full(bias.shape, single=True), _full(wuvt.shape, single=True),
                  _full(wo.shape, single=True),
                  pl.BlockSpec((QBLK, D), lambda b, q: (b * nblk + q, 0))],
        out_specs=pl.BlockSpec((QBLK, D), lambda b, q: (b * nblk + q, 0)),
        out_shape=jax.ShapeDtypeStruct((T, D), F32),
        scratch_shapes=[pltpu.VMEM((L, QBLK), I32),
                        pltpu.VMEM((32, nkt, 8, QBLK), I32),
                        pltpu.VMEM((nkt, KEY_TILE, QBLK), F32),
                        pltpu.VMEM((H // 2, KV_LORA + ONES_ROWS, 2 * QBLK), F32),
                        pltpu.VMEM((H * AT_DH, QBLK), F32)],
        compiler_params=_cparams(("arbitrary", "arbitrary")),
        name="at_core",
    )(qlat, qit, wit, ki, ckv, ckvt, bias, wuvt, wo, h)


def _rel_bucket(n):
    max_exact = REL_BUCKETS // 2
    nf = jnp.maximum(n, 1).astype(F32)
    large = max_exact + (jnp.log(nf / max_exact) / math.log(REL_MAX_DIST / max_exact)
                         * (REL_BUCKETS - max_exact)).astype(I32)
    large = jnp.minimum(large, REL_BUCKETS - 1)
    return jnp.where(n < max_exact, n, large)


def _bias_tiles(rel_bias):
    H = rel_bias.shape[1]
    span = KEY_TILE + QBLK - 1
    m = jnp.arange(NEAR_TILES * QBLK + span, dtype=I32)
    tab = rel_bias[_rel_bucket(jnp.maximum(m - (KEY_TILE - 1), 0))] - rel_bias[REL_BUCKETS - 1]
    tiles = []
    for d in range(NEAR_TILES):
        u = tab[d * QBLK:d * QBLK + span][::-1]
        r = jnp.tile(u, (KEY_TILE + 1, 1))[:KEY_TILE * (span + 1)].reshape(KEY_TILE, span + 1, H)
        tiles.append(r[:, :QBLK][:, ::-1])
    tiles.append(jnp.zeros_like(tiles[0]))
    t = jnp.stack(tiles).astype(F32) * LOG2E
    t = jnp.transpose(t, (0, 3, 1, 2))
    return jnp.concatenate([t[:, 0::2], t[:, 1::2]], axis=-1)


def kernel(x, p, hg_norm, hg_w_in, hg_lb, hg_onorm, hg_w_out, at_norm, at_w_in, at_q_norm,
           at_kv_norm, at_w_uq, at_w_uk, at_w_uv, at_w_qidx, at_w_out, rel_bias, ff_norm, ff_w_up,
           ff_conv_w, ff_conv_b, ff_w_down, ple_norm, ple_w_gate, ple_w_proj, final_norm):
    B, L, D = x.shape
    T = B * L
    depth = p.shape[0]
    row = lambda a: a.reshape(1, -1)
    h = x.reshape(T, D)
    for i in range(depth):
        j = i // 2
        if i % 2 == 0:
            q, k, v, gs, lf = _hg_in(h, row(hg_norm[j]), hg_lb, hg_w_in[j].astype(BF16), i)
            h = _hg_rec(q, k, v, gs, lf, h, row(hg_onorm[j]), hg_w_out[j].astype(BF16), B, L)
        else:
            w_in = at_w_in[j].astype(BF16)
            o1, o2, o3 = Q_LORA, Q_LORA + KV_LORA, Q_LORA + KV_LORA + IDX_DIM
            wuk = at_w_uk[j].astype(BF16)
            z = jnp.zeros_like(wuk[0::2])
            wuk_bd = jnp.concatenate([jnp.concatenate([wuk[0::2], z], axis=2),
                                      jnp.concatenate([z, wuk[1::2]], axis=2)], axis=1)
            wuvt = jnp.transpose(at_w_uv[j].astype(BF16), (0, 2, 1))
            qlat, qit, wit, ki, ckv, ckvt = _at_in(
                h, row(at_norm[j]), w_in[:, :o1], w_in[:, o1:o2], w_in[:, o2:o3], w_in[:, o3:].T,
                row(at_q_norm[j]), row(at_kv_norm[j]), at_w_uq[j].astype(BF16),
                at_w_qidx[j].astype(BF16).T, jnp.transpose(wuk_bd, (0, 2, 1)))
            topk = max(1, min(TOPK_MAX, L // 4))
            h = _at_core(qlat, qit, wit, ki, ckv, ckvt, _bias_tiles(rel_bias), wuvt,
                         at_w_out[j].astype(BF16), h, B, L, topk)
        h = _ffn_ple(h, row(ff_norm[i]), ff_w_up[i].astype(BF16), ff_conv_w[i], ff_conv_b[i],
                     ff_w_down[i].astype(BF16), p[i].reshape(T, PLE_DIM), row(ple_norm[i]),
                     ple_w_gate[i].astype(BF16), ple_w_proj[i].astype(BF16), row(final_norm),
                     i == depth - 1, L)
    return h.reshape(B, L, D)
```

```python
import functools
import math

import jax
import jax.numpy as jnp
import numpy as np
from jax import lax
from jax.experimental import pallas as pl
from jax.experimental.pallas import tpu as pltpu

F32 = jnp.float32
BF16 = jnp.bfloat16
I32 = jnp.int32

D_MODEL = 1024
HG_HEADS = 8
HG_DK = D_MODEL // HG_HEADS
AT_HEADS = 16
AT_DH = 64
Q_LORA = 384
KV_LORA = 256
IDX_HEADS = 8
IDX_DIM = 64
TOPK_MAX = 256
QBLK = 128
REL_BUCKETS = 32
REL_MAX_DIST = 128
PLE_DIM = 256
EPS = 1e-6

VMEM_LIMIT_V7X = 56 * 1024 * 1024
SUB = 16
KEY_TILE = 256
NEAR_TILES = (KEY_TILE + REL_MAX_DIST) // QBLK
NEG_BIG = -1e30
LOG2E = 1.4426950408889634
ONES_ROWS = 16
INT_MIN = -(2 ** 31)


def _cparams(sem):
    return pltpu.CompilerParams(dimension_semantics=sem, vmem_limit_bytes=VMEM_LIMIT_V7X)


def _rms(x, g):
    return x * lax.rsqrt(jnp.mean(x * x, axis=-1, keepdims=True) + EPS) * g


def _dot(a, b):
    return jnp.dot(a, b, preferred_element_type=F32)


def _dot_nt(a, b):
    return lax.dot_general(a, b, (((1,), (1,)), ((), ())), preferred_element_type=F32)


def _dot_tn(a, b):
    return lax.dot_general(a, b, (((0,), (0,)), ((), ())), preferred_element_type=F32)


def _full(shape, single=False):
    n = len(shape)
    if single:
        return pl.BlockSpec(shape, lambda *_: (0,) * n, pipeline_mode=pl.Buffered(1))
    return pl.BlockSpec(shape, lambda *_: (0,) * n)


def _hg_in_kernel(layer, x_ref, g_ref, lbp_ref, w_ref, q_ref, k_ref, v_ref, gs_ref, lf_ref):
    D = D_MODEL
    xn = _rms(x_ref[...], g_ref[...]).astype(BF16)
    lbp = lbp_ref[...]
    e = jnp.exp(lbp - jnp.max(lbp, axis=0, keepdims=True))
    lb = jnp.sum(e[: layer + 1], axis=0, keepdims=True) / jnp.sum(e, axis=0, keepdims=True)
    q = _dot(xn, w_ref[:, 0:D])
    q_ref[...] = (q * jax.nn.sigmoid(q)).astype(BF16)
    f = lb + (1.0 - lb) * jax.nn.sigmoid(_dot(xn, w_ref[:, D:2 * D]))
    lf_ref[...] = jnp.log(f) * LOG2E
    k_ref[...] = (1.0 - f).astype(BF16)
    v_ref[...] = _dot(xn, w_ref[:, 2 * D:3 * D]).astype(BF16)
    g = _dot(xn, w_ref[:, 3 * D:4 * D])
    gs_ref[...] = (g * jax.nn.sigmoid(g)).astype(BF16)


def _hg_in(x, g, lbp, w, layer, tm=1024):
    T, D = x.shape
    tok = pl.BlockSpec((tm, D), lambda i: (i, 0))
    bf = jax.ShapeDtypeStruct((T, D), BF16)
    return pl.pallas_call(
        functools.partial(_hg_in_kernel, layer),
        grid=(T // tm,),
        in_specs=[tok, _full((1, D)), _full(lbp.shape), _full(w.shape)],
        out_specs=[tok] * 5,
        out_shape=[bf, bf, bf, bf, jax.ShapeDtypeStruct((T, D), F32)],
        compiler_params=_cparams(("parallel",)),
        name="hg_in",
    )(x, g, lbp, w)


def _hg_rec_kernel(lt, q_ref, k_ref, v_ref, gs_ref, lf_ref, x_ref, on_ref, wo_ref, tri_ref,
                   out_ref, st_ref, b_ref, o_ref):
    H, DK = HG_HEADS, HG_DK

    @pl.when(pl.program_id(1) == 0)
    def _():
        st_ref[...] = jnp.zeros_like(st_ref)

    lf = lf_ref[...]
    hi = lf.astype(BF16)
    r1 = lf - hi.astype(F32)
    mid = r1.astype(BF16)
    lo = (r1 - mid.astype(F32)).astype(BF16)
    tri = tri_ref[...]
    b_ref[...] = _dot(tri, hi) + _dot(tri, mid) + _dot(tri, lo)

    HALF = SUB // 2
    row_h = lax.broadcasted_iota(I32, (HALF, DK), 0)
    lane_h = lax.broadcasted_iota(I32, (HALF, DK), 1)

    def sub_chunk(c, carry):
        r0 = pl.multiple_of(c * SUB, SUB)
        for h in range(H):
            ls = slice(h * DK, (h + 1) * DK)
            b = b_ref[pl.ds(r0, SUB), ls]
            q = q_ref[pl.ds(r0, SUB), ls].astype(F32)
            k = k_ref[pl.ds(r0, SUB), ls].astype(F32)
            v = v_ref[pl.ds(r0, SUB), ls].astype(F32)
            be = b[SUB - 1:SUB, :]
            st = st_ref[h]
            o = _dot_nt((q * jnp.exp2(b)).astype(BF16), st.astype(BF16))
            a_top = jnp.zeros((HALF, DK), F32)
            a_bot = jnp.zeros((HALF, DK), F32)
            for s in range(SUB):
                bs, ks = b[s:s + 1, :], k[s:s + 1, :]
                if s < HALF:
                    e = jnp.exp2(jnp.where(row_h >= s, b[:HALF] - bs, NEG_BIG))
                    a_top = jnp.where(lane_h == s, jnp.sum(q[:HALF] * e * ks, axis=-1, keepdims=True), a_top)
                    e = jnp.exp2(b[HALF:] - bs)
                else:
                    e = jnp.exp2(jnp.where(row_h >= s - HALF, b[HALF:] - bs, NEG_BIG))
                a_bot = jnp.where(lane_h == s, jnp.sum(q[HALF:] * e * ks, axis=-1, keepdims=True), a_bot)
            att = jnp.concatenate([a_top, a_bot], axis=0)[:, :SUB].astype(BF16)
            o_ref[pl.ds(r0, SUB), ls] = o + _dot(att, v.astype(BF16))
            kd = (k * jnp.exp2(be - b)).astype(BF16)
            st_ref[h] = st * jnp.exp2(be) + _dot_tn(v.astype(BF16), kd)
        return carry

    def four_sub_chunks(c, carry):
        for u in range(4):
            carry = sub_chunk(4 * c + u, carry)
        return carry

    lax.fori_loop(0, lt // (4 * SUB), four_sub_chunks, 0)

    on = on_ref[...]
    for h in range(H):
        ls = slice(h * DK, (h + 1) * DK)
        oh = _rms(o_ref[:, ls], on[:, ls]) * gs_ref[:, ls].astype(F32)
        o_ref[:, ls] = oh
    out_ref[...] = x_ref[...] + _dot(o_ref[...].astype(BF16), wo_ref[...])


def _hg_rec(q, k, v, gs, lf, x, onorm, wo, B, L, lt=256):
    T, D = x.shape
    nl = L // lt
    tok = pl.BlockSpec((lt, D), lambda b, j: (b * nl + j, 0))
    tri = (np.arange(lt)[:, None] >= np.arange(lt)[None, :]) & (
        np.arange(lt)[:, None] // SUB == np.arange(lt)[None, :] // SUB)
    tri = jnp.asarray(tri, BF16)
    return pl.pallas_call(
        functools.partial(_hg_rec_kernel, lt),
        grid=(B, nl),
        in_specs=[tok] * 6 + [_full((1, D)), _full((D, D)), _full((lt, lt))],
        out_specs=tok,
        out_shape=jax.ShapeDtypeStruct((T, D), F32),
        scratch_shapes=[pltpu.VMEM((HG_HEADS, HG_DK, HG_DK), F32),
                        pltpu.VMEM((lt, D), F32),
                        pltpu.VMEM((lt, D), F32)],
        compiler_params=_cparams(("arbitrary", "arbitrary")),
        name="hg_rec",
    )(q, k, v, gs, lf, x, onorm, wo, tri)


def _ffn_kernel(nseq, nf, final, h_ref, g_ref, wg_ref, wv_ref, cwg_ref, cwv_ref, cbg_ref, cbv_ref,
                wd_ref, p_ref, pg_ref, pwg_ref, pwp_ref, fn_ref, out_ref, xn_ref, cg_ref, cv_ref):
    i, j = pl.program_id(0), pl.program_id(1)

    @pl.when(j == 0)
    def _():
        xn_ref[...] = _rms(h_ref[...], g_ref[...]).astype(BF16)
        out_ref[...] = h_ref[...]

    seq_start = (i % nseq) == 0
    xn = xn_ref[...]

    def conv(u, w_ref, b_ref, carry_ref):
        fc = u.shape[1]
        carry = jnp.where(seq_start, 0.0, carry_ref[j])
        carry_ref[j] = u[u.shape[0] - 8:, :]
        r1 = pltpu.roll(u, 1, 0)
        r2 = pltpu.roll(u, 2, 0)
        rows = lax.broadcasted_iota(I32, (8, fc), 0)
        top1 = jnp.where(rows < 1, pltpu.roll(carry, 1, 0), r1[0:8])
        top2 = jnp.where(rows < 2, pltpu.roll(carry, 2, 0), r2[0:8])
        p1 = jnp.concatenate([top1, r1[8:]], axis=0)
        p2 = jnp.concatenate([top2, r2[8:]], axis=0)
        w = w_ref[0]
        return u * w[2:3, :] + p1 * w[1:2, :] + p2 * w[0:1, :] + b_ref[0]

    ug = conv(_dot(xn, wg_ref[...]), cwg_ref, cbg_ref, cg_ref)
    uv = conv(_dot(xn, wv_ref[...]), cwv_ref, cbv_ref, cv_ref)
    act = (ug * jax.nn.sigmoid(ug) * uv).astype(BF16)
    out_ref[...] += _dot(act, wd_ref[...])

    @pl.when(j == nf - 1)
    def _():
        h = out_ref[...]
        gate = jax.nn.sigmoid(_dot(_rms(h, pg_ref[...]).astype(BF16), pwg_ref[...]))
        h = h + gate * _dot(p_ref[...].astype(BF16), pwp_ref[...])
        if final:
            h = _rms(h, fn_ref[...])
        out_ref[...] = h


def _ffn_ple(h, g, w_up, conv_w, conv_b, w_down, p, pg, pwg, pwp, fnorm, final, L, tm=512, nf=2):
    T, D = h.shape
    F = w_down.shape[0]
    fc = F // nf
    assert fc % 128 == 0 and L % tm == 0
    cw = jnp.pad(conv_w, ((0, 5), (0, 0)))
    cwg = cw[:, :F].reshape(8, nf, fc).transpose(1, 0, 2)
    cwv = cw[:, F:].reshape(8, nf, fc).transpose(1, 0, 2)
    cbg = conv_b[:F].reshape(nf, 1, fc)
    cbv = conv_b[F:].reshape(nf, 1, fc)
    tok = pl.BlockSpec((tm, D), lambda i, j: (i, 0))
    return pl.pallas_call(
        functools.partial(_ffn_kernel, L // tm, nf, final),
        grid=(T // tm, nf),
        in_specs=[tok, _full((1, D)),
                  pl.BlockSpec((D, fc), lambda i, j: (0, j)),
                  pl.BlockSpec((D, fc), lambda i, j: (0, nf + j)),
                  pl.BlockSpec((1, 8, fc), lambda i, j: (j, 0, 0)),
                  pl.BlockSpec((1, 8, fc), lambda i, j: (j, 0, 0)),
                  pl.BlockSpec((1, 1, fc), lambda i, j: (j, 0, 0)),
                  pl.BlockSpec((1, 1, fc), lambda i, j: (j, 0, 0)),
                  pl.BlockSpec((fc, D), lambda i, j: (j, 0)),
                  pl.BlockSpec((tm, PLE_DIM), lambda i, j: (i, 0)), _full((1, D)),
                  _full((D, D), single=True), _full((PLE_DIM, D), single=True), _full((1, D))],
        out_specs=tok,
        out_shape=jax.ShapeDtypeStruct((T, D), F32),
        scratch_shapes=[pltpu.VMEM((tm, D), BF16),
                        pltpu.VMEM((nf, 8, fc), F32),
                        pltpu.VMEM((nf, 8, fc), F32)],
        compiler_params=_cparams(("arbitrary", "arbitrary")),
        name="conv_ffn_ple",
    )(h, g, w_up, w_up, cwg, cwv, cbg, cbv, w_down, p, pg, pwg, pwp, fnorm)


def _at_in_kernel(h_ref, g_ref, wcq_ref, wckv_ref, wki_ref, wwi_ref, qn_ref, kvn_ref,
                  wuq_ref, wqi_ref, wuk_ref,
                  qlat_ref, qit_ref, wit_ref, ki_ref, ckv_ref, ckvt_ref):
    tm = h_ref.shape[0]
    xn = _rms(h_ref[...], g_ref[...]).astype(BF16)
    cq = _rms(_dot(xn, wcq_ref[...]), qn_ref[...]).astype(BF16)
    ckv = _rms(_dot(xn, wckv_ref[...]), kvn_ref[...])
    ckv_ref[...] = ckv.astype(BF16)
    for c in range(tm // KEY_TILE):
        ckvt_ref[c] = jnp.concatenate([ckv[c * KEY_TILE:(c + 1) * KEY_TILE, :].T.astype(BF16),
                                       jnp.ones((ONES_ROWS, KEY_TILE), BF16)], axis=0)
    ki_ref[...] = _dot(xn, wki_ref[...]).astype(BF16)
    nq = tm // QBLK
    wit = _dot_nt(wwi_ref[...], xn) * (IDX_HEADS ** -0.5 * IDX_DIM ** -0.5)
    qit = _dot_nt(wqi_ref[...], cq).astype(BF16)
    for c in range(nq):
        wit_ref[c] = wit[:, c * QBLK:(c + 1) * QBLK]
        qit_ref[c] = qit[:, c * QBLK:(c + 1) * QBLK]
    qn = _dot(cq, wuq_ref[...]).astype(BF16)
    for pr in range(AT_HEADS // 2):
        ql = (_dot_nt(wuk_ref[pr], qn[:, pr * 128:(pr + 1) * 128])
              * (AT_DH ** -0.5 * LOG2E)).astype(BF16)
        for c in range(nq):
            qlat_ref[c, 2 * pr] = ql[:KV_LORA, c * QBLK:(c + 1) * QBLK]
            qlat_ref[c, 2 * pr + 1] = ql[KV_LORA:, c * QBLK:(c + 1) * QBLK]


def _at_in(h, g, wcq, wckv, wki, wwi_t, qn, kvn, wuq, wqi_t, wuk_bd_t, tm=512):
    T, D = h.shape
    nk = tm // KEY_TILE
    ins = [h, g, wcq, wckv, wki, wwi_t, qn, kvn, wuq, wqi_t, wuk_bd_t]
    in_specs = [pl.BlockSpec((tm, D), lambda i: (i, 0))] + [_full(a.shape) for a in ins[1:]]
    return pl.pallas_call(
        _at_in_kernel,
        grid=(T // tm,),
        in_specs=in_specs,
        out_specs=[pl.BlockSpec((tm // QBLK, AT_HEADS, KV_LORA, QBLK), lambda i: (i, 0, 0, 0)),
                   pl.BlockSpec((tm // QBLK, IDX_HEADS * IDX_DIM, QBLK), lambda i: (i, 0, 0)),
                   pl.BlockSpec((tm // QBLK, IDX_HEADS, QBLK), lambda i: (i, 0, 0)),
                   pl.BlockSpec((tm, IDX_DIM), lambda i: (i, 0)),
                   pl.BlockSpec((tm, KV_LORA), lambda i: (i, 0)),
                   pl.BlockSpec((nk, KV_LORA + ONES_ROWS, KEY_TILE), lambda i: (i, 0, 0))],
        out_shape=[jax.ShapeDtypeStruct((T // QBLK, AT_HEADS, KV_LORA, QBLK), BF16),
                   jax.ShapeDtypeStruct((T // QBLK, IDX_HEADS * IDX_DIM, QBLK), BF16),
                   jax.ShapeDtypeStruct((T // QBLK, IDX_HEADS, QBLK), F32),
                   jax.ShapeDtypeStruct((T, IDX_DIM), BF16),
                   jax.ShapeDtypeStruct((T, KV_LORA), BF16),
                   jax.ShapeDtypeStruct((T // KEY_TILE, KV_LORA + ONES_ROWS, KEY_TILE), BF16)],
        compiler_params=_cparams(("parallel",)),
        name="at_in",
    )(*ins)


def _at_core_kernel(topk, qlat_ref, qit_ref, wit_ref, ki_ref, ckv_ref, ckvt_ref, bias_ref,
                    wuvt_ref, wo_ref, h_ref, out_ref, ik_ref, planes_ref, mask_ref, acc_ref, ot_ref):
    H, TK = AT_HEADS, KEY_TILE
    assert TK == 32 * 8
    qb = pl.program_id(1)
    t0 = qb * QBLK
    nkt = qb // (TK // QBLK) + 1
    lane_q = lax.broadcasted_iota(I32, (TK, QBLK), 1)
    row_k = lax.broadcasted_iota(I32, (TK, QBLK), 0)

    wit = wit_ref[...]

    def score_tile(kt, carry):
        k0 = pl.multiple_of(kt * TK, TK)
        kk = ki_ref[pl.ds(k0, TK), :]
        acc = jnp.zeros((TK, 2 * QBLK), F32)
        for pr in range(IDX_HEADS // 2):
            qa = qit_ref[(2 * pr) * IDX_DIM:(2 * pr + 1) * IDX_DIM, :]
            qb_ = qit_ref[(2 * pr + 1) * IDX_DIM:(2 * pr + 2) * IDX_DIM, :]
            rel = jnp.maximum(_dot(kk, jnp.concatenate([qa, qb_], axis=1)), 0.0)
            w2 = jnp.concatenate([wit[2 * pr:2 * pr + 1, :], wit[2 * pr + 1:2 * pr + 2, :]], axis=1)
            acc = acc + rel * w2
        score = acc[:, :QBLK] + acc[:, QBLK:] + 0.0
        bits = pltpu.bitcast(score, I32)
        ikey = bits ^ ((bits >> 31) & 0x7FFFFFFF)
        causal = (k0 + row_k) <= (t0 + lane_q)
        xk = jnp.where(causal, ikey, INT_MIN)
        ik_ref[pl.ds(k0, TK), :] = xk
        u = (xk ^ INT_MIN).reshape(TK // 8, 8, QBLK)
        a = [u[v] for v in range(32)]
        m, j = 0x0000FFFF, 16
        while j:
            k = 0
            while k < 32:
                t = (a[k] ^ lax.shift_right_logical(a[k + j], jnp.full_like(a[k], j))) & m
                a[k] = a[k] ^ t
                a[k + j] = a[k + j] ^ (t << j)
                k = (k + j + 1) & ~j
            j >>= 1
            m = m ^ ((m << j) & 0xFFFFFFFF)
        for i in range(32):
            planes_ref[i, pl.ds(kt, 1)] = a[i][None]
        return carry

    @pl.when((pl.program_id(0) == 0) & (qb == 0))
    def _():
        planes_ref[...] = jnp.zeros_like(planes_ref)

    CT = 2 * TK
    nct = (nkt + 1) // 2
    lax.fori_loop(0, nct, lambda c, x: score_tile(2 * c + 1, score_tile(2 * c, x)), 0)

    row_c = lax.broadcasted_iota(I32, (CT, QBLK), 0)

    def count(pred):
        def body(ct, c):
            k0 = pl.multiple_of(ct * CT, CT)
            hit = jnp.where(pred(ik_ref[pl.ds(k0, CT), :], k0 + row_c), 1, 0)
            return c + jnp.sum(hit.reshape(CT // 8, 8, QBLK), axis=0)
        c = lax.fori_loop(0, nct, body, jnp.zeros((8, QBLK), I32))
        return jnp.sum(c, axis=0, keepdims=True)

    NT = planes_ref.shape[1]

    def bit_pass(i, st):
        alive, rem, tau_u = st
        t = alive & planes_ref[i]
        c = jnp.sum(jnp.sum(lax.population_count(t), axis=0), axis=0, keepdims=True)
        take = c >= rem
        return (jnp.where(take, t, alive ^ t), jnp.where(take, rem, rem - c),
                jnp.where(take, tau_u | (1 << (31 - i)), tau_u))

    alive0 = jnp.where(lax.broadcasted_iota(I32, (NT, 8, QBLK), 0) < nkt, -1, 0)
    ties, need, tau_u = lax.fori_loop(0, 32, bit_pass, (alive0, jnp.full((1, QBLK), topk, I32),
                                                        jnp.zeros((1, QBLK), I32)))
    tau = tau_u ^ INT_MIN
    n_ties = jnp.sum(jnp.sum(lax.population_count(ties), axis=0), axis=0, keepdims=True)

    idx_bits = ik_ref.shape[0].bit_length()

    def tie_search(_):
        def step(i, jc):
            cand = jc + (1 << (idx_bits - 1 - i))
            n = count(lambda x, kidx: (x == tau) & (kidx < cand))
            return jnp.where(n <= need, cand, jc)
        return lax.fori_loop(0, idx_bits, step, jnp.zeros((1, QBLK), I32))

    jcut = lax.cond(jnp.max(n_ties - need) > 0, tie_search,
                    lambda _: jnp.full((1, QBLK), 2 ** idx_bits, I32), 0)

    def mask_tile(kt, carry):
        k0 = pl.multiple_of(kt * TK, TK)
        x = ik_ref[pl.ds(k0, TK), :]
        kidx = k0 + row_k
        sel = ((x > tau) | ((x == tau) & (kidx < jcut))) & (kidx <= (t0 + lane_q))
        mask_ref[kt] = jnp.where(sel, 0.0, NEG_BIG)
        return carry

    lax.fori_loop(0, nkt, mask_tile, 0)

    near_key_tiles = -(-NEAR_TILES * QBLK // TK)
    nfar = jnp.maximum(nkt - near_key_tiles, 0)

    NP = H // 2
    acc_ref[...] = jnp.zeros_like(acc_ref)

    def tile(near, kt, m_all):
        k0 = pl.multiple_of(kt * TK, TK)
        am = mask_ref[kt]
        am2 = jnp.concatenate([am, am], axis=1)
        ck = ckv_ref[pl.ds(k0, TK), :]
        ckt = ckvt_ref[kt]
        ms = []
        for pr in range(NP):
            qt = jnp.concatenate([qlat_ref[2 * pr], qlat_ref[2 * pr + 1]], axis=1)
            lg = _dot(ck, qt) + am2
            if near:
                lg = lg + bias_ref[jnp.minimum((t0 - k0) // QBLK, NEAR_TILES), pr]
            m_old = m_all[pr:pr + 1, :]
            m_new = jnp.maximum(m_old, jnp.max(lg, axis=0, keepdims=True))
            p = jnp.exp2((lg - m_new).astype(BF16))
            acc_ref[pr] = acc_ref[pr] * jnp.exp2(m_old - m_new) + _dot(ckt, p)
            ms.append(m_new)
        return jnp.concatenate(ms, axis=0)

    m_all = jnp.full((NP, 2 * QBLK), NEG_BIG, F32)
    m_all = lax.fori_loop(0, nfar // 2, lambda c, m: tile(False, 2 * c + 1, tile(False, 2 * c, m)), m_all)
    m_all = lax.fori_loop(2 * (nfar // 2), nfar, functools.partial(tile, False), m_all)
    nn2 = (nkt - nfar) // 2
    m_all = lax.fori_loop(0, nn2, lambda c, m: tile(True, nfar + 2 * c + 1, tile(True, nfar + 2 * c, m)),
                          m_all)
    m_all = lax.fori_loop(nfar + 2 * nn2, nkt, functools.partial(tile, True), m_all)
    for pr in range(NP):
        a = acc_ref[pr]
        olat = (a[:KV_LORA] / a[KV_LORA:KV_LORA + 1]).astype(BF16)
        ot_ref[pr * 2 * AT_DH:(pr + 1) * 2 * AT_DH, :] = jnp.concatenate(
            [_dot(wuvt_ref[2 * pr], olat[:, :QBLK]), _dot(wuvt_ref[2 * pr + 1], olat[:, QBLK:])], axis=0)
    out_ref[...] = h_ref[...] + _dot(ot_ref[...].T.astype(BF16), wo_ref[...])


def _at_core(qlat, qit, wit, ki, ckv, ckvt, bias, wuvt, wo, h, B, L, topk):
    T, D = h.shape
    nblk = L // QBLK
    nkt = L // KEY_TILE
    H = AT_HEADS
    assert L % (2 * KEY_TILE) == 0
    return pl.pallas_call(
        functools.partial(_at_core_kernel, topk),
        grid=(B, nblk),
        in_specs=[pl.BlockSpec((None, H, KV_LORA, QBLK), lambda b, q: (b * nblk + q, 0, 0, 0)),
                  pl.BlockSpec((None, IDX_HEADS * IDX_DIM, QBLK), lambda b, q: (b * nblk + q, 0, 0)),
                  pl.BlockSpec((None, IDX_HEADS, QBLK), lambda b, q: (b * nblk + q, 0, 0)),
                  pl.BlockSpec((L, IDX_DIM), lambda b, q: (b, 0)),
                  pl.BlockSpec((L, KV_LORA), lambda b, q: (b, 0)),
                  pl.BlockSpec((nkt, KV_LORA + ONES_ROWS, KEY_TILE), lambda b, q: (b, 0, 0)),
                  _full(bias.shape, single=True), _full(wuvt.shape, single=True),
                  _full(wo.shape, single=True),
                  pl.BlockSpec((QBLK, D), lambda b, q: (b * nblk + q, 0))],
        out_specs=pl.BlockSpec((QBLK, D), lambda b, q: (b * nblk + q, 0)),
        out_shape=jax.ShapeDtypeStruct((T, D), F32),
        scratch_shapes=[pltpu.VMEM((L, QBLK), I32),
                        pltpu.VMEM((32, nkt, 8, QBLK), I32),
                        pltpu.VMEM((nkt, KEY_TILE, QBLK), F32),
                        pltpu.VMEM((H // 2, KV_LORA + ONES_ROWS, 2 * QBLK), F32),
                        pltpu.VMEM((H * AT_DH, QBLK), F32)],
        compiler_params=_cparams(("arbitrary", "arbitrary")),
        name="at_core",
    )(qlat, qit, wit, ki, ckv, ckvt, bias, wuvt, wo, h)


def _rel_bucket(n):
    max_exact = REL_BUCKETS // 2
    nf = jnp.maximum(n, 1).astype(F32)
    large = max_exact + (jnp.log(nf / max_exact) / math.log(REL_MAX_DIST / max_exact)
                         * (REL_BUCKETS - max_exact)).astype(I32)
    large = jnp.minimum(large, REL_BUCKETS - 1)
    return jnp.where(n < max_exact, n, large)


def _bias_tiles(rel_bias):
    H = rel_bias.shape[1]
    span = KEY_TILE + QBLK - 1
    m = jnp.arange(NEAR_TILES * QBLK + span, dtype=I32)
    tab = rel_bias[_rel_bucket(jnp.maximum(m - (KEY_TILE - 1), 0))] - rel_bias[REL_BUCKETS - 1]
    tiles = []
    for d in range(NEAR_TILES):
        u = tab[d * QBLK:d * QBLK + span][::-1]
        r = jnp.tile(u, (KEY_TILE + 1, 1))[:KEY_TILE * (span + 1)].reshape(KEY_TILE, span + 1, H)
        tiles.append(r[:, :QBLK][:, ::-1])
    tiles.append(jnp.zeros_like(tiles[0]))
    t = jnp.stack(tiles).astype(F32) * LOG2E
    t = jnp.transpose(t, (0, 3, 1, 2))
    return jnp.concatenate([t[:, 0::2], t[:, 1::2]], axis=-1)


def kernel(x, p, hg_norm, hg_w_in, hg_lb, hg_onorm, hg_w_out, at_norm, at_w_in, at_q_norm,
           at_kv_norm, at_w_uq, at_w_uk, at_w_uv, at_w_qidx, at_w_out, rel_bias, ff_norm, ff_w_up,
           ff_conv_w, ff_conv_b, ff_w_down, ple_norm, ple_w_gate, ple_w_proj, final_norm):
    B, L, D = x.shape
    T = B * L
    depth = p.shape[0]
    row = lambda a: a.reshape(1, -1)
    h = x.reshape(T, D)
    for i in range(depth):
        j = i // 2
        if i % 2 == 0:
            q, k, v, gs, lf = _hg_in(h, row(hg_norm[j]), hg_lb, hg_w_in[j].astype(BF16), i)
            h = _hg_rec(q, k, v, gs, lf, h, row(hg_onorm[j]), hg_w_out[j].astype(BF16), B, L)
        else:
            w_in = at_w_in[j].astype(BF16)
            o1, o2, o3 = Q_LORA, Q_LORA + KV_LORA, Q_LORA + KV_LORA + IDX_DIM
            wuk = at_w_uk[j].astype(BF16)
            z = jnp.zeros_like(wuk[0::2])
            wuk_bd = jnp.concatenate([jnp.concatenate([wuk[0::2], z], axis=2),
                                      jnp.concatenate([z, wuk[1::2]], axis=2)], axis=1)
            wuvt = jnp.transpose(at_w_uv[j].astype(BF16), (0, 2, 1))
            qlat, qit, wit, ki, ckv, ckvt = _at_in(
                h, row(at_norm[j]), w_in[:, :o1], w_in[:, o1:o2], w_in[:, o2:o3], w_in[:, o3:].T,
                row(at_q_norm[j]), row(at_kv_norm[j]), at_w_uq[j].astype(BF16),
                at_w_qidx[j].astype(BF16).T, jnp.transpose(wuk_bd, (0, 2, 1)))
            topk = max(1, min(TOPK_MAX, L // 4))
            h = _at_core(qlat, qit, wit, ki, ckv, ckvt, _bias_tiles(rel_bias), wuvt,
                         at_w_out[j].astype(BF16), h, B, L, topk)
        h = _ffn_ple(h, row(ff_norm[i]), ff_w_up[i].astype(BF16), ff_conv_w[i], ff_conv_b[i],
                     ff_w_down[i].astype(BF16), p[i].reshape(T, PLE_DIM), row(ple_norm[i]),
                     ple_w_gate[i].astype(BF16), ple_w_proj[i].astype(BF16), row(final_norm),
                     i == depth - 1, L)
    return h.reshape(B, L, D)
```
